```python
import jax, jax.numpy as jnp
from jax import lax
import numpy as np

D_MODEL = 2048
BATCH = 2
SEQ = 4096
DEPTH = 1
DEC_BATCH = 8
DEC_SEQ = 32
PAST_LEN = 2048

CHUNK = 64
MIX_WIDTH = D_MODEL
CONV_WIDTH = MIX_WIDTH // 2
ATTN_WIDTH = MIX_WIDTH - CONV_WIDTH
HEAD_DIM = 128
N_HEADS = ATTN_WIDTH // HEAD_DIM
CONV_KERNEL = 31
CONV_STATE = CONV_KERNEL - 1
N_GROUPS = 4
EXPERTS_PER_GROUP = 8
N_EXPERTS = N_GROUPS * EXPERTS_PER_GROUP
TOP_K = 2
D_EXPERT = D_MODEL // 4
PLE_DIM = 256
Q_BLOCK = 128
MOE_BLOCK = 128
EPS = 1e-6
IN_COLS = 2 * CONV_WIDTH + 3 * ATTN_WIDTH + N_HEADS

kernel_name = "hymba_conformer_fox_hmoe_stream_step"


def rmsnorm(x, g):
    xf = x.astype(jnp.float32)
    y = xf * lax.rsqrt(jnp.mean(xf * xf, axis=-1, keepdims=True) + EPS)
    return (y * g.astype(jnp.float32)).astype(x.dtype)


def layernorm(x, g, b):
    xf = x.astype(jnp.float32)
    mu = jnp.mean(xf, axis=-1, keepdims=True)
    xc = xf - mu
    y = xc * lax.rsqrt(jnp.mean(xc * xc, axis=-1, keepdims=True) + EPS)
    return (y * g.astype(jnp.float32) + b.astype(jnp.float32)).astype(x.dtype)


def project_in(a, w_in, b_f, q_gain, k_gain):
    z = a @ w_in
    c0 = 2 * CONV_WIDTH
    c_in, q, k, v, f = jnp.split(z, [c0, c0 + ATTN_WIDTH, c0 + 2 * ATTN_WIDTH, c0 + 3 * ATTN_WIDTH], axis=-1)
    u = c_in[..., :CONV_WIDTH] * jax.nn.sigmoid(c_in[..., CONV_WIDTH:])
    shp = q.shape[:-1] + (N_HEADS, HEAD_DIM)
    q = rmsnorm(q.reshape(shp), q_gain)
    k = rmsnorm(k.reshape(shp), k_gain)
    v = v.reshape(shp)
    logf = jax.nn.log_sigmoid(f.astype(jnp.float32) + b_f.astype(jnp.float32))
    return u, q, k, v, logf


def conv_tail(u_ext, w_dw, b_dw, ln_g, ln_b):
    y = lax.conv_general_dilated(u_ext, w_dw[:, None, :], window_strides=(1,), padding='VALID',
                                 dimension_numbers=('NWC', 'WIO', 'NWC'),
                                 feature_group_count=CONV_WIDTH) + b_dw
    return jax.nn.silu(layernorm(y, ln_g, ln_b))


def fox_attend(q, k, v, Fq, Fk, q_pos, k_pos):
    s = jnp.einsum('bqhd,bkhd->bhqk', q, k).astype(jnp.float32) * (HEAD_DIM ** -0.5)
    s = s + (jnp.transpose(Fq, (0, 2, 1))[..., :, None] - jnp.transpose(Fk, (0, 2, 1))[..., None, :])
    s = jnp.where(k_pos[None, :] <= q_pos[:, None], s, -jnp.inf)
    p = jax.nn.softmax(s, axis=-1)
    return jnp.einsum('bhqk,bkhd->bqhd', p.astype(v.dtype), v)


def fox_prompt(q, k, v, F):
    B, T = q.shape[:2]
    nb = T // Q_BLOCK
    qb = q.reshape(B, nb, Q_BLOCK, N_HEADS, HEAD_DIM).transpose(1, 0, 2, 3, 4)
    Fb = F.reshape(B, nb, Q_BLOCK, N_HEADS).transpose(1, 0, 2, 3)
    k_pos = jnp.arange(T)

    def one_block(args):
        qi, Fi, bi = args
        q_pos = bi * Q_BLOCK + jnp.arange(Q_BLOCK)
        return fox_attend(qi, k, v, Fi, F, q_pos, k_pos)

    out = lax.map(one_block, (qb, Fb, jnp.arange(nb)))
    return out.transpose(1, 0, 2, 3, 4).reshape(B, T, N_HEADS, HEAD_DIM)


def merge_out(yc, ya, gc, ga, w_out):
    ya = ya.reshape(ya.shape[:-2] + (ATTN_WIDTH,))
    y = jnp.concatenate([rmsnorm(yc, gc), rmsnorm(ya, ga)], axis=-1)
    return y @ w_out


def hier_moe(x, wg, bg, we, be, w1, w3, w2):
    B, T, D = x.shape
    n = B * T
    xt = x.reshape(n, D)
    pg = jax.nn.softmax((xt @ wg).astype(jnp.float32) + bg.astype(jnp.float32), axis=-1)
    g_star = jnp.argmax(pg, axis=-1)
    pg_star = jnp.take_along_axis(pg, g_star[:, None], axis=1)
    le = ((xt @ we).astype(jnp.float32) + be.astype(jnp.float32)).reshape(n, N_GROUPS, EXPERTS_PER_GROUP)
    le = jnp.take_along_axis(le, g_star[:, None, None], axis=1)[:, 0]
    top_p, top_i = lax.top_k(jax.nn.softmax(le, axis=-1), TOP_K)
    gates = pg_star * top_p / jnp.sum(top_p, axis=-1, keepdims=True)
    e_idx = g_star[:, None] * EXPERTS_PER_GROUP + top_i

    N = n * TOP_K
    flat_e = e_idx.reshape(N).astype(jnp.int32)
    flat_g = gates.reshape(N)
    flat_tok = jnp.repeat(jnp.arange(n, dtype=jnp.int32), TOP_K)
    order = jnp.argsort(flat_e)
    se, stok, sg = flat_e[order], flat_tok[order], flat_g[order]
    counts = jnp.bincount(flat_e, length=N_EXPERTS)
    starts = jnp.cumsum(counts) - counts
    pcounts = (counts + MOE_BLOCK - 1) // MOE_BLOCK * MOE_BLOCK
    pends = jnp.cumsum(pcounts)
    pstarts = pends - pcounts
    dest = pstarts[se] + (jnp.arange(N) - starts[se])
    nb = -(-(N + N_EXPERTS * (MOE_BLOCK - 1)) // MOE_BLOCK)
    P = nb * MOE_BLOCK
    row_tok = jnp.full((P,), n, jnp.int32).at[dest].set(stok)
    row_gate = jnp.zeros((P,), x.dtype).at[dest].set(sg.astype(x.dtype))
    block_e = jnp.clip(jnp.searchsorted(pends, jnp.arange(nb) * MOE_BLOCK, side='right'), 0, N_EXPERTS - 1)
    x_pad = jnp.concatenate([xt, jnp.zeros((1, D), xt.dtype)], axis=0)
    xb = x_pad[row_tok].reshape(nb, MOE_BLOCK, D)

    def expert_block(args):
        xblk, e = args
        h = jax.nn.silu(xblk @ w1[e]) * (xblk @ w3[e])
        return h @ w2[e]

    yb = lax.map(expert_block, (xb, block_e)).reshape(P, D) * row_gate[:, None]
    out = jnp.zeros((n + 1, D), yb.dtype).at[row_tok].add(yb)[:n]
    return out.reshape(B, T, D)


def ple_add(h, p, g_ple, w_pg, w_pp):
    return h + jax.nn.sigmoid(rmsnorm(h, g_ple) @ w_pg) * (p @ w_pp)


def setup_inputs(seed: int = 0) -> dict:
    key = jax.random.key(seed)
    ks = iter(jax.random.split(key, 40))
    nrm = lambda shape, s=1.0: jax.random.normal(next(ks), shape, jnp.float32) * s
    gain = lambda shape: 1.0 + nrm(shape, 0.05)
    L = DEPTH
    return {
        "x_prompt": nrm((BATCH, SEQ, D_MODEL)),
        "x_sample": nrm((DEC_BATCH, DEC_SEQ, D_MODEL)),
        "cache_k": nrm((L, DEC_BATCH, PAST_LEN, N_HEADS, HEAD_DIM)),
        "cache_v": nrm((L, DEC_BATCH, PAST_LEN, N_HEADS, HEAD_DIM)),
        "cache_logf": jax.nn.log_sigmoid(nrm((L, DEC_BATCH, PAST_LEN, N_HEADS)) + 2.0),
        "cache_conv": nrm((L, DEC_BATCH, CONV_STATE, CONV_WIDTH), 0.5),
        "p_prompt": nrm((L, BATCH, SEQ, PLE_DIM)),
        "p_sample": nrm((L, DEC_BATCH, DEC_SEQ, PLE_DIM)),
        "g_mix": gain((L, D_MODEL)),
        "w_in": nrm((L, D_MODEL, IN_COLS), D_MODEL ** -0.5),
        "b_f": 2.0 + nrm((L, N_HEADS), 0.1),
        "q_gain": gain((L, HEAD_DIM)),
        "k_gain": gain((L, HEAD_DIM)),
        "w_dw": nrm((L, CONV_KERNEL, CONV_WIDTH), CONV_KERNEL ** -0.5),
        "b_dw": nrm((L, CONV_WIDTH), 0.02),
        "ln_g": gain((L, CONV_WIDTH)),
        "ln_b": nrm((L, CONV_WIDTH), 0.02),
        "gc": gain((L, CONV_WIDTH)),
        "ga": gain((L, ATTN_WIDTH)),
        "w_out": nrm((L, MIX_WIDTH, D_MODEL), MIX_WIDTH ** -0.5),
        "g_ffn": gain((L, D_MODEL)),
        "w_router_g": nrm((L, D_MODEL, N_GROUPS), D_MODEL ** -0.5),
        "b_router_g": nrm((L, N_GROUPS), 0.01),
        "w_router_e": nrm((L, D_MODEL, N_EXPERTS), D_MODEL ** -0.5),
        "b_router_e": nrm((L, N_EXPERTS), 0.01),
        "w1": nrm((L, N_EXPERTS, D_MODEL, D_EXPERT), D_MODEL ** -0.5),
        "w3": nrm((L, N_EXPERTS, D_MODEL, D_EXPERT), D_MODEL ** -0.5),
        "w2": nrm((L, N_EXPERTS, D_EXPERT, D_MODEL), D_EXPERT ** -0.5),
        "g_ple": gain((L, D_MODEL)),
        "w_pg": nrm((L, D_MODEL, D_MODEL), D_MODEL ** -0.5),
        "w_pp": nrm((L, PLE_DIM, D_MODEL), PLE_DIM ** -0.5),
    }


def reference(x_prompt, x_sample, cache_k, cache_v, cache_logf, cache_conv, p_prompt, p_sample,
              g_mix, w_in, b_f, q_gain, k_gain, w_dw, b_dw, ln_g, ln_b, gc, ga, w_out,
              g_ffn, w_router_g, b_router_g, w_router_e, b_router_e, w1, w3, w2,
              g_ple, w_pg, w_pp):
    hp, hs = x_prompt, x_sample
    t_new = x_sample.shape[1]
    assert t_new <= CHUNK
    past = cache_k.shape[2]
    kp_l, vp_l, fp_l, cp_l, ks_l, vs_l, fs_l, cs_l = [], [], [], [], [], [], [], []
    for i in range(DEPTH):
        a = rmsnorm(hp, g_mix[i])
        u, q, k, v, logf = project_in(a, w_in[i], b_f[i], q_gain[i], k_gain[i])
        u_ext = jnp.pad(u, ((0, 0), (CONV_STATE, 0), (0, 0)))
        yc = conv_tail(u_ext, w_dw[i], b_dw[i], ln_g[i], ln_b[i])
        ya = fox_prompt(q, k, v, jnp.cumsum(logf, axis=1))
        hp = hp + merge_out(yc, ya, gc[i], ga[i], w_out[i])
        hp = hp + hier_moe(rmsnorm(hp, g_ffn[i]), w_router_g[i], b_router_g[i], w_router_e[i],
                           b_router_e[i], w1[i], w3[i], w2[i])
        hp = ple_add(hp, p_prompt[i], g_ple[i], w_pg[i], w_pp[i])
        kp_l.append(k); vp_l.append(v); fp_l.append(logf); cp_l.append(u_ext[:, -CONV_STATE:])

        a = rmsnorm(hs, g_mix[i])
        u, q, k, v, logf = project_in(a, w_in[i], b_f[i], q_gain[i], k_gain[i])
        u_ext = jnp.concatenate([cache_conv[i].astype(u.dtype), u], axis=1)
        yc = conv_tail(u_ext, w_dw[i], b_dw[i], ln_g[i], ln_b[i])
        k_all = jnp.concatenate([cache_k[i].astype(k.dtype), k], axis=1)
        v_all = jnp.concatenate([cache_v[i].astype(v.dtype), v], axis=1)
        F_all = jnp.cumsum(jnp.concatenate([cache_logf[i].astype(jnp.float32), logf], axis=1), axis=1)
        q_pos = past + jnp.arange(t_new)
        k_pos = jnp.arange(past + t_new)
        ya = fox_attend(q, k_all, v_all, F_all[:, past:], F_all, q_pos, k_pos)
        hs = hs + merge_out(yc, ya, gc[i], ga[i], w_out[i])
        hs = hs + hier_moe(rmsnorm(hs, g_ffn[i]), w_router_g[i], b_router_g[i], w_router_e[i],
                           b_router_e[i], w1[i], w3[i], w2[i])
        hs = ple_add(hs, p_sample[i], g_ple[i], w_pg[i], w_pp[i])
        ks_l.append(k); vs_l.append(v); fs_l.append(logf); cs_l.append(u_ext[:, -CONV_STATE:])

    new_k_prompt = jnp.stack(kp_l, 0)
    new_v_prompt = jnp.stack(vp_l, 0)
    new_logf_prompt = jnp.stack(fp_l, 0)
    new_conv_prompt = jnp.stack(cp_l, 0)
    new_k_sample = jnp.stack(ks_l, 0)
    new_v_sample = jnp.stack(vs_l, 0)
    new_logf_sample = jnp.stack(fs_l, 0)
    new_conv_sample = jnp.stack(cs_l, 0)
    return (hp, hs, new_k_prompt, new_v_prompt, new_logf_prompt, new_conv_prompt,
            new_k_sample, new_v_sample, new_logf_sample, new_conv_sample)
```

```python
import functools

import jax
import jax.numpy as jnp
from jax import lax
from jax.experimental import pallas as pl
from jax.experimental.pallas import tpu as pltpu

D_MODEL = 2048
CONV_WIDTH = 1024
ATTN_WIDTH = 1024
HEAD_DIM = 128
N_HEADS = 8
CONV_KERNEL = 31
CONV_STATE = CONV_KERNEL - 1
N_GROUPS = 4
EXPERTS_PER_GROUP = 8
N_EXPERTS = 32
D_EXPERT = 512
PLE_DIM = 256
MOE_BLOCK = 128
EPS = 1e-6
MAIN_COLS = 2 * CONV_WIDTH + 3 * ATTN_WIDTH
Q_OFF = 2 * CONV_WIDTH
K_OFF = Q_OFF + ATTN_WIDTH
V_OFF = K_OFF + ATTN_WIDTH

LANES = 128
CONV_HALO = 32
VMEM_LIMIT = 56 * 1024 * 1024

F32 = jnp.float32
BF16 = jnp.bfloat16
NEG_BIG = -1e30


def _dot(a, b):
    return jnp.dot(a, b, preferred_element_type=F32)


def _params(sem):
    return pltpu.CompilerParams(dimension_semantics=sem, vmem_limit_bytes=VMEM_LIMIT)


def _resident(shape):
    return pl.BlockSpec(shape, lambda *_: (0,) * len(shape), pipeline_mode=pl.Buffered(1))


def _inproj_kernel(x_ref, g_ref, w_ref, wf_ref, bf_ref, qg_ref, kg_ref,
                   u_ref, q_ref, k_ref, kb_ref, v_ref, vb_ref, lf_ref):
    x = x_ref[...]
    ms = jnp.mean(x * x, axis=-1, keepdims=True)
    a = (x * lax.rsqrt(ms + EPS) * g_ref[...]).astype(BF16)

    ch = 256
    for c in range(0, CONV_WIDTH, ch):
        val = _dot(a, w_ref[:, c:c + ch])
        gate = _dot(a, w_ref[:, CONV_WIDTH + c:CONV_WIDTH + c + ch])
        u_ref[:, c:c + ch] = val * jax.nn.sigmoid(gate)

    def head_norm(z, gain):
        return z * lax.rsqrt(jnp.mean(z * z, axis=-1, keepdims=True) + EPS) * gain

    scale = HEAD_DIM ** -0.5
    for c in range(0, ATTN_WIDTH, ch):
        zq = _dot(a, w_ref[:, Q_OFF + c:Q_OFF + c + ch])
        zk = _dot(a, w_ref[:, K_OFF + c:K_OFF + c + ch])
        zv = _dot(a, w_ref[:, V_OFF + c:V_OFF + c + ch])
        for s in range(0, ch, HEAD_DIM):
            qn = head_norm(zq[:, s:s + HEAD_DIM], qg_ref[...])
            kn = head_norm(zk[:, s:s + HEAD_DIM], kg_ref[...])
            q_ref[:, c + s:c + s + HEAD_DIM] = (qn * scale).astype(BF16)
            k_ref[:, c + s:c + s + HEAD_DIM] = kn
            kb_ref[:, c + s:c + s + HEAD_DIM] = kn.astype(BF16)
        v_ref[:, c:c + ch] = zv
        vb_ref[:, c:c + ch] = zv.astype(BF16)

    f = _dot(a, wf_ref[...]) + bf_ref[...]
    lf = jnp.minimum(f, 0.0) - jnp.log1p(jnp.exp(-jnp.abs(f)))
    lf_ref[...] = lf[:, :N_HEADS]


def _in_proj(x, g_mix, w_main, w_f, b_f, q_gain, k_gain, tm):
    n = x.shape[0]
    row = lambda w: pl.BlockSpec((tm, w), lambda i: (i, 0))
    out_shape = (
        jax.ShapeDtypeStruct((n, CONV_WIDTH), F32),
        jax.ShapeDtypeStruct((n, ATTN_WIDTH), BF16),
        jax.ShapeDtypeStruct((n, ATTN_WIDTH), F32),
        jax.ShapeDtypeStruct((n, ATTN_WIDTH), BF16),
        jax.ShapeDtypeStruct((n, ATTN_WIDTH), F32),
        jax.ShapeDtypeStruct((n, ATTN_WIDTH), BF16),
        jax.ShapeDtypeStruct((n, N_HEADS), F32),
    )
    return pl.pallas_call(
        _inproj_kernel,
        grid=(n // tm,),
        in_specs=[row(D_MODEL), _resident((1, D_MODEL)), _resident((D_MODEL, MAIN_COLS)),
                  _resident((D_MODEL, LANES)), _resident((1, LANES)),
                  _resident((1, HEAD_DIM)), _resident((1, HEAD_DIM))],
        out_specs=(row(CONV_WIDTH), row(ATTN_WIDTH), row(ATTN_WIDTH), row(ATTN_WIDTH),
                   row(ATTN_WIDTH), row(ATTN_WIDTH), row(N_HEADS)),
        out_shape=out_shape,
        compiler_params=_params(("parallel",)),
        name="in_proj",
    )(x, g_mix, w_main, w_f, b_f, q_gain, k_gain)


def _cumsum_kernel(x_ref, o_ref):
    x = x_ref[...]
    width = x.shape[1]
    lane = lax.broadcasted_iota(jnp.int32, x.shape, 1)
    s = 1
    while s < width:
        x = x + jnp.where(lane >= s, pltpu.roll(x, s, axis=1), 0.0)
        s *= 2
    o_ref[...] = x


def _cumsum_lanes(x):
    return pl.pallas_call(
        _cumsum_kernel,
        out_shape=jax.ShapeDtypeStruct(x.shape, F32),
        name="cumsum",
    )(x)


def _conv_kernel(u_ref, halo_ref, ctx_ref, w_ref, bdw_ref, lng_ref, lnb_ref, gc_ref,
                 o_ref, ext_ref, y_ref, *, tm):
    i = pl.program_id(1)

    @pl.when(i == 0)
    def _():
        ext_ref[0:CONV_HALO, :] = ctx_ref[0]

    @pl.when(i > 0)
    def _():
        ext_ref[0:CONV_HALO, :] = halo_ref[...]

    ext_ref[CONV_HALO:CONV_HALO + tm, :] = u_ref[...]

    rows = 32
    ch = 256
    first = CONV_HALO - CONV_STATE

    def conv_rows(r, carry):
        r0 = pl.multiple_of(r * rows, rows)
        for c in range(0, CONV_WIDTH, ch):
            acc = jnp.zeros((rows, ch), F32)
            win = ext_ref[pl.ds(r0, rows + CONV_HALO), c:c + ch]
            for j in range(CONV_KERNEL):
                acc = acc + win[first + j:first + j + rows] * w_ref[j:j + 1, c:c + ch]
            y_ref[pl.ds(r0, rows), c:c + ch] = acc
        return carry

    lax.fori_loop(0, tm // rows, conv_rows, 0)

    def norm_rows(r, carry):
        r0 = pl.multiple_of(r * rows, rows)
        y = y_ref[pl.ds(r0, rows), :] + bdw_ref[...]
        mu = jnp.mean(y, axis=-1, keepdims=True)
        yc = y - mu
        var = jnp.mean(yc * yc, axis=-1, keepdims=True)
        z = yc * lax.rsqrt(var + EPS) * lng_ref[...] + lnb_ref[...]
        s = z * jax.nn.sigmoid(z)
        ms = jnp.mean(s * s, axis=-1, keepdims=True)
        o_ref[pl.ds(r0, rows), :] = (s * lax.rsqrt(ms + EPS) * gc_ref[...]).astype(BF16)
        return carry

    lax.fori_loop(0, tm // rows, norm_rows, 0)


def _conv_module(u, ctx, w_dw, b_dw, ln_g, ln_b, gc, batch, seq, tm):
    nt = seq // tm
    hb = tm // CONV_HALO
    kernel = functools.partial(_conv_kernel, tm=tm)
    return pl.pallas_call(
        kernel,
        grid=(batch, nt),
        in_specs=[
            pl.BlockSpec((tm, CONV_WIDTH), lambda b, i: (b * nt + i, 0)),
            pl.BlockSpec((CONV_HALO, CONV_WIDTH),
                         lambda b, i: (jnp.maximum((b * nt + i) * hb - 1, 0), 0)),
            pl.BlockSpec((1, CONV_HALO, CONV_WIDTH), lambda b, i: (b, 0, 0)),
            _resident((CONV_HALO, CONV_WIDTH)),
            _resident((1, CONV_WIDTH)), _resident((1, CONV_WIDTH)),
            _resident((1, CONV_WIDTH)), _resident((1, CONV_WIDTH)),
        ],
        out_specs=pl.BlockSpec((tm, CONV_WIDTH), lambda b, i: (b * nt + i, 0)),
        out_shape=jax.ShapeDtypeStruct((batch * seq, CONV_WIDTH), BF16),
        scratch_shapes=[pltpu.VMEM((CONV_HALO + tm, CONV_WIDTH), F32),
                        pltpu.VMEM((tm, CONV_WIDTH), F32)],
        compiler_params=_params(("parallel", "arbitrary")),
        name="conv_module",
    )(u, u, ctx, w_dw, b_dw, ln_g, ln_b, gc)


def _qk(q, k):
    return lax.dot_general(q, k, (((1,), (1,)), ((), ())), preferred_element_type=F32)


def _attn_prompt_kernel(q_ref, k_ref, v_ref, fcol_ref, frow_ref, o_ref, *, tq):
    i = pl.program_id(1)
    tk = tq
    row = lax.broadcasted_iota(jnp.int32, (tq, tk), 0)
    col = lax.broadcasted_iota(jnp.int32, (tq, tk), 1)
    causal = col <= row

    for h in range(N_HEADS):
        hs = slice(h * HEAD_DIM, (h + 1) * HEAD_DIM)
        q = q_ref[:, hs]
        fq = fcol_ref[:, h:h + 1]

        def scores(ks):
            s = _qk(q, k_ref[pl.ds(ks, tk), hs])
            fk = frow_ref[0, h:h + 1, pl.ds(ks, tk)]
            return s + (fq - fk)

        def update(s, ks, carry):
            m, l, acc = carry
            m_new = jnp.maximum(m, jnp.max(s, axis=-1, keepdims=True))
            alpha = jnp.exp(m - m_new)
            p = jnp.exp(s - m_new)
            l = alpha * l + jnp.sum(p, axis=-1, keepdims=True)
            acc = alpha * acc + _dot(p.astype(BF16), v_ref[pl.ds(ks, tk), hs])
            return m_new, l, acc

        def body(j, carry):
            ks = pl.multiple_of(j * tk, tk)
            return update(scores(ks), ks, carry)

        init = (jnp.full((tq, 1), NEG_BIG, F32), jnp.zeros((tq, 1), F32),
                jnp.zeros((tq, HEAD_DIM), F32))
        carry = lax.fori_loop(0, i, body, init)
        kd = pl.multiple_of(i * tk, tk)
        s = jnp.where(causal, scores(kd), -jnp.inf)
        m, l, acc = update(s, kd, carry)
        o_ref[:, hs] = acc / l


def _attn_prompt(q, kb, vb, fcol, frow, batch, seq, tq):
    nq = seq // tq
    kernel = functools.partial(_attn_prompt_kernel, tq=tq)
    return pl.pallas_call(
        kernel,
        grid=(batch, nq),
        in_specs=[
            pl.BlockSpec((tq, ATTN_WIDTH), lambda b, i: (b * nq + i, 0)),
            pl.BlockSpec((seq, ATTN_WIDTH), lambda b, i: (b, 0)),
            pl.BlockSpec((seq, ATTN_WIDTH), lambda b, i: (b, 0)),
            pl.BlockSpec((tq, N_HEADS), lambda b, i: (b * nq + i, 0)),
            pl.BlockSpec((1, N_HEADS, seq), lambda b, i: (b, 0, 0)),
        ],
        out_specs=pl.BlockSpec((tq, ATTN_WIDTH), lambda b, i: (b * nq + i, 0)),
        out_shape=jax.ShapeDtypeStruct((batch * seq, ATTN_WIDTH), F32),
        compiler_params=_params(("parallel", "arbitrary")),
        name="attn_prompt",
    )(q, kb, vb, fcol, frow)


def _attn_sample_kernel(q_ref, kn_ref, vn_ref, ck_ref, cv_ref, fc_ref, fnrow_ref, fncol_ref,
                        o_ref, *, t_new, past):
    row = lax.broadcasted_iota(jnp.int32, (t_new, t_new), 0)
    col = lax.broadcasted_iota(jnp.int32, (t_new, t_new), 1)
    causal = col <= row
    for h in range(N_HEADS):
        hs = slice(h * HEAD_DIM, (h + 1) * HEAD_DIM)
        q = q_ref[:, hs]
        fn_q = fncol_ref[:, h:h + 1]
        fc = fc_ref[0, h:h + 1, :]
        fc_last = fc[:, past - 1:past]
        s_c = _qk(q, ck_ref[0, :, hs].astype(BF16)) + ((fc_last + fn_q) - fc)
        fn_k = fnrow_ref[0, h:h + 1, 0:t_new]
        s_n = _qk(q, kn_ref[:, hs]) + (fn_q - fn_k)
        s_n = jnp.where(causal, s_n, -jnp.inf)
        m = jnp.maximum(jnp.max(s_c, axis=-1, keepdims=True), jnp.max(s_n, axis=-1, keepdims=True))
        p_c = jnp.exp(s_c - m)
        p_n = jnp.exp(s_n - m)
        l = jnp.sum(p_c, axis=-1, keepdims=True) + jnp.sum(p_n, axis=-1, keepdims=True)
        acc = _dot(p_c.astype(BF16), cv_ref[0, :, hs].astype(BF16)) + _dot(p_n.astype(BF16), vn_ref[:, hs])
        o_ref[:, hs] = acc / l


def _attn_sample(q, kb, vb, cache_k, cache_v, fc_row, fn_row, fn_col, batch, t_new, past):
    kernel = functools.partial(_attn_sample_kernel, t_new=t_new, past=past)
    tok = lambda w: pl.BlockSpec((t_new, w), lambda b: (b, 0))
    return pl.pallas_call(
        kernel,
        grid=(batch,),
        in_specs=[
            tok(ATTN_WIDTH), tok(ATTN_WIDTH), tok(ATTN_WIDTH),
            pl.BlockSpec((1, past, ATTN_WIDTH), lambda b: (b, 0, 0)),
            pl.BlockSpec((1, past, ATTN_WIDTH), lambda b: (b, 0, 0)),
            pl.BlockSpec((1, N_HEADS, past), lambda b: (b, 0, 0)),
            pl.BlockSpec((1, N_HEADS, LANES), lambda b: (b, 0, 0)),
            tok(N_HEADS),
        ],
        out_specs=tok(ATTN_WIDTH),
        out_shape=jax.ShapeDtypeStruct((batch * t_new, ATTN_WIDTH), F32),
        compiler_params=_params(("parallel",)),
        name="attn_sample",
    )(q, kb, vb, cache_k, cache_v, fc_row, fn_row, fn_col)


def _pack_bf16_pair(lo, hi):
    lo_bits = lax.bitcast_convert_type(lo, jnp.uint32) >> 16
    hi_bits = lax.bitcast_convert_type(hi, jnp.uint32) & jnp.uint32(0xFFFF0000)
    return lo_bits | hi_bits


def _unpack_bf16_pair(w):
    lo = lax.bitcast_convert_type(w << 16, F32).astype(BF16)
    hi = lax.bitcast_convert_type(w & jnp.uint32(0xFFFF0000), F32).astype(BF16)
    return lo, hi


def _merge_kernel(yc_ref, ya_ref, x_ref, ga_ref, wo_ref, gf_ref, wrh_ref, wrl_ref, br_ref,
                  h_ref, xp_ref, lg_ref):
    ya = ya_ref[...]
    ya_n = (ya * lax.rsqrt(jnp.mean(ya * ya, axis=-1, keepdims=True) + EPS) * ga_ref[...]).astype(BF16)
    y = _dot(yc_ref[...], wo_ref[0:CONV_WIDTH, :]) + _dot(ya_n, wo_ref[CONV_WIDTH:, :])
    h = x_ref[...] + y
    h_ref[...] = h
    xn = h * lax.rsqrt(jnp.mean(h * h, axis=-1, keepdims=True) + EPS) * gf_ref[...]
    xn_hi = xn.astype(BF16)
    xn_hi32 = xn_hi.astype(F32)
    xn_lo = (xn - xn_hi32).astype(BF16)
    lg_ref[...] = (_dot(xn_hi, wrh_ref[...]) + _dot(xn_lo, wrh_ref[...]) + _dot(xn_hi, wrl_ref[...])
                   + br_ref[...])
    half = D_MODEL // 2
    xp_ref[...] = _pack_bf16_pair(xn_hi32[:, :half], xn_hi32[:, half:])


def _merge(yc_n, ya, x, ga, w_out, g_ffn, wr_hi, wr_lo, b_r, tm):
    n = x.shape[0]
    row = lambda w: pl.BlockSpec((tm, w), lambda i: (i, 0))
    return pl.pallas_call(
        _merge_kernel,
        grid=(n // tm,),
        in_specs=[row(CONV_WIDTH), row(ATTN_WIDTH), row(D_MODEL), _resident((1, ATTN_WIDTH)),
                  _resident((D_MODEL, D_MODEL)), _resident((1, D_MODEL)),
                  _resident((D_MODEL, LANES)), _resident((D_MODEL, LANES)), _resident((1, LANES))],
        out_specs=(row(D_MODEL), row(D_MODEL // 2), row(LANES)),
        out_shape=(jax.ShapeDtypeStruct((n, D_MODEL), F32),
                   jax.ShapeDtypeStruct((n, D_MODEL // 2), jnp.uint32),
                   jax.ShapeDtypeStruct((n, LANES), F32)),
        compiler_params=_params(("parallel",)),
        name="merge_out",
    )(yc_n, ya, x, ga, w_out, g_ffn, wr_hi, wr_lo, b_r)


def _route_kernel(lg_ref, info_ref, cnt_ref, carry_ref, *, tm):
    step = pl.program_id(0)

    @pl.when(step == 0)
    def _():
        carry_ref[...] = jnp.zeros_like(carry_ref)

    lg = lg_ref[...]
    lane = lax.broadcasted_iota(jnp.int32, lg.shape, 1)
    lanef = lane.astype(F32)
    big = jnp.float32(1e9)
    rmax = lambda v: jnp.max(v, axis=-1, keepdims=True)
    rmin = lambda v: jnp.min(v, axis=-1, keepdims=True)
    rsum = lambda v: jnp.sum(v, axis=-1, keepdims=True)

    is_g = (lane >= N_EXPERTS) & (lane < N_EXPERTS + N_GROUPS)
    gl = jnp.where(is_g, lg, NEG_BIG)
    gmax = rmax(gl)
    gsum = rsum(jnp.where(is_g, jnp.exp(gl - gmax), 0.0))
    pg_star = 1.0 / gsum
    g_idx = rmin(jnp.where(is_g & (gl == gmax), lanef - N_EXPERTS, big))

    e_lo = g_idx * EXPERTS_PER_GROUP
    is_e = (lanef >= e_lo) & (lanef < e_lo + EXPERTS_PER_GROUP)
    el = jnp.where(is_e, lg, NEG_BIG)
    m1 = rmax(el)
    i1 = rmin(jnp.where(is_e & (el == m1), lanef, big))
    sel1 = lanef == i1
    el2 = jnp.where(sel1, NEG_BIG, el)
    m2 = rmax(el2)
    i2 = rmin(jnp.where(is_e & (el2 == m2) & jnp.logical_not(sel1), lanef, big))
    sel2 = lanef == i2
    z = rsum(jnp.where(is_e, jnp.exp(el - m1), 0.0))
    p1 = 1.0 / z
    p2 = jnp.exp(m2 - m1) / z
    gate1 = pg_star * p1 / (p1 + p2)
    gate2 = pg_star * p2 / (p1 + p2)

    onehot = jnp.where(sel1 | sel2, 1.0, 0.0)
    r = lax.broadcasted_iota(jnp.int32, (tm, tm), 0)
    c = lax.broadcasted_iota(jnp.int32, (tm, tm), 1)
    tri = jnp.where(c < r, 1.0, 0.0).astype(BF16)
    before = _dot(tri, onehot.astype(BF16)) + carry_ref[...]
    rank1 = rsum(jnp.where(sel1, before, 0.0))
    rank2 = rsum(jnp.where(sel2, before, 0.0))
    carry_ref[...] = carry_ref[...] + jnp.sum(onehot, axis=0, keepdims=True)
    cnt_ref[...] = carry_ref[...]

    info = jnp.zeros_like(lg)
    for k, val in enumerate((i1, i2, rank1, rank2, gate1, gate2)):
        info = jnp.where(lane == k, val, info)
    info_ref[...] = info


def _route(logits, tm):
    n = logits.shape[0]
    kernel = functools.partial(_route_kernel, tm=tm)
    return pl.pallas_call(
        kernel,
        grid=(n // tm,),
        in_specs=[pl.BlockSpec((tm, LANES), lambda i: (i, 0))],
        out_specs=(pl.BlockSpec((tm, LANES), lambda i: (i, 0)),
                   pl.BlockSpec((1, LANES), lambda i: (0, 0))),
        out_shape=(jax.ShapeDtypeStruct((n, LANES), F32), jax.ShapeDtypeStruct((1, LANES), F32)),
        scratch_shapes=[pltpu.VMEM((1, LANES), F32)],
        compiler_params=_params(("arbitrary",)),
        name="route",
    )(logits)


def _dispatch_kernel(dest_ref, x_hbm, xs_in_hbm, xs_hbm, sem, *, n):
    del xs_in_hbm

    def row_copy(t, k):
        return pltpu.make_async_copy(x_hbm.at[pl.ds(t, 1)], xs_hbm.at[pl.ds(dest_ref[2 * t + k], 1)], sem)

    def issue(t, carry):
        row_copy(t, 0).start()
        row_copy(t, 1).start()
        return carry

    def drain(t, carry):
        row_copy(t, 0).wait()
        row_copy(t, 1).wait()
        return carry

    lax.fori_loop(0, n, issue, 0)
    lax.fori_loop(0, n, drain, 0)


def _dispatch(dest_flat, x_packed, n_slots):
    n, w = x_packed.shape
    xs0 = jnp.zeros((n_slots, w), x_packed.dtype)
    kernel = functools.partial(_dispatch_kernel, n=n)
    grid_spec = pltpu.PrefetchScalarGridSpec(
        num_scalar_prefetch=1,
        grid=(1,),
        in_specs=[pl.BlockSpec(memory_space=pl.ANY), pl.BlockSpec(memory_space=pl.ANY)],
        out_specs=pl.BlockSpec(memory_space=pl.ANY),
        scratch_shapes=[pltpu.SemaphoreType.DMA(())],
    )
    return pl.pallas_call(
        kernel,
        grid_spec=grid_spec,
        out_shape=jax.ShapeDtypeStruct((n_slots, w), x_packed.dtype),
        input_output_aliases={2: 0},
        compiler_params=pltpu.CompilerParams(dimension_semantics=("arbitrary",),
                                             has_side_effects=True),
        name="dispatch",
    )(dest_flat, x_packed, xs0)


def _expert_kernel(be_ref, nu_ref, xs_ref, w1_ref, w3_ref, w2_ref, y_ref):
    b = pl.program_id(0)

    @pl.when(b < nu_ref[0])
    def _():
        lo, hi = _unpack_bf16_pair(xs_ref[...])
        half = D_MODEL // 2
        h1 = _dot(lo, w1_ref[0, :half, :]) + _dot(hi, w1_ref[0, half:, :])
        h3 = _dot(lo, w3_ref[0, :half, :]) + _dot(hi, w3_ref[0, half:, :])
        h = (h1 * jax.nn.sigmoid(h1) * h3).astype(BF16)
        y_ref[...] = _dot(h, w2_ref[0])


def _experts(block_e, n_used, xs, w1, w3, w2):
    n_slots = xs.shape[0]
    nb = n_slots // MOE_BLOCK
    blk = lambda b, be, nu: (jnp.minimum(b, nu[0] - 1), 0)
    wsel = lambda b, be, nu: (be[jnp.minimum(b, nu[0] - 1)], 0, 0)
    grid_spec = pltpu.PrefetchScalarGridSpec(
        num_scalar_prefetch=2,
        grid=(nb,),
        in_specs=[pl.BlockSpec((MOE_BLOCK, D_MODEL // 2), blk),
                  pl.BlockSpec((1, D_MODEL, D_EXPERT), wsel),
                  pl.BlockSpec((1, D_MODEL, D_EXPERT), wsel),
                  pl.BlockSpec((1, D_EXPERT, D_MODEL), wsel)],
        out_specs=pl.BlockSpec((MOE_BLOCK, D_MODEL), blk),
    )
    return pl.pallas_call(
        _expert_kernel,
        grid_spec=grid_spec,
        out_shape=jax.ShapeDtypeStruct((n_slots, D_MODEL), F32),
        compiler_params=_params(("arbitrary",)),
        name="experts",
    )(block_e, n_used, xs, w1, w3, w2)


def _combine_kernel(dest_ref, h_ref, info_ref, yb_hbm, o_ref, buf_ref, sem, *, tm, t_off):
    i = pl.program_id(0)
    nsteps = pl.num_programs(0)

    def row_copy(step, slot, r, k):
        t = t_off + step * tm + r
        return pltpu.make_async_copy(yb_hbm.at[pl.ds(dest_ref[2 * t + k], 1)],
                                     buf_ref.at[slot, k, pl.ds(r, 1)], sem.at[slot])

    def issue(step, slot):
        def body(r, carry):
            row_copy(step, slot, r, 0).start()
            row_copy(step, slot, r, 1).start()
            return carry
        lax.fori_loop(0, tm, body, 0)

    def drain(step, slot):
        def body(r, carry):
            row_copy(step, slot, r, 0).wait()
            row_copy(step, slot, r, 1).wait()
            return carry
        lax.fori_loop(0, tm, body, 0)

    slot = i % 2

    @pl.when(i == 0)
    def _():
        issue(0, 0)

    @pl.when(i + 1 < nsteps)
    def _():
        issue(i + 1, 1 - slot)

    drain(i, slot)
    g0 = info_ref[:, 4:5]
    g1 = info_ref[:, 5:6]
    o_ref[...] = h_ref[...] + (g0 * buf_ref[slot, 0] + g1 * buf_ref[slot, 1])


def _combine(dest_flat, h, info, yb, t_off, tm):
    n = h.shape[0]
    kernel = functools.partial(_combine_kernel, tm=tm, t_off=t_off)
    ob = t_off // tm
    grid_spec = pltpu.PrefetchScalarGridSpec(
        num_scalar_prefetch=1,
        grid=(n // tm,),
        in_specs=[pl.BlockSpec((tm, D_MODEL), lambda i, d: (i, 0)),
                  pl.BlockSpec((tm, LANES), lambda i, d: (i + ob, 0)),
                  pl.BlockSpec(memory_space=pl.ANY)],
        out_specs=pl.BlockSpec((tm, D_MODEL), lambda i, d: (i, 0)),
        scratch_shapes=[pltpu.VMEM((2, 2, tm, D_MODEL), F32), pltpu.SemaphoreType.DMA((2,))],
    )
    return pl.pallas_call(
        kernel,
        grid_spec=grid_spec,
        out_shape=jax.ShapeDtypeStruct((n, D_MODEL), F32),
        compiler_params=_params(("arbitrary",)),
        name="combine",
    )(dest_flat, h, info, yb)


def _ple_kernel(h_ref, p_ref, g_ref, wpg_ref, wpp_ref, o_ref):
    h = h_ref[...]
    hn = (h * lax.rsqrt(jnp.mean(h * h, axis=-1, keepdims=True) + EPS) * g_ref[...]).astype(BF16)
    pb = p_ref[...].astype(BF16)
    ch = 512
    for c in range(0, D_MODEL, ch):
        gate = jax.nn.sigmoid(_dot(hn, wpg_ref[:, c:c + ch]))
        o_ref[:, c:c + ch] = h_ref[:, c:c + ch] + gate * _dot(pb, wpp_ref[:, c:c + ch])


def _ple(h, p, g_ple, w_pg, w_pp, tm):
    n = h.shape[0]
    row = lambda w: pl.BlockSpec((tm, w), lambda i: (i, 0))
    return pl.pallas_call(
        _ple_kernel,
        grid=(n // tm,),
        in_specs=[row(D_MODEL), row(PLE_DIM), _resident((1, D_MODEL)),
                  _resident((D_MODEL, D_MODEL)), _resident((PLE_DIM, D_MODEL))],
        out_specs=row(D_MODEL),
        out_shape=jax.ShapeDtypeStruct((n, D_MODEL), F32),
        compiler_params=_params(("parallel",)),
        name="ple",
    )(h, p, g_ple, w_pg, w_pp)


def _mixer_tokens(x2d, wts, tm):
    return _in_proj(x2d, wts["g_mix"], wts["w_main"], wts["w_f"], wts["b_f"],
                    wts["q_gain"], wts["k_gain"], tm)


def kernel(x_prompt, x_sample, cache_k, cache_v, cache_logf, cache_conv, p_prompt, p_sample,
           g_mix, w_in, b_f, q_gain, k_gain, w_dw, b_dw, ln_g, ln_b, gc, ga, w_out,
           g_ffn, w_router_g, b_router_g, w_router_e, b_router_e, w1, w3, w2,
           g_ple, w_pg, w_pp):
    batch, seq, _ = x_prompt.shape
    dec_batch, t_new, _ = x_sample.shape
    past = cache_k.shape[2]
    n_p = batch * seq
    n_s = dec_batch * t_new
    n_all = n_p + n_s
    tm = 256
    li = 0

    w_in_l = w_in[li]
    pad_lanes = lambda a: jnp.pad(a, ((0, 0), (0, LANES - a.shape[1])))
    row2d = lambda a: a.reshape(1, -1)
    wts = {
        "g_mix": row2d(g_mix[li]),
        "w_main": w_in_l[:, :MAIN_COLS].astype(BF16),
        "w_f": pad_lanes(w_in_l[:, MAIN_COLS:]).astype(BF16),
        "b_f": pad_lanes(row2d(b_f[li])),
        "q_gain": row2d(q_gain[li]),
        "k_gain": row2d(k_gain[li]),
    }
    w_dw_p = jnp.pad(w_dw[li], ((0, CONV_HALO - CONV_KERNEL), (0, 0)))
    w_out_b = w_out[li].astype(BF16)
    w_r = pad_lanes(jnp.concatenate([w_router_e[li], w_router_g[li]], axis=1))
    wr_hi = w_r.astype(BF16)
    wr_lo = (w_r - wr_hi.astype(F32)).astype(BF16)
    b_r = pad_lanes(row2d(jnp.concatenate([b_router_e[li], b_router_g[li]])))
    w1_b, w3_b, w2_b = w1[li].astype(BF16), w3[li].astype(BF16), w2[li].astype(BF16)
    w_pg_b, w_pp_b = w_pg[li].astype(BF16), w_pp[li].astype(BF16)

    xp = x_prompt.reshape(n_p, D_MODEL)
    u_p, q_p, k_p, kb_p, v_p, vb_p, lf_p = _mixer_tokens(xp, wts, tm)
    lf_p_row = lf_p.reshape(batch, seq, N_HEADS).transpose(0, 2, 1)
    f_p_row = _cumsum_lanes(lf_p_row.reshape(batch * N_HEADS, seq)).reshape(batch, N_HEADS, seq)
    f_p_col = f_p_row.transpose(0, 2, 1).reshape(n_p, N_HEADS)
    ctx_p = jnp.zeros((batch, CONV_HALO, CONV_WIDTH), F32)
    conv_args = (w_dw_p, row2d(b_dw[li]), row2d(ln_g[li]), row2d(ln_b[li]), row2d(gc[li]))
    yc_p = _conv_module(u_p, ctx_p, *conv_args, batch, seq, tm)
    ya_p = _attn_prompt(q_p, kb_p, vb_p, f_p_col, f_p_row, batch, seq, tm)
    merge_args = (row2d(ga[li]), w_out_b, row2d(g_ffn[li]), wr_hi, wr_lo, b_r)
    h_p, xpk_p, lg_p = _merge(yc_p, ya_p, xp, *merge_args, tm)

    xs_ = x_sample.reshape(n_s, D_MODEL)
    u_s, q_s, k_s, kb_s, v_s, vb_s, lf_s = _mixer_tokens(xs_, wts, tm)
    clf_row = cache_logf[li].transpose(0, 2, 1).reshape(dec_batch * N_HEADS, past)
    fc_row = _cumsum_lanes(clf_row).reshape(dec_batch, N_HEADS, past)
    lf_s_row = lf_s.reshape(dec_batch, t_new, N_HEADS).transpose(0, 2, 1).reshape(dec_batch * N_HEADS, t_new)
    fn_row = _cumsum_lanes(jnp.pad(lf_s_row, ((0, 0), (0, LANES - t_new)))).reshape(dec_batch, N_HEADS, LANES)
    fn_col = fn_row[:, :, :t_new].transpose(0, 2, 1).reshape(n_s, N_HEADS)
    ctx_s = jnp.pad(cache_conv[li], ((0, 0), (CONV_HALO - CONV_STATE, 0), (0, 0)))
    yc_s = _conv_module(u_s, ctx_s, *conv_args, dec_batch, t_new, t_new)
    ya_s = _attn_sample(q_s, kb_s, vb_s,
                        cache_k[li].reshape(dec_batch, past, ATTN_WIDTH),
                        cache_v[li].reshape(dec_batch, past, ATTN_WIDTH),
                        fc_row, fn_row, fn_col, dec_batch, t_new, past)
    h_s, xpk_s, lg_s = _merge(yc_s, ya_s, xs_, *merge_args, tm)

    info, counts = _route(jnp.concatenate([lg_p, lg_s], axis=0), tm)
    counts = counts[0, :N_EXPERTS].astype(jnp.int32)
    bcounts = (counts + MOE_BLOCK - 1) // MOE_BLOCK
    bends = jnp.cumsum(bcounts)
    pstarts = (bends - bcounts) * MOE_BLOCK
    n_rows = n_all * 2
    nb = -(-(n_rows + N_EXPERTS * (MOE_BLOCK - 1)) // MOE_BLOCK)
    e_idx = info[:, 0:2].astype(jnp.int32)
    dest = (pstarts[e_idx] + info[:, 2:4].astype(jnp.int32)).reshape(n_rows)
    block_e = jnp.minimum(jnp.searchsorted(bends, jnp.arange(nb, dtype=jnp.int32), side="right"),
                          N_EXPERTS - 1).astype(jnp.int32)
    n_used = bends[N_EXPERTS - 1:].astype(jnp.int32)
    xs_sorted = _dispatch(dest, jnp.concatenate([xpk_p, xpk_s], axis=0), nb * MOE_BLOCK)
    yb = _experts(block_e, n_used, xs_sorted, w1_b, w3_b, w2_b)
    h2_p = _combine(dest, h_p, info, yb, 0, tm)
    h2_s = _combine(dest, h_s, info, yb, n_p, tm)

    y_p = _ple(h2_p, p_prompt[li].reshape(n_p, PLE_DIM), row2d(g_ple[li]), w_pg_b, w_pp_b, tm)
    y_s = _ple(h2_s, p_sample[li].reshape(n_s, PLE_DIM), row2d(g_ple[li]), w_pg_b, w_pp_b, tm)

    heads = lambda a, b, t: a.reshape(1, b, t, N_HEADS, HEAD_DIM)
    return (
        y_p.reshape(batch, seq, D_MODEL),
        y_s.reshape(dec_batch, t_new, D_MODEL),
        heads(k_p, batch, seq), heads(v_p, batch, seq),
        lf_p.reshape(1, batch, seq, N_HEADS),
        u_p.reshape(batch, seq, CONV_WIDTH)[None, :, seq - CONV_STATE:, :],
        heads(k_s, dec_batch, t_new), heads(v_s, dec_batch, t_new),
        lf_s.reshape(1, dec_batch, t_new, N_HEADS),
        u_s.reshape(dec_batch, t_new, CONV_WIDTH)[None, :, t_new - CONV_STATE:, :],
    )
```

```python
import functools

import jax
import jax.numpy as jnp
from jax import lax
from jax.experimental import pallas as pl
from jax.experimental.pallas import tpu as pltpu

D_MODEL = 2048
CONV_WIDTH = 1024
ATTN_WIDTH = 1024
HEAD_DIM = 128
N_HEADS = 8
CONV_KERNEL = 31
CONV_STATE = CONV_KERNEL - 1
N_GROUPS = 4
EXPERTS_PER_GROUP = 8
N_EXPERTS = 32
D_EXPERT = 512
PLE_DIM = 256
MOE_BLOCK = 128
EPS = 1e-6
MAIN_COLS = 2 * CONV_WIDTH + 3 * ATTN_WIDTH
Q_OFF = 2 * CONV_WIDTH
K_OFF = Q_OFF + ATTN_WIDTH
V_OFF = K_OFF + ATTN_WIDTH

LANES = 128
CONV_HALO = 32
VMEM_LIMIT = 56 * 1024 * 1024

F32 = jnp.float32
BF16 = jnp.bfloat16
NEG_BIG = -1e30
LOG2E = 1.4426950408889634


def _dot(a, b):
    return jnp.dot(a, b, preferred_element_type=F32)


def _params(sem):
    return pltpu.CompilerParams(dimension_semantics=sem, vmem_limit_bytes=VMEM_LIMIT)


def _resident(shape):
    return pl.BlockSpec(shape, lambda *_: (0,) * len(shape), pipeline_mode=pl.Buffered(1))


def _inproj_kernel(x_ref, g_ref, w_ref, wf_ref, bf_ref, qg_ref, kg_ref,
                   u_ref, q_ref, k_ref, kb_ref, v_ref, vb_ref, lf_ref):
    x = x_ref[...]
    ms = jnp.mean(x * x, axis=-1, keepdims=True)
    a = (x * lax.rsqrt(ms + EPS) * g_ref[...]).astype(BF16)

    ch = 256
    for c in range(0, CONV_WIDTH, ch):
        val = _dot(a, w_ref[:, c:c + ch])
        gate = _dot(a, w_ref[:, CONV_WIDTH + c:CONV_WIDTH + c + ch])
        u_ref[:, c:c + ch] = val * jax.nn.sigmoid(gate)

    def head_norm(z, gain):
        return z * lax.rsqrt(jnp.mean(z * z, axis=-1, keepdims=True) + EPS) * gain

    scale = LOG2E * HEAD_DIM ** -0.5
    for c in range(0, ATTN_WIDTH, ch):
        zq = _dot(a, w_ref[:, Q_OFF + c:Q_OFF + c + ch])
        zk = _dot(a, w_ref[:, K_OFF + c:K_OFF + c + ch])
        zv = _dot(a, w_ref[:, V_OFF + c:V_OFF + c + ch])
        for s in range(0, ch, HEAD_DIM):
            qn = head_norm(zq[:, s:s + HEAD_DIM], qg_ref[...])
            kn = head_norm(zk[:, s:s + HEAD_DIM], kg_ref[...])
            q_ref[:, c + s:c + s + HEAD_DIM] = (qn * scale).astype(BF16)
            k_ref[:, c + s:c + s + HEAD_DIM] = kn
            kb_ref[:, c + s:c + s + HEAD_DIM] = kn.astype(BF16)
        v_ref[:, c:c + ch] = zv
        vb_ref[:, c:c + ch] = zv.astype(BF16)

    f = _dot(a, wf_ref[...]) + bf_ref[...]
    lf = jnp.minimum(f, 0.0) - jnp.log1p(jnp.exp(-jnp.abs(f)))
    lf_ref[...] = lf[:, :N_HEADS]


def _in_proj(x, g_mix, w_main, w_f, b_f, q_gain, k_gain, tm):
    n = x.shape[0]
    row = lambda w: pl.BlockSpec((tm, w), lambda i: (i, 0))
    out_shape = (
        jax.ShapeDtypeStruct((n, CONV_WIDTH), F32),
        jax.ShapeDtypeStruct((n, ATTN_WIDTH), BF16),
        jax.ShapeDtypeStruct((n, ATTN_WIDTH), F32),
        jax.ShapeDtypeStruct((n, ATTN_WIDTH), BF16),
        jax.ShapeDtypeStruct((n, ATTN_WIDTH), F32),
        jax.ShapeDtypeStruct((n, ATTN_WIDTH), BF16),
        jax.ShapeDtypeStruct((n, N_HEADS), F32),
    )
    return pl.pallas_call(
        _inproj_kernel,
        grid=(n // tm,),
        in_specs=[row(D_MODEL), _resident((1, D_MODEL)), _resident((D_MODEL, MAIN_COLS)),
                  _resident((D_MODEL, LANES)), _resident((1, LANES)),
                  _resident((1, HEAD_DIM)), _resident((1, HEAD_DIM))],
        out_specs=(row(CONV_WIDTH), row(ATTN_WIDTH), row(ATTN_WIDTH), row(ATTN_WIDTH),
                   row(ATTN_WIDTH), row(ATTN_WIDTH), row(N_HEADS)),
        out_shape=out_shape,
        compiler_params=_params(("parallel",)),
        name="in_proj",
    )(x, g_mix, w_main, w_f, b_f, q_gain, k_gain)


def _cumsum_kernel(x_ref, o_ref):
    x = x_ref[...]
    width = x.shape[1]
    lane = lax.broadcasted_iota(jnp.int32, x.shape, 1)
    s = 1
    while s < width:
        x = x + jnp.where(lane >= s, pltpu.roll(x, s, axis=1), 0.0)
        s *= 2
    o_ref[...] = x


def _cumsum_lanes(x):
    return pl.pallas_call(
        _cumsum_kernel,
        out_shape=jax.ShapeDtypeStruct(x.shape, F32),
        name="cumsum",
    )(x)


def _conv_kernel(u_ref, halo_ref, ctx_ref, w_ref, bdw_ref, lng_ref, lnb_ref, gc_ref,
                 o_ref, ext_ref, y_ref, *, tm):
    i = pl.program_id(1)

    @pl.when(i == 0)
    def _():
        ext_ref[0:CONV_HALO, :] = ctx_ref[0]

    @pl.when(i > 0)
    def _():
        ext_ref[0:CONV_HALO, :] = halo_ref[...]

    ext_ref[CONV_HALO:CONV_HALO + tm, :] = u_ref[...]

    rows = 32
    ch = 256
    first = CONV_HALO - CONV_STATE

    def conv_rows(r, carry):
        r0 = pl.multiple_of(r * rows, rows)
        for c in range(0, CONV_WIDTH, ch):
            acc = jnp.zeros((rows, ch), F32)
            win = ext_ref[pl.ds(r0, rows + CONV_HALO), c:c + ch]
            for j in range(CONV_KERNEL):
                acc = acc + win[first + j:first + j + rows] * w_ref[j:j + 1, c:c + ch]
            y_ref[pl.ds(r0, rows), c:c + ch] = acc
        return carry

    lax.fori_loop(0, tm // rows, conv_rows, 0)

    def norm_rows(r, carry):
        r0 = pl.multiple_of(r * rows, rows)
        y = y_ref[pl.ds(r0, rows), :] + bdw_ref[...]
        mu = jnp.mean(y, axis=-1, keepdims=True)
        yc = y - mu
        var = jnp.mean(yc * yc, axis=-1, keepdims=True)
        z = yc * lax.rsqrt(var + EPS) * lng_ref[...] + lnb_ref[...]
        s = z * jax.nn.sigmoid(z)
        ms = jnp.mean(s * s, axis=-1, keepdims=True)
        o_ref[pl.ds(r0, rows), :] = (s * lax.rsqrt(ms + EPS) * gc_ref[...]).astype(BF16)
        return carry

    lax.fori_loop(0, tm // rows, norm_rows, 0)


def _conv_module(u, ctx, w_dw, b_dw, ln_g, ln_b, gc, batch, seq, tm):
    nt = seq // tm
    hb = tm // CONV_HALO
    kernel = functools.partial(_conv_kernel, tm=tm)
    return pl.pallas_call(
        kernel,
        grid=(batch, nt),
        in_specs=[
            pl.BlockSpec((tm, CONV_WIDTH), lambda b, i: (b * nt + i, 0)),
            pl.BlockSpec((CONV_HALO, CONV_WIDTH),
                         lambda b, i: (jnp.maximum((b * nt + i) * hb - 1, 0), 0)),
            pl.BlockSpec((1, CONV_HALO, CONV_WIDTH), lambda b, i: (b, 0, 0)),
            _resident((CONV_HALO, CONV_WIDTH)),
            _resident((1, CONV_WIDTH)), _resident((1, CONV_WIDTH)),
            _resident((1, CONV_WIDTH)), _resident((1, CONV_WIDTH)),
        ],
        out_specs=pl.BlockSpec((tm, CONV_WIDTH), lambda b, i: (b * nt + i, 0)),
        out_shape=jax.ShapeDtypeStruct((batch * seq, CONV_WIDTH), BF16),
        scratch_shapes=[pltpu.VMEM((CONV_HALO + tm, CONV_WIDTH), F32),
                        pltpu.VMEM((tm, CONV_WIDTH), F32)],
        compiler_params=_params(("parallel", "arbitrary")),
        name="conv_module",
    )(u, u, ctx, w_dw, b_dw, ln_g, ln_b, gc)


def _qk(q, k):
    return lax.dot_general(q, k, (((1,), (1,)), ((), ())), preferred_element_type=F32)


AUG_TERMS = 3


def _aug_kernel(f_ref, qa_ref, ka_ref):
    f = f_ref[...] * LOG2E
    tm = f.shape[0]
    lane = lax.broadcasted_iota(jnp.int32, (tm, HEAD_DIM), 1)
    for h in range(N_HEADS):
        hs = slice(h * HEAD_DIM, (h + 1) * HEAD_DIM)
        rest = f[:, h:h + 1]
        qa = jnp.where((lane >= AUG_TERMS) & (lane < 2 * AUG_TERMS), 1.0, 0.0)
        ka = jnp.where(lane < AUG_TERMS, 1.0, 0.0)
        for t in range(AUG_TERMS):
            piece = rest.astype(BF16).astype(F32)
            rest = rest - piece
            qa = jnp.where(lane == t, piece, qa)
            ka = jnp.where(lane == AUG_TERMS + t, -piece, ka)
        qa_ref[:, hs] = qa.astype(BF16)
        ka_ref[:, hs] = ka.astype(BF16)


def _attn_aug(f_col, tm):
    n = f_col.shape[0]
    return pl.pallas_call(
        _aug_kernel,
        grid=(n // tm,),
        in_specs=[pl.BlockSpec((tm, N_HEADS), lambda i: (i, 0))],
        out_specs=(pl.BlockSpec((tm, ATTN_WIDTH), lambda i: (i, 0)),
                   pl.BlockSpec((tm, ATTN_WIDTH), lambda i: (i, 0))),
        out_shape=(jax.ShapeDtypeStruct((n, ATTN_WIDTH), BF16),
                   jax.ShapeDtypeStruct((n, ATTN_WIDTH), BF16)),
        compiler_params=_params(("parallel",)),
        name="attn_aug",
    )(f_col)


def _attn_prompt_kernel(q_ref, qa_ref, k_ref, ka_ref, v_ref, o_ref, m_ref, l_ref, acc_ref, *, tq, tk):
    i = pl.program_id(1)
    key = lax.broadcasted_iota(jnp.int32, (tk, tq), 0)
    qry = lax.broadcasted_iota(jnp.int32, (tk, tq), 1)

    m_ref[...] = jnp.full(m_ref.shape, NEG_BIG, F32)
    l_ref[...] = jnp.zeros(l_ref.shape, F32)
    acc_ref[...] = jnp.zeros(acc_ref.shape, F32)

    def tile_step(ks, diag_offset):
        def scores(h):
            hs = slice(h * HEAD_DIM, (h + 1) * HEAD_DIM)
            qf = jnp.concatenate([q_ref[:, hs], qa_ref[:, hs]], axis=1)
            kf = jnp.concatenate([k_ref[pl.ds(ks, tk), hs], ka_ref[pl.ds(ks, tk), hs]], axis=1)
            return _qk(kf, qf)

        s_next = scores(0)
        for h in range(N_HEADS):
            hs = slice(h * HEAD_DIM, (h + 1) * HEAD_DIM)
            s = s_next
            if h + 1 < N_HEADS:
                s_next = scores(h + 1)
            if diag_offset is not None:
                s = jnp.where(key + diag_offset <= qry, s, -jnp.inf)
            m = m_ref[h]
            m_new = jnp.maximum(m, jnp.max(s, axis=0, keepdims=True))
            alpha = jnp.exp2(m - m_new)
            p = jnp.exp2(s - m_new)
            m_ref[h] = m_new
            l_ref[h] = alpha * l_ref[h] + jnp.sum(p, axis=0, keepdims=True)
            pv = lax.dot_general(v_ref[pl.ds(ks, tk), hs], p.astype(BF16), (((0,), (0,)), ((), ())),
                                 preferred_element_type=F32)
            acc_ref[h] = alpha * acc_ref[h] + pv

    def body(j, carry):
        tile_step(pl.multiple_of(j * tk, tk), None)
        return carry

    lax.fori_loop(0, i * (tq // tk), body, 0)
    for d in range(tq // tk):
        tile_step(pl.multiple_of(i * tq + d * tk, tk), d * tk)
    for h in range(N_HEADS):
        o_ref[:, h * HEAD_DIM:(h + 1) * HEAD_DIM] = (acc_ref[h] / l_ref[h]).T


def _attn_prompt(q, qa, kb, ka, vb, batch, seq, tq, tk):
    nq = seq // tq
    kernel = functools.partial(_attn_prompt_kernel, tq=tq, tk=tk)
    qblk = pl.BlockSpec((tq, ATTN_WIDTH), lambda b, i: (b * nq + i, 0))
    kblk = pl.BlockSpec((seq, ATTN_WIDTH), lambda b, i: (b, 0), pipeline_mode=pl.Buffered(1))
    return pl.pallas_call(
        kernel,
        grid=(batch, nq),
        in_specs=[qblk, qblk, kblk, kblk, kblk],
        out_specs=pl.BlockSpec((tq, ATTN_WIDTH), lambda b, i: (b * nq + i, 0)),
        out_shape=jax.ShapeDtypeStruct((batch * seq, ATTN_WIDTH), F32),
        scratch_shapes=[pltpu.VMEM((N_HEADS, 1, tq), F32), pltpu.VMEM((N_HEADS, 1, tq), F32),
                        pltpu.VMEM((N_HEADS, HEAD_DIM, tq), F32)],
        compiler_params=_params(("parallel", "arbitrary")),
        name="attn_prompt",
    )(q, qa, kb, ka, vb)


def _attn_sample_kernel(q_ref, kn_ref, vn_ref, ck_ref, cv_ref, fc_ref, fnrow_ref, fncol_ref,
                        o_ref, *, t_new, past):
    row = lax.broadcasted_iota(jnp.int32, (t_new, t_new), 0)
    col = lax.broadcasted_iota(jnp.int32, (t_new, t_new), 1)
    causal = col <= row
    for h in range(N_HEADS):
        hs = slice(h * HEAD_DIM, (h + 1) * HEAD_DIM)
        q = q_ref[:, hs]
        fn_q = fncol_ref[:, h:h + 1]
        fc = fc_ref[0, h:h + 1, :]
        fc_last = fc[:, past - 1:past]
        s_c = _qk(q, ck_ref[0, :, h, :].astype(BF16)) + ((fc_last + fn_q) - fc) * LOG2E
        fn_k = fnrow_ref[0, h:h + 1, 0:t_new]
        s_n = _qk(q, kn_ref[:, hs]) + (fn_q - fn_k) * LOG2E
        s_n = jnp.where(causal, s_n, -jnp.inf)
        m = jnp.maximum(jnp.max(s_c, axis=-1, keepdims=True), jnp.max(s_n, axis=-1, keepdims=True))
        p_c = jnp.exp2(s_c - m)
        p_n = jnp.exp2(s_n - m)
        l = jnp.sum(p_c, axis=-1, keepdims=True) + jnp.sum(p_n, axis=-1, keepdims=True)
        acc = _dot(p_c.astype(BF16), cv_ref[0, :, h, :].astype(BF16)) + _dot(p_n.astype(BF16), vn_ref[:, hs])
        o_ref[:, hs] = acc / l


def _attn_sample(q, kb, vb, cache_k, cache_v, fc_row, fn_row, fn_col, batch, t_new, past):
    kernel = functools.partial(_attn_sample_kernel, t_new=t_new, past=past)
    tok = lambda w: pl.BlockSpec((t_new, w), lambda b: (b, 0))
    return pl.pallas_call(
        kernel,
        grid=(batch,),
        in_specs=[
            tok(ATTN_WIDTH), tok(ATTN_WIDTH), tok(ATTN_WIDTH),
            pl.BlockSpec((1, past, N_HEADS, HEAD_DIM), lambda b: (b, 0, 0, 0)),
            pl.BlockSpec((1, past, N_HEADS, HEAD_DIM), lambda b: (b, 0, 0, 0)),
            pl.BlockSpec((1, N_HEADS, past), lambda b: (b, 0, 0)),
            pl.BlockSpec((1, N_HEADS, LANES), lambda b: (b, 0, 0)),
            tok(N_HEADS),
        ],
        out_specs=tok(ATTN_WIDTH),
        out_shape=jax.ShapeDtypeStruct((batch * t_new, ATTN_WIDTH), F32),
        compiler_params=_params(("parallel",)),
        name="attn_sample",
    )(q, kb, vb, cache_k, cache_v, fc_row, fn_row, fn_col)


def _pack_bf16_pair(lo, hi):
    lo_bits = lax.bitcast_convert_type(lo, jnp.uint32) >> 16
    hi_bits = lax.bitcast_convert_type(hi, jnp.uint32) & jnp.uint32(0xFFFF0000)
    return lo_bits | hi_bits


def _unpack_bf16_pair(w):
    lo = lax.bitcast_convert_type(w << 16, F32).astype(BF16)
    hi = lax.bitcast_convert_type(w & jnp.uint32(0xFFFF0000), F32).astype(BF16)
    return lo, hi


def _merge_kernel(yc_ref, ya_ref, x_ref, ga_ref, wo_ref, gf_ref, wrh_ref, wrl_ref, br_ref,
                  h_ref, xp_ref, lg_ref):
    ya = ya_ref[...]
    ya_n = (ya * lax.rsqrt(jnp.mean(ya * ya, axis=-1, keepdims=True) + EPS) * ga_ref[...]).astype(BF16)
    y = _dot(yc_ref[...], wo_ref[0:CONV_WIDTH, :]) + _dot(ya_n, wo_ref[CONV_WIDTH:, :])
    h = x_ref[...] + y
    h_ref[...] = h
    xn = h * lax.rsqrt(jnp.mean(h * h, axis=-1, keepdims=True) + EPS) * gf_ref[...]
    xn_hi = xn.astype(BF16)
    xn_hi32 = xn_hi.astype(F32)
    xn_lo = (xn - xn_hi32).astype(BF16)
    lg_ref[...] = (_dot(xn_hi, wrh_ref[...]) + _dot(xn_lo, wrh_ref[...]) + _dot(xn_hi, wrl_ref[...])
                   + br_ref[...])
    half = D_MODEL // 2
    xp_ref[...] = _pack_bf16_pair(xn_hi32[:, :half], xn_hi32[:, half:])


def _merge(yc_n, ya, x, ga, w_out, g_ffn, wr_hi, wr_lo, b_r, tm):
    n = x.shape[0]
    row = lambda w: pl.BlockSpec((tm, w), lambda i: (i, 0))
    return pl.pallas_call(
        _merge_kernel,
        grid=(n // tm,),
        in_specs=[row(CONV_WIDTH), row(ATTN_WIDTH), row(D_MODEL), _resident((1, ATTN_WIDTH)),
                  _resident((D_MODEL, D_MODEL)), _resident((1, D_MODEL)),
                  _resident((D_MODEL, LANES)), _resident((D_MODEL, LANES)), _resident((1, LANES))],
        out_specs=(row(D_MODEL), row(D_MODEL // 2), row(LANES)),
        out_shape=(jax.ShapeDtypeStruct((n, D_MODEL), F32),
                   jax.ShapeDtypeStruct((n, D_MODEL // 2), jnp.uint32),
                   jax.ShapeDtypeStruct((n, LANES), F32)),
        compiler_params=_params(("parallel",)),
        name="merge_out",
    )(yc_n, ya, x, ga, w_out, g_ffn, wr_hi, wr_lo, b_r)


def _route_kernel(lg_ref, info_ref, cnt_ref, carry_ref, *, tm):
    step = pl.program_id(0)

    @pl.when(step == 0)
    def _():
        carry_ref[...] = jnp.zeros_like(carry_ref)

    lg = lg_ref[...]
    lane = lax.broadcasted_iota(jnp.int32, lg.shape, 1)
    lanef = lane.astype(F32)
    big = jnp.float32(1e9)
    rmax = lambda v: jnp.max(v, axis=-1, keepdims=True)
    rmin = lambda v: jnp.min(v, axis=-1, keepdims=True)
    rsum = lambda v: jnp.sum(v, axis=-1, keepdims=True)

    is_g = (lane >= N_EXPERTS) & (lane < N_EXPERTS + N_GROUPS)
    gl = jnp.where(is_g, lg, NEG_BIG)
    gmax = rmax(gl)
    gsum = rsum(jnp.where(is_g, jnp.exp(gl - gmax), 0.0))
    pg_star = 1.0 / gsum
    g_idx = rmin(jnp.where(is_g & (gl == gmax), lanef - N_EXPERTS, big))

    e_lo = g_idx * EXPERTS_PER_GROUP
    is_e = (lanef >= e_lo) & (lanef < e_lo + EXPERTS_PER_GROUP)
    el = jnp.where(is_e, lg, NEG_BIG)
    m1 = rmax(el)
    i1 = rmin(jnp.where(is_e & (el == m1), lanef, big))
    sel1 = lanef == i1
    el2 = jnp.where(sel1, NEG_BIG, el)
    m2 = rmax(el2)
    i2 = rmin(jnp.where(is_e & (el2 == m2) & jnp.logical_not(sel1), lanef, big))
    sel2 = lanef == i2
    z = rsum(jnp.where(is_e, jnp.exp(el - m1), 0.0))
    p1 = 1.0 / z
    p2 = jnp.exp(m2 - m1) / z
    gate1 = pg_star * p1 / (p1 + p2)
    gate2 = pg_star * p2 / (p1 + p2)

    onehot = jnp.where(sel1 | sel2, 1.0, 0.0)
    r = lax.broadcasted_iota(jnp.int32, (tm, tm), 0)
    c = lax.broadcasted_iota(jnp.int32, (tm, tm), 1)
    tri = jnp.where(c < r, 1.0, 0.0).astype(BF16)
    before = _dot(tri, onehot.astype(BF16)) + carry_ref[...]
    rank1 = rsum(jnp.where(sel1, before, 0.0))
    rank2 = rsum(jnp.where(sel2, before, 0.0))
    carry_ref[...] = carry_ref[...] + jnp.sum(onehot, axis=0, keepdims=True)
    cnt_ref[...] = carry_ref[...]

    info = jnp.zeros_like(lg)
    for k, val in enumerate((i1, i2, rank1, rank2, gate1, gate2)):
        info = jnp.where(lane == k, val, info)
    info_ref[...] = info


def _route(logits, tm):
    n = logits.shape[0]
    kernel = functools.partial(_route_kernel, tm=tm)
    return pl.pallas_call(
        kernel,
        grid=(n // tm,),
        in_specs=[pl.BlockSpec((tm, LANES), lambda i: (i, 0))],
        out_specs=(pl.BlockSpec((tm, LANES), lambda i: (i, 0)),
                   pl.BlockSpec((1, LANES), lambda i: (0, 0))),
        out_shape=(jax.ShapeDtypeStruct((n, LANES), F32), jax.ShapeDtypeStruct((1, LANES), F32)),
        scratch_shapes=[pltpu.VMEM((1, LANES), F32)],
        compiler_params=_params(("arbitrary",)),
        name="route",
    )(logits)


DISPATCH_CHUNK = 128


def _dispatch_kernel(dest_ref, xa_ref, xb_ref, xs_in_hbm, xs_hbm, sem):
    del xs_in_hbm

    def scatter(src_ref, t_off):
        def row_copy(t, k):
            return pltpu.make_async_copy(src_ref.at[pl.ds(t, 1)],
                                         xs_hbm.at[pl.ds(dest_ref[2 * (t_off + t) + k], 1)], sem)

        def issue(c):
            def body(r, carry):
                row_copy(c * DISPATCH_CHUNK + r, 0).start()
                row_copy(c * DISPATCH_CHUNK + r, 1).start()
                return carry
            lax.fori_loop(0, DISPATCH_CHUNK, body, 0, unroll=8)

        def drain(c):
            def body(r, carry):
                row_copy(c * DISPATCH_CHUNK + r, 0).wait()
                row_copy(c * DISPATCH_CHUNK + r, 1).wait()
                return carry
            lax.fori_loop(0, DISPATCH_CHUNK, body, 0, unroll=8)

        n_chunks = src_ref.shape[0] // DISPATCH_CHUNK

        def chunk(c, carry):
            issue(c)

            @pl.when(c > 0)
            def _():
                drain(c - 1)
            return carry

        lax.fori_loop(0, n_chunks, chunk, 0)
        drain(n_chunks - 1)

    scatter(xa_ref, 0)
    scatter(xb_ref, xa_ref.shape[0])


def _dispatch(dest_flat, xa, xb, n_slots):
    w = xa.shape[1]
    xs0 = jnp.zeros((n_slots, w), xa.dtype)
    vmem = pl.BlockSpec(memory_space=pltpu.VMEM)
    grid_spec = pltpu.PrefetchScalarGridSpec(
        num_scalar_prefetch=1,
        grid=(1,),
        in_specs=[vmem, vmem, pl.BlockSpec(memory_space=pl.ANY)],
        out_specs=pl.BlockSpec(memory_space=pl.ANY),
        scratch_shapes=[pltpu.SemaphoreType.DMA(())],
    )
    return pl.pallas_call(
        _dispatch_kernel,
        grid_spec=grid_spec,
        out_shape=jax.ShapeDtypeStruct((n_slots, w), xa.dtype),
        input_output_aliases={3: 0},
        compiler_params=pltpu.CompilerParams(dimension_semantics=("arbitrary",),
                                             vmem_limit_bytes=VMEM_LIMIT, has_side_effects=True),
        name="dispatch",
    )(dest_flat, xa, xb, xs0)


def _expert_kernel(be_ref, nu_ref, xs_ref, w1_ref, w3_ref, w2_ref, y_ref, w1b_ref, w3b_ref, w2b_ref):
    b = pl.program_id(0)
    active = b < nu_ref[0]
    new_expert = jnp.logical_or(b == 0, be_ref[b] != be_ref[jnp.maximum(b - 1, 0)])

    @pl.when(jnp.logical_and(active, new_expert))
    def _():
        w1b_ref[...] = w1_ref[0].astype(BF16)
        w3b_ref[...] = w3_ref[0].astype(BF16)
        w2b_ref[...] = w2_ref[0].astype(BF16)

    @pl.when(active)
    def _():
        lo, hi = _unpack_bf16_pair(xs_ref[...])
        half = D_MODEL // 2
        h1 = _dot(lo, w1b_ref[:half, :]) + _dot(hi, w1b_ref[half:, :])
        h3 = _dot(lo, w3b_ref[:half, :]) + _dot(hi, w3b_ref[half:, :])
        h = (h1 * jax.nn.sigmoid(h1) * h3).astype(BF16)
        y_ref[...] = _dot(h, w2b_ref[...])


def _experts(block_e, n_used, xs, w1, w3, w2):
    n_slots = xs.shape[0]
    nb = n_slots // MOE_BLOCK
    blk = lambda b, be, nu: (jnp.minimum(b, nu[0] - 1), 0)
    wsel = lambda b, be, nu: (be[jnp.minimum(b, nu[0] - 1)], 0, 0)
    grid_spec = pltpu.PrefetchScalarGridSpec(
        num_scalar_prefetch=2,
        grid=(nb,),
        in_specs=[pl.BlockSpec((MOE_BLOCK, D_MODEL // 2), blk),
                  pl.BlockSpec((1, D_MODEL, D_EXPERT), wsel),
                  pl.BlockSpec((1, D_MODEL, D_EXPERT), wsel),
                  pl.BlockSpec((1, D_EXPERT, D_MODEL), wsel)],
        out_specs=pl.BlockSpec((MOE_BLOCK, D_MODEL), blk),
        scratch_shapes=[pltpu.VMEM((D_MODEL, D_EXPERT), BF16), pltpu.VMEM((D_MODEL, D_EXPERT), BF16),
                        pltpu.VMEM((D_EXPERT, D_MODEL), BF16)],
    )
    return pl.pallas_call(
        _expert_kernel,
        grid_spec=grid_spec,
        out_shape=jax.ShapeDtypeStruct((n_slots, D_MODEL), F32),
        compiler_params=_params(("arbitrary",)),
        name="experts",
    )(block_e, n_used, xs, w1, w3, w2)


def _combine_kernel(dest_ref, h_ref, info_ref, yb_hbm, o_ref, buf_ref, sem, *, tm, t_off):
    i = pl.program_id(0)
    nsteps = pl.num_programs(0)

    def row_copy(step, slot, r, k):
        t = t_off + step * tm + r
        return pltpu.make_async_copy(yb_hbm.at[pl.ds(dest_ref[2 * t + k], 1)],
                                     buf_ref.at[slot, k, pl.ds(r, 1)], sem.at[slot])

    def issue(step, slot):
        def body(r, carry):
            row_copy(step, slot, r, 0).start()
            row_copy(step, slot, r, 1).start()
            return carry
        lax.fori_loop(0, tm, body, 0)

    def drain(step, slot):
        def body(r, carry):
            row_copy(step, slot, r, 0).wait()
            row_copy(step, slot, r, 1).wait()
            return carry
        lax.fori_loop(0, tm, body, 0)

    slot = i % 2

    @pl.when(i == 0)
    def _():
        issue(0, 0)

    @pl.when(i + 1 < nsteps)
    def _():
        issue(i + 1, 1 - slot)

    drain(i, slot)
    g0 = info_ref[:, 4:5]
    g1 = info_ref[:, 5:6]
    o_ref[...] = h_ref[...] + (g0 * buf_ref[slot, 0] + g1 * buf_ref[slot, 1])


def _combine(dest_flat, h, info, yb, t_off, tm):
    n = h.shape[0]
    kernel = functools.partial(_combine_kernel, tm=tm, t_off=t_off)
    ob = t_off // tm
    grid_spec = pltpu.PrefetchScalarGridSpec(
        num_scalar_prefetch=1,
        grid=(n // tm,),
        in_specs=[pl.BlockSpec((tm, D_MODEL), lambda i, d: (i, 0)),
                  pl.BlockSpec((tm, LANES), lambda i, d: (i + ob, 0)),
                  pl.BlockSpec(memory_space=pl.ANY)],
        out_specs=pl.BlockSpec((tm, D_MODEL), lambda i, d: (i, 0)),
        scratch_shapes=[pltpu.VMEM((2, 2, tm, D_MODEL), F32), pltpu.SemaphoreType.DMA((2,))],
    )
    return pl.pallas_call(
        kernel,
        grid_spec=grid_spec,
        out_shape=jax.ShapeDtypeStruct((n, D_MODEL), F32),
        compiler_params=_params(("arbitrary",)),
        name="combine",
    )(dest_flat, h, info, yb)


def _ple_kernel(h_ref, p_ref, g_ref, wpg_ref, wpp_ref, o_ref):
    h = h_ref[...]
    hn = (h * lax.rsqrt(jnp.mean(h * h, axis=-1, keepdims=True) + EPS) * g_ref[...]).astype(BF16)
    pb = p_ref[...].astype(BF16)
    ch = 512
    for c in range(0, D_MODEL, ch):
        gate = jax.nn.sigmoid(_dot(hn, wpg_ref[:, c:c + ch]))
        o_ref[:, c:c + ch] = h_ref[:, c:c + ch] + gate * _dot(pb, wpp_ref[:, c:c + ch])


def _ple(h, p, g_ple, w_pg, w_pp, tm):
    n = h.shape[0]
    row = lambda w: pl.BlockSpec((tm, w), lambda i: (i, 0))
    return pl.pallas_call(
        _ple_kernel,
        grid=(n // tm,),
        in_specs=[row(D_MODEL), row(PLE_DIM), _resident((1, D_MODEL)),
                  _resident((D_MODEL, D_MODEL)), _resident((PLE_DIM, D_MODEL))],
        out_specs=row(D_MODEL),
        out_shape=jax.ShapeDtypeStruct((n, D_MODEL), F32),
        compiler_params=_params(("parallel",)),
        name="ple",
    )(h, p, g_ple, w_pg, w_pp)


def _mixer_tokens(x2d, wts, tm):
    return _in_proj(x2d, wts["g_mix"], wts["w_main"], wts["w_f"], wts["b_f"],
                    wts["q_gain"], wts["k_gain"], tm)


def kernel(x_prompt, x_sample, cache_k, cache_v, cache_logf, cache_conv, p_prompt, p_sample,
           g_mix, w_in, b_f, q_gain, k_gain, w_dw, b_dw, ln_g, ln_b, gc, ga, w_out,
           g_ffn, w_router_g, b_router_g, w_router_e, b_router_e, w1, w3, w2,
           g_ple, w_pg, w_pp):
    batch, seq, _ = x_prompt.shape
    dec_batch, t_new, _ = x_sample.shape
    past = cache_k.shape[2]
    n_p = batch * seq
    n_s = dec_batch * t_new
    n_all = n_p + n_s
    tm = 256
    li = 0

    w_in_l = w_in[li]
    pad_lanes = lambda a: jnp.pad(a, ((0, 0), (0, LANES - a.shape[1])))
    row2d = lambda a: a.reshape(1, -1)
    wts = {
        "g_mix": row2d(g_mix[li]),
        "w_main": w_in_l[:, :MAIN_COLS].astype(BF16),
        "w_f": pad_lanes(w_in_l[:, MAIN_COLS:]).astype(BF16),
        "b_f": pad_lanes(row2d(b_f[li])),
        "q_gain": row2d(q_gain[li]),
        "k_gain": row2d(k_gain[li]),
    }
    w_dw_p = jnp.pad(w_dw[li], ((0, CONV_HALO - CONV_KERNEL), (0, 0)))
    w_out_b = w_out[li].astype(BF16)
    w_r = pad_lanes(jnp.concatenate([w_router_e[li], w_router_g[li]], axis=1))
    wr_hi = w_r.astype(BF16)
    wr_lo = (w_r - wr_hi.astype(F32)).astype(BF16)
    b_r = pad_lanes(row2d(jnp.concatenate([b_router_e[li], b_router_g[li]])))
    w_pg_b, w_pp_b = w_pg[li].astype(BF16), w_pp[li].astype(BF16)

    xp = x_prompt.reshape(n_p, D_MODEL)
    u_p, q_p, k_p, kb_p, v_p, vb_p, lf_p = _mixer_tokens(xp, wts, tm)
    lf_p_row = lf_p.reshape(batch, seq, N_HEADS).transpose(0, 2, 1)
    f_p_row = _cumsum_lanes(lf_p_row.reshape(batch * N_HEADS, seq)).reshape(batch, N_HEADS, seq)
    f_p_col = f_p_row.transpose(0, 2, 1).reshape(n_p, N_HEADS)
    ctx_p = jnp.zeros((batch, CONV_HALO, CONV_WIDTH), F32)
    conv_args = (w_dw_p, row2d(b_dw[li]), row2d(ln_g[li]), row2d(ln_b[li]), row2d(gc[li]))
    yc_p = _conv_module(u_p, ctx_p, *conv_args, batch, seq, tm)
    qa_p, ka_p = _attn_aug(f_p_col, tm)
    ya_p = _attn_prompt(q_p, qa_p, kb_p, ka_p, vb_p, batch, seq, 256, 256)
    merge_args = (row2d(ga[li]), w_out_b, row2d(g_ffn[li]), wr_hi, wr_lo, b_r)
    h_p, xpk_p, lg_p = _merge(yc_p, ya_p, xp, *merge_args, tm)

    xs_ = x_sample.reshape(n_s, D_MODEL)
    u_s, q_s, k_s, kb_s, v_s, vb_s, lf_s = _mixer_tokens(xs_, wts, tm)
    clf_row = cache_logf[li].transpose(0, 2, 1).reshape(dec_batch * N_HEADS, past)
    fc_row = _cumsum_lanes(clf_row).reshape(dec_batch, N_HEADS, past)
    lf_s_row = lf_s.reshape(dec_batch, t_new, N_HEADS).transpose(0, 2, 1).reshape(dec_batch * N_HEADS, t_new)
    fn_row = _cumsum_lanes(jnp.pad(lf_s_row, ((0, 0), (0, LANES - t_new)))).reshape(dec_batch, N_HEADS, LANES)
    fn_col = fn_row[:, :, :t_new].transpose(0, 2, 1).reshape(n_s, N_HEADS)
    ctx_s = jnp.pad(cache_conv[li], ((0, 0), (CONV_HALO - CONV_STATE, 0), (0, 0)))
    yc_s = _conv_module(u_s, ctx_s, *conv_args, dec_batch, t_new, t_new)
    ya_s = _attn_sample(q_s, kb_s, vb_s,
                        cache_k[li], cache_v[li],
                        fc_row, fn_row, fn_col, dec_batch, t_new, past)
    h_s, xpk_s, lg_s = _merge(yc_s, ya_s, xs_, *merge_args, tm)

    info, counts = _route(jnp.concatenate([lg_p, lg_s], axis=0), tm)
    counts = counts[0, :N_EXPERTS].astype(jnp.int32)
    bcounts = (counts + MOE_BLOCK - 1) // MOE_BLOCK
    bends = jnp.cumsum(bcounts)
    pstarts = (bends - bcounts) * MOE_BLOCK
    n_rows = n_all * 2
    nb = -(-(n_rows + N_EXPERTS * (MOE_BLOCK - 1)) // MOE_BLOCK)
    e_idx = info[:, 0:2].astype(jnp.int32)
    dest = (pstarts[e_idx] + info[:, 2:4].astype(jnp.int32)).reshape(n_rows)
    block_e = jnp.minimum(jnp.sum(bends[None, :] <= jnp.arange(nb, dtype=jnp.int32)[:, None], axis=1),
                          N_EXPERTS - 1).astype(jnp.int32)
    n_used = bends[N_EXPERTS - 1:].astype(jnp.int32)
    xs_sorted = _dispatch(dest, xpk_p, xpk_s, nb * MOE_BLOCK)
    yb = _experts(block_e, n_used, xs_sorted, w1[li], w3[li], w2[li])
    h2_p = _combine(dest, h_p, info, yb, 0, tm)
    h2_s = _combine(dest, h_s, info, yb, n_p, tm)

    y_p = _ple(h2_p, p_prompt[li].reshape(n_p, PLE_DIM), row2d(g_ple[li]), w_pg_b, w_pp_b, tm)
    y_s = _ple(h2_s, p_sample[li].reshape(n_s, PLE_DIM), row2d(g_ple[li]), w_pg_b, w_pp_b, tm)

    heads = lambda a, b, t: a.reshape(1, b, t, N_HEADS, HEAD_DIM)
    return (
        y_p.reshape(batch, seq, D_MODEL),
        y_s.reshape(dec_batch, t_new, D_MODEL),
        heads(k_p, batch, seq), heads(v_p, batch, seq),
        lf_p.reshape(1, batch, seq, N_HEADS),
        u_p.reshape(batch, seq, CONV_WIDTH)[None, :, seq - CONV_STATE:, :],
        heads(k_s, dec_batch, t_new), heads(v_s, dec_batch, t_new),
        lf_s.reshape(1, dec_batch, t_new, N_HEADS),
        u_s.reshape(dec_batch, t_new, CONV_WIDTH)[None, :, t_new - CONV_STATE:, :],
    )
```

```python
import functools

import jax
import jax.numpy as jnp
from jax import lax
from jax.experimental import pallas as pl
from jax.experimental.pallas import tpu as pltpu

D_MODEL = 2048
CONV_WIDTH = 1024
ATTN_WIDTH = 1024
HEAD_DIM = 128
N_HEADS = 8
CONV_KERNEL = 31
CONV_STATE = CONV_KERNEL - 1
N_GROUPS = 4
EXPERTS_PER_GROUP = 8
N_EXPERTS = 32
D_EXPERT = 512
PLE_DIM = 256
MOE_BLOCK = 128
EPS = 1e-6
MAIN_COLS = 2 * CONV_WIDTH + 3 * ATTN_WIDTH
Q_OFF = 2 * CONV_WIDTH
K_OFF = Q_OFF + ATTN_WIDTH
V_OFF = K_OFF + ATTN_WIDTH

LANES = 128
CONV_HALO = 32
VMEM_LIMIT = 56 * 1024 * 1024

F32 = jnp.float32
BF16 = jnp.bfloat16
NEG_BIG = -1e30
LOG2E = 1.4426950408889634


def _dot(a, b):
    return jnp.dot(a, b, preferred_element_type=F32)


def _params(sem):
    return pltpu.CompilerParams(dimension_semantics=sem, vmem_limit_bytes=VMEM_LIMIT)


def _resident(shape):
    return pl.BlockSpec(shape, lambda *_: (0,) * len(shape), pipeline_mode=pl.Buffered(1))


def _inproj_kernel(x_ref, g_ref, w_ref, wf_ref, bf_ref, qg_ref, kg_ref,
                   u_ref, q_ref, k_ref, kb_ref, v_ref, vb_ref, vbt_ref, lf_ref):
    x = x_ref[...]
    ms = jnp.mean(x * x, axis=-1, keepdims=True)
    a = (x * lax.rsqrt(ms + EPS) * g_ref[...]).astype(BF16)

    ch = 256
    for c in range(0, CONV_WIDTH, ch):
        val = _dot(a, w_ref[:, c:c + ch])
        gate = _dot(a, w_ref[:, CONV_WIDTH + c:CONV_WIDTH + c + ch])
        u_ref[:, c:c + ch] = val * jax.nn.sigmoid(gate)

    def head_norm(z, gain):
        return z * lax.rsqrt(jnp.mean(z * z, axis=-1, keepdims=True) + EPS) * gain

    scale = LOG2E * HEAD_DIM ** -0.5
    for c in range(0, ATTN_WIDTH, ch):
        zq = _dot(a, w_ref[:, Q_OFF + c:Q_OFF + c + ch])
        zk = _dot(a, w_ref[:, K_OFF + c:K_OFF + c + ch])
        zv = _dot(a, w_ref[:, V_OFF + c:V_OFF + c + ch])
        for s in range(0, ch, HEAD_DIM):
            qn = head_norm(zq[:, s:s + HEAD_DIM], qg_ref[...])
            kn = head_norm(zk[:, s:s + HEAD_DIM], kg_ref[...])
            q_ref[:, c + s:c + s + HEAD_DIM] = (qn * scale).astype(BF16)
            k_ref[:, c + s:c + s + HEAD_DIM] = kn
            kb_ref[:, c + s:c + s + HEAD_DIM] = kn.astype(BF16)
        v_ref[:, c:c + ch] = zv
        vb_ref[:, c:c + ch] = zv.astype(BF16)
        vbt_ref[c:c + ch, :] = zv.T.astype(BF16)

    f = _dot(a, wf_ref[...]) + bf_ref[...]
    lf = jnp.minimum(f, 0.0) - jnp.log1p(jnp.exp(-jnp.abs(f)))
    lf_ref[...] = lf[:, :N_HEADS]


def _in_proj(x, g_mix, w_main, w_f, b_f, q_gain, k_gain, tm):
    n = x.shape[0]
    row = lambda w: pl.BlockSpec((tm, w), lambda i: (i, 0))
    out_shape = (
        jax.ShapeDtypeStruct((n, CONV_WIDTH), F32),
        jax.ShapeDtypeStruct((n, ATTN_WIDTH), BF16),
        jax.ShapeDtypeStruct((n, ATTN_WIDTH), F32),
        jax.ShapeDtypeStruct((n, ATTN_WIDTH), BF16),
        jax.ShapeDtypeStruct((n, ATTN_WIDTH), F32),
        jax.ShapeDtypeStruct((n, ATTN_WIDTH), BF16),
        jax.ShapeDtypeStruct((ATTN_WIDTH, n), BF16),
        jax.ShapeDtypeStruct((n, N_HEADS), F32),
    )
    return pl.pallas_call(
        _inproj_kernel,
        grid=(n // tm,),
        in_specs=[row(D_MODEL), _resident((1, D_MODEL)), _resident((D_MODEL, MAIN_COLS)),
                  _resident((D_MODEL, LANES)), _resident((1, LANES)),
                  _resident((1, HEAD_DIM)), _resident((1, HEAD_DIM))],
        out_specs=(row(CONV_WIDTH), row(ATTN_WIDTH), row(ATTN_WIDTH), row(ATTN_WIDTH),
                   row(ATTN_WIDTH), row(ATTN_WIDTH),
                   pl.BlockSpec((ATTN_WIDTH, tm), lambda i: (0, i)), row(N_HEADS)),
        out_shape=out_shape,
        compiler_params=_params(("parallel",)),
        name="in_proj",
    )(x, g_mix, w_main, w_f, b_f, q_gain, k_gain)


def _cumsum_kernel(x_ref, o_ref):
    x = x_ref[...]
    width = x.shape[1]
    lane = lax.broadcasted_iota(jnp.int32, x.shape, 1)
    s = 1
    while s < width:
        x = x + jnp.where(lane >= s, pltpu.roll(x, s, axis=1), 0.0)
        s *= 2
    o_ref[...] = x


def _cumsum_lanes(x):
    return pl.pallas_call(
        _cumsum_kernel,
        out_shape=jax.ShapeDtypeStruct(x.shape, F32),
        name="cumsum",
    )(x)


def _conv_kernel(u_ref, halo_ref, ctx_ref, w_ref, bdw_ref, lng_ref, lnb_ref, gc_ref,
                 o_ref, ext_ref, y_ref, *, tm):
    i = pl.program_id(1)

    @pl.when(i == 0)
    def _():
        ext_ref[0:CONV_HALO, :] = ctx_ref[0]

    @pl.when(i > 0)
    def _():
        ext_ref[0:CONV_HALO, :] = halo_ref[...]

    ext_ref[CONV_HALO:CONV_HALO + tm, :] = u_ref[...]

    rows = min(64, tm)
    ch = LANES
    sub = 8
    wlen = rows + CONV_HALO
    first = CONV_HALO - CONV_STATE

    def conv_rows(r, carry):
        r0 = pl.multiple_of(r * rows, rows)
        for c in range(0, CONV_WIDTH, ch):
            acc = jnp.zeros((rows, ch), F32)
            win = ext_ref[pl.ds(r0, wlen), c:c + ch]
            for rho in range(sub):
                sh = win if rho == 0 else pltpu.roll(win, wlen - rho, axis=0)
                for a in range(wlen // sub):
                    j = sub * a + rho - first
                    if 0 <= j < CONV_KERNEL:
                        acc = acc + sh[sub * a:sub * a + rows] * w_ref[j:j + 1, c:c + ch]
            y_ref[pl.ds(r0, rows), c:c + ch] = acc
        return carry

    lax.fori_loop(0, tm // rows, conv_rows, 0)

    def norm_rows(r, carry):
        r0 = pl.multiple_of(r * rows, rows)
        y = y_ref[pl.ds(r0, rows), :] + bdw_ref[...]
        mu = jnp.mean(y, axis=-1, keepdims=True)
        yc = y - mu
        var = jnp.mean(yc * yc, axis=-1, keepdims=True)
        z = yc * lax.rsqrt(var + EPS) * lng_ref[...] + lnb_ref[...]
        s = z * jax.nn.sigmoid(z)
        ms = jnp.mean(s * s, axis=-1, keepdims=True)
        o_ref[pl.ds(r0, rows), :] = (s * lax.rsqrt(ms + EPS) * gc_ref[...]).astype(BF16)
        return carry

    lax.fori_loop(0, tm // rows, norm_rows, 0)


def _conv_module(u, ctx, w_dw, b_dw, ln_g, ln_b, gc, batch, seq, tm):
    nt = seq // tm
    hb = tm // CONV_HALO
    kernel = functools.partial(_conv_kernel, tm=tm)
    return pl.pallas_call(
        kernel,
        grid=(batch, nt),
        in_specs=[
            pl.BlockSpec((tm, CONV_WIDTH), lambda b, i: (b * nt + i, 0)),
            pl.BlockSpec((CONV_HALO, CONV_WIDTH),
                         lambda b, i: (jnp.maximum((b * nt + i) * hb - 1, 0), 0)),
            pl.BlockSpec((1, CONV_HALO, CONV_WIDTH), lambda b, i: (b, 0, 0)),
            _resident((CONV_HALO, CONV_WIDTH)),
            _resident((1, CONV_WIDTH)), _resident((1, CONV_WIDTH)),
            _resident((1, CONV_WIDTH)), _resident((1, CONV_WIDTH)),
        ],
        out_specs=pl.BlockSpec((tm, CONV_WIDTH), lambda b, i: (b * nt + i, 0)),
        out_shape=jax.ShapeDtypeStruct((batch * seq, CONV_WIDTH), BF16),
        scratch_shapes=[pltpu.VMEM((CONV_HALO + tm, CONV_WIDTH), F32),
                        pltpu.VMEM((tm, CONV_WIDTH), F32)],
        compiler_params=_params(("parallel", "arbitrary")),
        name="conv_module",
    )(u, u, ctx, w_dw, b_dw, ln_g, ln_b, gc)


def _qk(q, k):
    return lax.dot_general(q, k, (((1,), (1,)), ((), ())), preferred_element_type=F32)


AUG_TERMS = 3
QK_AHEAD = 4


def _aug_kernel(f_ref, qa_ref, ka_ref):
    f = f_ref[...] * LOG2E
    tm = f.shape[0]
    lane = lax.broadcasted_iota(jnp.int32, (tm, HEAD_DIM), 1)
    for h in range(N_HEADS):
        hs = slice(h * HEAD_DIM, (h + 1) * HEAD_DIM)
        rest = f[:, h:h + 1]
        qa = jnp.where((lane >= AUG_TERMS) & (lane < 2 * AUG_TERMS), 1.0, 0.0)
        ka = jnp.where(lane < AUG_TERMS, 1.0, 0.0)
        for t in range(AUG_TERMS):
            piece = rest.astype(BF16).astype(F32)
            rest = rest - piece
            qa = jnp.where(lane == t, piece, qa)
            ka = jnp.where(lane == AUG_TERMS + t, -piece, ka)
        qa_ref[:, hs] = qa.astype(BF16)
        ka_ref[:, hs] = ka.astype(BF16)


def _attn_aug(f_col, tm):
    n = f_col.shape[0]
    return pl.pallas_call(
        _aug_kernel,
        grid=(n // tm,),
        in_specs=[pl.BlockSpec((tm, N_HEADS), lambda i: (i, 0))],
        out_specs=(pl.BlockSpec((tm, ATTN_WIDTH), lambda i: (i, 0)),
                   pl.BlockSpec((tm, ATTN_WIDTH), lambda i: (i, 0))),
        out_shape=(jax.ShapeDtypeStruct((n, ATTN_WIDTH), BF16),
                   jax.ShapeDtypeStruct((n, ATTN_WIDTH), BF16)),
        compiler_params=_params(("parallel",)),
        name="attn_aug",
    )(f_col)


def _attn_prompt_kernel(q_ref, qa_ref, k_ref, ka_ref, vt_ref, o_ref, m_ref, l_ref, acc_ref, qt_ref,
                        sp_ref, *, tq, tk):
    i = pl.program_id(1)
    key = lax.broadcasted_iota(jnp.int32, (tk, tq), 0)
    qry = lax.broadcasted_iota(jnp.int32, (tk, tq), 1)

    m_ref[...] = jnp.full(m_ref.shape, NEG_BIG, F32)
    l_ref[...] = jnp.zeros(l_ref.shape, F32)
    acc_ref[...] = jnp.zeros(acc_ref.shape, F32)
    for h in range(N_HEADS):
        hs = slice(h * HEAD_DIM, (h + 1) * HEAD_DIM)
        qt_ref[h] = jnp.concatenate([q_ref[:, hs], qa_ref[:, hs]], axis=1).T

    def scores(ks, h):
        hs = slice(h * HEAD_DIM, (h + 1) * HEAD_DIM)
        kf = jnp.concatenate([k_ref[pl.ds(ks, tk), hs], ka_ref[pl.ds(ks, tk), hs]], axis=1)
        return _dot(kf, qt_ref[h])

    def tile_step(ks, ks_next, masked):
        pending = [sp_ref[a] for a in range(QK_AHEAD)]
        for h in range(N_HEADS):
            hs = slice(h * HEAD_DIM, (h + 1) * HEAD_DIM)
            s = pending.pop(0)
            if h + QK_AHEAD < N_HEADS:
                pending.append(scores(ks, h + QK_AHEAD))
            elif ks_next is not None:
                pending.append(scores(ks_next, h + QK_AHEAD - N_HEADS))
            if masked:
                s = jnp.where(key <= qry, s, -jnp.inf)
            m = m_ref[h]
            m_new = jnp.maximum(m, jnp.max(s, axis=0, keepdims=True))
            alpha = jnp.exp2(m - m_new)
            p = jnp.exp2(s - m_new)
            m_ref[h] = m_new
            l_ref[h] = alpha * l_ref[h] + jnp.sum(p, axis=0, keepdims=True)
            pv = _dot(vt_ref[hs, pl.ds(ks, tk)], p.astype(BF16))
            acc_ref[h] = alpha * acc_ref[h] + pv
        for a, s in enumerate(pending):
            sp_ref[a] = s

    def body(j, carry):
        tile_step(pl.multiple_of(j * tk, tk), pl.multiple_of((j + 1) * tk, tk), False)
        return carry

    for a in range(QK_AHEAD):
        sp_ref[a] = scores(0, a)
    lax.fori_loop(0, i, body, 0)
    tile_step(pl.multiple_of(i * tk, tk), None, True)
    for h in range(N_HEADS):
        o_ref[:, h * HEAD_DIM:(h + 1) * HEAD_DIM] = (acc_ref[h] / l_ref[h]).T


def _attn_prompt(q, qa, kb, ka, vbt, batch, seq, tq, tk):
    nq = seq // tq
    kernel = functools.partial(_attn_prompt_kernel, tq=tq, tk=tk)
    qblk = pl.BlockSpec((tq, ATTN_WIDTH), lambda b, i: (b * nq + i, 0))
    kblk = pl.BlockSpec((seq, ATTN_WIDTH), lambda b, i: (b, 0), pipeline_mode=pl.Buffered(1))
    vblk = pl.BlockSpec((ATTN_WIDTH, seq), lambda b, i: (0, b), pipeline_mode=pl.Buffered(1))
    return pl.pallas_call(
        kernel,
        grid=(batch, nq),
        in_specs=[qblk, qblk, kblk, kblk, vblk],
        out_specs=pl.BlockSpec((tq, ATTN_WIDTH), lambda b, i: (b * nq + i, 0)),
        out_shape=jax.ShapeDtypeStruct((batch * seq, ATTN_WIDTH), F32),
        scratch_shapes=[pltpu.VMEM((N_HEADS, 1, tq), F32), pltpu.VMEM((N_HEADS, 1, tq), F32),
                        pltpu.VMEM((N_HEADS, HEAD_DIM, tq), F32),
                        pltpu.VMEM((N_HEADS, 2 * HEAD_DIM, tq), BF16),
                        pltpu.VMEM((QK_AHEAD, tk, tq), F32)],
        compiler_params=_params(("parallel", "arbitrary")),
        name="attn_prompt",
    )(q, qa, kb, ka, vbt)


def _attn_sample_kernel(q_ref, kn_ref, vn_ref, ck_ref, cv_ref, fc_ref, fnrow_ref, fncol_ref,
                        o_ref, *, t_new, past):
    row = lax.broadcasted_iota(jnp.int32, (t_new, t_new), 0)
    col = lax.broadcasted_iota(jnp.int32, (t_new, t_new), 1)
    causal = col <= row
    for h in range(N_HEADS):
        hs = slice(h * HEAD_DIM, (h + 1) * HEAD_DIM)
        q = q_ref[:, hs]
        fn_q = fncol_ref[:, h:h + 1]
        fc = fc_ref[0, h:h + 1, :]
        fc_last = fc[:, past - 1:past]
        s_c = _qk(q, ck_ref[0, :, h, :].astype(BF16)) + ((fc_last + fn_q) - fc) * LOG2E
        fn_k = fnrow_ref[0, h:h + 1, 0:t_new]
        s_n = _qk(q, kn_ref[:, hs]) + (fn_q - fn_k) * LOG2E
        s_n = jnp.where(causal, s_n, -jnp.inf)
        m = jnp.maximum(jnp.max(s_c, axis=-1, keepdims=True), jnp.max(s_n, axis=-1, keepdims=True))
        p_c = jnp.exp2(s_c - m)
        p_n = jnp.exp2(s_n - m)
        l = jnp.sum(p_c, axis=-1, keepdims=True) + jnp.sum(p_n, axis=-1, keepdims=True)
        acc = _dot(p_c.astype(BF16), cv_ref[0, :, h, :].astype(BF16)) + _dot(p_n.astype(BF16), vn_ref[:, hs])
        o_ref[:, hs] = acc / l


def _attn_sample(q, kb, vb, cache_k, cache_v, fc_row, fn_row, fn_col, batch, t_new, past):
    kernel = functools.partial(_attn_sample_kernel, t_new=t_new, past=past)
    tok = lambda w: pl.BlockSpec((t_new, w), lambda b: (b, 0))
    return pl.pallas_call(
        kernel,
        grid=(batch,),
        in_specs=[
            tok(ATTN_WIDTH), tok(ATTN_WIDTH), tok(ATTN_WIDTH),
            pl.BlockSpec((1, past, N_HEADS, HEAD_DIM), lambda b: (b, 0, 0, 0)),
            pl.BlockSpec((1, past, N_HEADS, HEAD_DIM), lambda b: (b, 0, 0, 0)),
            pl.BlockSpec((1, N_HEADS, past), lambda b: (b, 0, 0)),
            pl.BlockSpec((1, N_HEADS, LANES), lambda b: (b, 0, 0)),
            tok(N_HEADS),
        ],
        out_specs=tok(ATTN_WIDTH),
        out_shape=jax.ShapeDtypeStruct((batch * t_new, ATTN_WIDTH), F32),
        compiler_params=_params(("parallel",)),
        name="attn_sample",
    )(q, kb, vb, cache_k, cache_v, fc_row, fn_row, fn_col)


def _pack_bf16_pair(lo, hi):
    lo_bits = lax.bitcast_convert_type(lo, jnp.uint32) >> 16
    hi_bits = lax.bitcast_convert_type(hi, jnp.uint32) & jnp.uint32(0xFFFF0000)
    return lo_bits | hi_bits


def _unpack_bf16_pair(w):
    lo = lax.bitcast_convert_type(w << 16, F32).astype(BF16)
    hi = lax.bitcast_convert_type(w & jnp.uint32(0xFFFF0000), F32).astype(BF16)
    return lo, hi


def _merge_kernel(yc_ref, ya_ref, x_ref, ga_ref, wo_ref, gf_ref, wrh_ref, wrl_ref, br_ref,
                  h_ref, xp_ref, lg_ref):
    ya = ya_ref[...]
    ya_n = (ya * lax.rsqrt(jnp.mean(ya * ya, axis=-1, keepdims=True) + EPS) * ga_ref[...]).astype(BF16)
    y = _dot(yc_ref[...], wo_ref[0:CONV_WIDTH, :]) + _dot(ya_n, wo_ref[CONV_WIDTH:, :])
    h = x_ref[...] + y
    h_ref[...] = h
    xn = h * lax.rsqrt(jnp.mean(h * h, axis=-1, keepdims=True) + EPS) * gf_ref[...]
    xn_hi = xn.astype(BF16)
    xn_hi32 = xn_hi.astype(F32)
    xn_lo = (xn - xn_hi32).astype(BF16)
    lg_ref[...] = (_dot(xn_hi, wrh_ref[...]) + _dot(xn_lo, wrh_ref[...]) + _dot(xn_hi, wrl_ref[...])
                   + br_ref[...])
    half = D_MODEL // 2
    xp_ref[...] = _pack_bf16_pair(xn_hi32[:, :half], xn_hi32[:, half:])


def _merge(yc_n, ya, x, ga, w_out, g_ffn, wr_hi, wr_lo, b_r, tm):
    n = x.shape[0]
    row = lambda w: pl.BlockSpec((tm, w), lambda i: (i, 0))
    return pl.pallas_call(
        _merge_kernel,
        grid=(n // tm,),
        in_specs=[row(CONV_WIDTH), row(ATTN_WIDTH), row(D_MODEL), _resident((1, ATTN_WIDTH)),
                  _resident((D_MODEL, D_MODEL)), _resident((1, D_MODEL)),
                  _resident((D_MODEL, LANES)), _resident((D_MODEL, LANES)), _resident((1, LANES))],
        out_specs=(row(D_MODEL), row(D_MODEL // 2), row(LANES)),
        out_shape=(jax.ShapeDtypeStruct((n, D_MODEL), F32),
                   jax.ShapeDtypeStruct((n, D_MODEL // 2), jnp.uint32),
                   jax.ShapeDtypeStruct((n, LANES), F32)),
        compiler_params=_params(("parallel",)),
        name="merge_out",
    )(yc_n, ya, x, ga, w_out, g_ffn, wr_hi, wr_lo, b_r)


def _route_kernel(lg_ref, info_ref, infot_ref, cnt_ref, carry_ref, *, tm):
    step = pl.program_id(0)

    @pl.when(step == 0)
    def _():
        carry_ref[...] = jnp.zeros_like(carry_ref)

    lg = lg_ref[...]
    lane = lax.broadcasted_iota(jnp.int32, lg.shape, 1)
    lanef = lane.astype(F32)
    big = jnp.float32(1e9)
    rmax = lambda v: jnp.max(v, axis=-1, keepdims=True)
    rmin = lambda v: jnp.min(v, axis=-1, keepdims=True)
    rsum = lambda v: jnp.sum(v, axis=-1, keepdims=True)

    is_g = (lane >= N_EXPERTS) & (lane < N_EXPERTS + N_GROUPS)
    gl = jnp.where(is_g, lg, NEG_BIG)
    gmax = rmax(gl)
    gsum = rsum(jnp.where(is_g, jnp.exp(gl - gmax), 0.0))
    pg_star = 1.0 / gsum
    g_idx = rmin(jnp.where(is_g & (gl == gmax), lanef - N_EXPERTS, big))

    e_lo = g_idx * EXPERTS_PER_GROUP
    is_e = (lanef >= e_lo) & (lanef < e_lo + EXPERTS_PER_GROUP)
    el = jnp.where(is_e, lg, NEG_BIG)
    m1 = rmax(el)
    i1 = rmin(jnp.where(is_e & (el == m1), lanef, big))
    sel1 = lanef == i1
    el2 = jnp.where(sel1, NEG_BIG, el)
    m2 = rmax(el2)
    i2 = rmin(jnp.where(is_e & (el2 == m2) & jnp.logical_not(sel1), lanef, big))
    sel2 = lanef == i2
    z = rsum(jnp.where(is_e, jnp.exp(el - m1), 0.0))
    p1 = 1.0 / z
    p2 = jnp.exp(m2 - m1) / z
    gate1 = pg_star * p1 / (p1 + p2)
    gate2 = pg_star * p2 / (p1 + p2)

    onehot = jnp.where(sel1 | sel2, 1.0, 0.0)
    r = lax.broadcasted_iota(jnp.int32, (tm, tm), 0)
    c = lax.broadcasted_iota(jnp.int32, (tm, tm), 1)
    tri = jnp.where(c < r, 1.0, 0.0).astype(BF16)
    before = _dot(tri, onehot.astype(BF16)) + carry_ref[...]
    rank1 = rsum(jnp.where(sel1, before, 0.0))
    rank2 = rsum(jnp.where(sel2, before, 0.0))
    carry_ref[...] = carry_ref[...] + jnp.sum(onehot, axis=0, keepdims=True)
    cnt_ref[...] = carry_ref[...]

    info = jnp.zeros_like(lg)
    for k, val in enumerate((i1, i2, rank1, rank2, gate1, gate2)):
        info = jnp.where(lane == k, val, info)
    info_ref[...] = info
    infot_ref[...] = info.T[0:8, :]


def _route(logits, tm):
    n = logits.shape[0]
    kernel = functools.partial(_route_kernel, tm=tm)
    return pl.pallas_call(
        kernel,
        grid=(n // tm,),
        in_specs=[pl.BlockSpec((tm, LANES), lambda i: (i, 0))],
        out_specs=(pl.BlockSpec((tm, LANES), lambda i: (i, 0)),
                   pl.BlockSpec((8, tm), lambda i: (0, i)),
                   pl.BlockSpec((1, LANES), lambda i: (0, 0))),
        out_shape=(jax.ShapeDtypeStruct((n, LANES), F32), jax.ShapeDtypeStruct((8, n), F32),
                   jax.ShapeDtypeStruct((1, LANES), F32)),
        scratch_shapes=[pltpu.VMEM((1, LANES), F32)],
        compiler_params=_params(("arbitrary",)),
        name="route",
    )(logits)


DISPATCH_CHUNK = 128


def _dispatch_kernel(dest_ref, xa_ref, xb_ref, xs_in_hbm, xs_hbm, sem):
    del xs_in_hbm
    n_all = dest_ref.shape[0] // 2

    def scatter(src_ref, t_off):
        def row_copy(t, k):
            return pltpu.make_async_copy(src_ref.at[pl.ds(t, 1)],
                                         xs_hbm.at[pl.ds(dest_ref[k * n_all + t_off + t], 1)], sem)

        def issue(c):
            def body(r, carry):
                row_copy(c * DISPATCH_CHUNK + r, 0).start()
                row_copy(c * DISPATCH_CHUNK + r, 1).start()
                return carry
            lax.fori_loop(0, DISPATCH_CHUNK, body, 0, unroll=8)

        def drain(c):
            def body(r, carry):
                row_copy(c * DISPATCH_CHUNK + r, 0).wait()
                row_copy(c * DISPATCH_CHUNK + r, 1).wait()
                return carry
            lax.fori_loop(0, DISPATCH_CHUNK, body, 0, unroll=8)

        n_chunks = src_ref.shape[0] // DISPATCH_CHUNK

        def chunk(c, carry):
            issue(c)

            @pl.when(c > 0)
            def _():
                drain(c - 1)
            return carry

        lax.fori_loop(0, n_chunks, chunk, 0)
        drain(n_chunks - 1)

    scatter(xa_ref, 0)
    scatter(xb_ref, xa_ref.shape[0])


def _dispatch(dest_flat, xa, xb, n_slots):
    w = xa.shape[1]
    xs0 = jnp.zeros((n_slots, w), xa.dtype)
    vmem = pl.BlockSpec(memory_space=pltpu.VMEM)
    grid_spec = pltpu.PrefetchScalarGridSpec(
        num_scalar_prefetch=1,
        grid=(1,),
        in_specs=[vmem, vmem, pl.BlockSpec(memory_space=pl.ANY)],
        out_specs=pl.BlockSpec(memory_space=pl.ANY),
        scratch_shapes=[pltpu.SemaphoreType.DMA(())],
    )
    return pl.pallas_call(
        _dispatch_kernel,
        grid_spec=grid_spec,
        out_shape=jax.ShapeDtypeStruct((n_slots, w), xa.dtype),
        input_output_aliases={3: 0},
        compiler_params=pltpu.CompilerParams(dimension_semantics=("arbitrary",),
                                             vmem_limit_bytes=VMEM_LIMIT, has_side_effects=True),
        name="dispatch",
    )(dest_flat, xa, xb, xs0)


def _expert_kernel(be_ref, nu_ref, xs_ref, w1_ref, w3_ref, w2_ref, y_ref, w1b_ref, w3b_ref, w2b_ref):
    b = pl.program_id(0)
    active = b < nu_ref[0]
    new_expert = jnp.logical_or(b == 0, be_ref[b] != be_ref[jnp.maximum(b - 1, 0)])

    @pl.when(jnp.logical_and(active, new_expert))
    def _():
        w1b_ref[...] = w1_ref[0].astype(BF16)
        w3b_ref[...] = w3_ref[0].astype(BF16)
        w2b_ref[...] = w2_ref[0].astype(BF16)

    @pl.when(active)
    def _():
        lo, hi = _unpack_bf16_pair(xs_ref[...])
        half = D_MODEL // 2
        h1 = _dot(lo, w1b_ref[:half, :]) + _dot(hi, w1b_ref[half:, :])
        h3 = _dot(lo, w3b_ref[:half, :]) + _dot(hi, w3b_ref[half:, :])
        h = (h1 * jax.nn.sigmoid(h1) * h3).astype(BF16)
        y_ref[...] = _dot(h, w2b_ref[...])

    @pl.when(jnp.logical_not(active))
    def _():
        y_ref[...] = jnp.zeros_like(y_ref)


def _experts(block_e, n_used, xs, w1, w3, w2):
    n_slots = xs.shape[0]
    nb = n_slots // MOE_BLOCK
    blk = lambda b, be, nu: (jnp.minimum(b, nu[0] - 1), 0)
    wsel = lambda b, be, nu: (be[jnp.minimum(b, nu[0] - 1)], 0, 0)
    grid_spec = pltpu.PrefetchScalarGridSpec(
        num_scalar_prefetch=2,
        grid=(nb,),
        in_specs=[pl.BlockSpec((MOE_BLOCK, D_MODEL // 2), blk),
                  pl.BlockSpec((1, D_MODEL, D_EXPERT), wsel),
                  pl.BlockSpec((1, D_MODEL, D_EXPERT), wsel),
                  pl.BlockSpec((1, D_EXPERT, D_MODEL), wsel)],
        out_specs=pl.BlockSpec((MOE_BLOCK, D_MODEL), lambda b, be, nu: (b, 0)),
        scratch_shapes=[pltpu.VMEM((D_MODEL, D_EXPERT), BF16), pltpu.VMEM((D_MODEL, D_EXPERT), BF16),
                        pltpu.VMEM((D_EXPERT, D_MODEL), BF16)],
    )
    return pl.pallas_call(
        _expert_kernel,
        grid_spec=grid_spec,
        out_shape=jax.ShapeDtypeStruct((n_slots, D_MODEL), F32),
        compiler_params=_params(("arbitrary",)),
        name="experts",
    )(block_e, n_used, xs, w1, w3, w2)


def _combine_kernel(dest_ref, h_ref, info_ref, yb_hbm, o_ref, buf_ref, sem, *, tm, t_off):
    i = pl.program_id(0)
    nsteps = pl.num_programs(0)
    n_all = dest_ref.shape[0] // 2

    def row_copy(step, slot, r, k):
        t = t_off + step * tm + r
        return pltpu.make_async_copy(yb_hbm.at[pl.ds(dest_ref[k * n_all + t], 1)],
                                     buf_ref.at[slot, k, pl.ds(r, 1)], sem.at[slot])

    def issue(step, slot):
        def body(r, carry):
            row_copy(step, slot, r, 0).start()
            row_copy(step, slot, r, 1).start()
            return carry
        lax.fori_loop(0, tm, body, 0, unroll=8)

    def drain(step, slot):
        del step

        def body(r, carry):
            for k in range(2):
                pltpu.make_async_copy(yb_hbm.at[pl.ds(0, 1)], buf_ref.at[slot, k, pl.ds(r, 1)],
                                      sem.at[slot]).wait()
            return carry
        lax.fori_loop(0, tm, body, 0, unroll=8)

    slot = i % 2

    @pl.when(i == 0)
    def _():
        issue(0, 0)

    @pl.when(i + 1 < nsteps)
    def _():
        issue(i + 1, 1 - slot)

    drain(i, slot)
    g0 = info_ref[:, 4:5]
    g1 = info_ref[:, 5:6]
    o_ref[...] = h_ref[...] + (g0 * buf_ref[slot, 0] + g1 * buf_ref[slot, 1])


def _combine(dest_flat, h, info, yb, t_off, tm):
    n = h.shape[0]
    kernel = functools.partial(_combine_kernel, tm=tm, t_off=t_off)
    ob = t_off // tm
    grid_spec = pltpu.PrefetchScalarGridSpec(
        num_scalar_prefetch=1,
        grid=(n // tm,),
        in_specs=[pl.BlockSpec((tm, D_MODEL), lambda i, d: (i, 0)),
                  pl.BlockSpec((tm, LANES), lambda i, d: (i + ob, 0)),
                  pl.BlockSpec(memory_space=pl.ANY)],
        out_specs=pl.BlockSpec((tm, D_MODEL), lambda i, d: (i, 0)),
        scratch_shapes=[pltpu.VMEM((2, 2, tm, D_MODEL), F32), pltpu.SemaphoreType.DMA((2,))],
    )
    return pl.pallas_call(
        kernel,
        grid_spec=grid_spec,
        out_shape=jax.ShapeDtypeStruct((n, D_MODEL), F32),
        compiler_params=_params(("arbitrary",)),
        name="combine",
    )(dest_flat, h, info, yb)


def _ple_kernel(h_ref, p_ref, g_ref, wpg_ref, wpp_ref, o_ref):
    h = h_ref[...]
    hn = (h * lax.rsqrt(jnp.mean(h * h, axis=-1, keepdims=True) + EPS) * g_ref[...]).astype(BF16)
    pb = p_ref[...].astype(BF16)
    ch = 512
    for c in range(0, D_MODEL, ch):
        gate = jax.nn.sigmoid(_dot(hn, wpg_ref[:, c:c + ch]))
        o_ref[:, c:c + ch] = h_ref[:, c:c + ch] + gate * _dot(pb, wpp_ref[:, c:c + ch])


def _ple(h, p, g_ple, w_pg, w_pp, tm):
    n = h.shape[0]
    row = lambda w: pl.BlockSpec((tm, w), lambda i: (i, 0))
    return pl.pallas_call(
        _ple_kernel,
        grid=(n // tm,),
        in_specs=[row(D_MODEL), row(PLE_DIM), _resident((1, D_MODEL)),
                  _resident((D_MODEL, D_MODEL)), _resident((PLE_DIM, D_MODEL))],
        out_specs=row(D_MODEL),
        out_shape=jax.ShapeDtypeStruct((n, D_MODEL), F32),
        compiler_params=_params(("parallel",)),
        name="ple",
    )(h, p, g_ple, w_pg, w_pp)


def _mixer_tokens(x2d, wts, tm):
    return _in_proj(x2d, wts["g_mix"], wts["w_main"], wts["w_f"], wts["b_f"],
                    wts["q_gain"], wts["k_gain"], tm)


def kernel(x_prompt, x_sample, cache_k, cache_v, cache_logf, cache_conv, p_prompt, p_sample,
           g_mix, w_in, b_f, q_gain, k_gain, w_dw, b_dw, ln_g, ln_b, gc, ga, w_out,
           g_ffn, w_router_g, b_router_g, w_router_e, b_router_e, w1, w3, w2,
           g_ple, w_pg, w_pp):
    batch, seq, _ = x_prompt.shape
    dec_batch, t_new, _ = x_sample.shape
    past = cache_k.shape[2]
    n_p = batch * seq
    n_s = dec_batch * t_new
    n_all = n_p + n_s
    tm = 256
    li = 0

    w_in_l = w_in[li]
    pad_lanes = lambda a: jnp.pad(a, ((0, 0), (0, LANES - a.shape[1])))
    row2d = lambda a: a.reshape(1, -1)
    wts = {
        "g_mix": row2d(g_mix[li]),
        "w_main": w_in_l[:, :MAIN_COLS].astype(BF16),
        "w_f": pad_lanes(w_in_l[:, MAIN_COLS:]).astype(BF16),
        "b_f": pad_lanes(row2d(b_f[li])),
        "q_gain": row2d(q_gain[li]),
        "k_gain": row2d(k_gain[li]),
    }
    w_dw_p = jnp.pad(w_dw[li], ((0, CONV_HALO - CONV_KERNEL), (0, 0)))
    w_out_b = w_out[li].astype(BF16)
    w_r = pad_lanes(jnp.concatenate([w_router_e[li], w_router_g[li]], axis=1))
    wr_hi = w_r.astype(BF16)
    wr_lo = (w_r - wr_hi.astype(F32)).astype(BF16)
    b_r = pad_lanes(row2d(jnp.concatenate([b_router_e[li], b_router_g[li]])))
    w_pg_b, w_pp_b = w_pg[li].astype(BF16), w_pp[li].astype(BF16)

    xp = x_prompt.reshape(n_p, D_MODEL)
    u_p, q_p, k_p, kb_p, v_p, _, vbt_p, lf_p = _mixer_tokens(xp, wts, tm)
    lf_p_row = lf_p.reshape(batch, seq, N_HEADS).transpose(0, 2, 1)
    f_p_row = _cumsum_lanes(lf_p_row.reshape(batch * N_HEADS, seq)).reshape(batch, N_HEADS, seq)
    f_p_col = f_p_row.transpose(0, 2, 1).reshape(n_p, N_HEADS)
    ctx_p = jnp.zeros((batch, CONV_HALO, CONV_WIDTH), F32)
    conv_args = (w_dw_p, row2d(b_dw[li]), row2d(ln_g[li]), row2d(ln_b[li]), row2d(gc[li]))
    yc_p = _conv_module(u_p, ctx_p, *conv_args, batch, seq, tm)
    qa_p, ka_p = _attn_aug(f_p_col, tm)
    ya_p = _attn_prompt(q_p, qa_p, kb_p, ka_p, vbt_p, batch, seq, 256, 256)
    merge_args = (row2d(ga[li]), w_out_b, row2d(g_ffn[li]), wr_hi, wr_lo, b_r)
    h_p, xpk_p, lg_p = _merge(yc_p, ya_p, xp, *merge_args, tm)

    xs_ = x_sample.reshape(n_s, D_MODEL)
    u_s, q_s, k_s, kb_s, v_s, vb_s, _, lf_s = _mixer_tokens(xs_, wts, tm)
    clf_row = cache_logf[li].transpose(0, 2, 1).reshape(dec_batch * N_HEADS, past)
    fc_row = _cumsum_lanes(clf_row).reshape(dec_batch, N_HEADS, past)
    lf_s_row = lf_s.reshape(dec_batch, t_new, N_HEADS).transpose(0, 2, 1).reshape(dec_batch * N_HEADS, t_new)
    fn_row = _cumsum_lanes(jnp.pad(lf_s_row, ((0, 0), (0, LANES - t_new)))).reshape(dec_batch, N_HEADS, LANES)
    fn_col = fn_row[:, :, :t_new].transpose(0, 2, 1).reshape(n_s, N_HEADS)
    ctx_s = jnp.pad(cache_conv[li], ((0, 0), (CONV_HALO - CONV_STATE, 0), (0, 0)))
    yc_s = _conv_module(u_s, ctx_s, *conv_args, dec_batch, t_new, t_new)
    ya_s = _attn_sample(q_s, kb_s, vb_s,
                        cache_k[li], cache_v[li],
                        fc_row, fn_row, fn_col, dec_batch, t_new, past)
    h_s, xpk_s, lg_s = _merge(yc_s, ya_s, xs_, *merge_args, tm)

    info, info_t, counts = _route(jnp.concatenate([lg_p, lg_s], axis=0), tm)
    counts = counts[0, :N_EXPERTS].astype(jnp.int32)
    bcounts = (counts + MOE_BLOCK - 1) // MOE_BLOCK
    bends = jnp.cumsum(bcounts)
    pstarts = (bends - bcounts) * MOE_BLOCK
    n_rows = n_all * 2
    nb = -(-(n_rows + N_EXPERTS * (MOE_BLOCK - 1)) // MOE_BLOCK)
    e_idx = info_t[0:2].astype(jnp.int32)
    dest = (pstarts[e_idx] + info_t[2:4].astype(jnp.int32)).reshape(n_rows)
    block_e = jnp.minimum(jnp.sum(bends[None, :] <= jnp.arange(nb, dtype=jnp.int32)[:, None], axis=1),
                          N_EXPERTS - 1).astype(jnp.int32)
    n_used = bends[N_EXPERTS - 1:].astype(jnp.int32)
    xs_sorted = _dispatch(dest, xpk_p, xpk_s, nb * MOE_BLOCK)
    yb = _experts(block_e, n_used, xs_sorted, w1[li], w3[li], w2[li])
    h2_p = _combine(dest, h_p, info, yb, 0, tm)
    h2_s = _combine(dest, h_s, info, yb, n_p, tm)

    y_p = _ple(h2_p, p_prompt[li].reshape(n_p, PLE_DIM), row2d(g_ple[li]), w_pg_b, w_pp_b, tm)
    y_s = _ple(h2_s, p_sample[li].reshape(n_s, PLE_DIM), row2d(g_ple[li]), w_pg_b, w_pp_b, tm)

    heads = lambda a, b, t: a.reshape(1, b, t, N_HEADS, HEAD_DIM)
    return (
        y_p.reshape(batch, seq, D_MODEL),
        y_s.reshape(dec_batch, t_new, D_MODEL),
        heads(k_p, batch, seq), heads(v_p, batch, seq),
        lf_p.reshape(1, batch, seq, N_HEADS),
        u_p.reshape(batch, seq, CONV_WIDTH)[None, :, seq - CONV_STATE:, :],
        heads(k_s, dec_batch, t_new), heads(v_s, dec_batch, t_new),
        lf_s.reshape(1, dec_batch, t_new, N_HEADS),
        u_s.reshape(dec_batch, t_new, CONV_WIDTH)[None, :, t_new - CONV_STATE:, :],
    )
```

```python
import functools

import jax
import jax.numpy as jnp
from jax import lax
from jax.experimental import pallas as pl
from jax.experimental.pallas import tpu as pltpu

D_MODEL = 2048
CONV_WIDTH = 1024
ATTN_WIDTH = 1024
HEAD_DIM = 128
N_HEADS = 8
CONV_KERNEL = 31
CONV_STATE = CONV_KERNEL - 1
N_GROUPS = 4
EXPERTS_PER_GROUP = 8
N_EXPERTS = 32
D_EXPERT = 512
PLE_DIM = 256
MOE_BLOCK = 128
EPS = 1e-6
MAIN_COLS = 2 * CONV_WIDTH + 3 * ATTN_WIDTH
Q_OFF = 2 * CONV_WIDTH
K_OFF = Q_OFF + ATTN_WIDTH
V_OFF = K_OFF + ATTN_WIDTH

LANES = 128
CONV_HALO = 32
VMEM_LIMIT = 56 * 1024 * 1024

F32 = jnp.float32
BF16 = jnp.bfloat16
NEG_BIG = -1e30
LOG2E = 1.4426950408889634


def _dot(a, b):
    return jnp.dot(a, b, preferred_element_type=F32)


def _params(sem):
    return pltpu.CompilerParams(dimension_semantics=sem, vmem_limit_bytes=VMEM_LIMIT)


def _resident(shape):
    return pl.BlockSpec(shape, lambda *_: (0,) * len(shape), pipeline_mode=pl.Buffered(1))


def _inproj_kernel(x_ref, g_ref, w_ref, wf_ref, bf_ref, qg_ref, kg_ref,
                   u_ref, q_ref, k_ref, kb_ref, v_ref, vb_ref, vbt_ref, lf_ref):
    x = x_ref[...]
    ms = jnp.mean(x * x, axis=-1, keepdims=True)
    a = (x * lax.rsqrt(ms + EPS) * g_ref[...]).astype(BF16)

    ch = 256
    for c in range(0, CONV_WIDTH, ch):
        val = _dot(a, w_ref[:, c:c + ch])
        gate = _dot(a, w_ref[:, CONV_WIDTH + c:CONV_WIDTH + c + ch])
        u_ref[:, c:c + ch] = val * jax.nn.sigmoid(gate)

    def head_norm(z, gain):
        return z * lax.rsqrt(jnp.mean(z * z, axis=-1, keepdims=True) + EPS) * gain

    scale = LOG2E * HEAD_DIM ** -0.5
    for c in range(0, ATTN_WIDTH, ch):
        zq = _dot(a, w_ref[:, Q_OFF + c:Q_OFF + c + ch])
        zk = _dot(a, w_ref[:, K_OFF + c:K_OFF + c + ch])
        zv = _dot(a, w_ref[:, V_OFF + c:V_OFF + c + ch])
        for s in range(0, ch, HEAD_DIM):
            qn = head_norm(zq[:, s:s + HEAD_DIM], qg_ref[...])
            kn = head_norm(zk[:, s:s + HEAD_DIM], kg_ref[...])
            q_ref[:, c + s:c + s + HEAD_DIM] = (qn * scale).astype(BF16)
            k_ref[:, c + s:c + s + HEAD_DIM] = kn
            kb_ref[:, c + s:c + s + HEAD_DIM] = kn.astype(BF16)
        v_ref[:, c:c + ch] = zv
        vb_ref[:, c:c + ch] = zv.astype(BF16)
        vbt_ref[c:c + ch, :] = zv.T.astype(BF16)

    f = _dot(a, wf_ref[...]) + bf_ref[...]
    lf = jnp.minimum(f, 0.0) - jnp.log1p(jnp.exp(-jnp.abs(f)))
    lf_ref[...] = lf[:, :N_HEADS]


def _in_proj(x, g_mix, w_main, w_f, b_f, q_gain, k_gain, tm):
    n = x.shape[0]
    row = lambda w: pl.BlockSpec((tm, w), lambda i: (i, 0))
    out_shape = (
        jax.ShapeDtypeStruct((n, CONV_WIDTH), F32),
        jax.ShapeDtypeStruct((n, ATTN_WIDTH), BF16),
        jax.ShapeDtypeStruct((n, ATTN_WIDTH), F32),
        jax.ShapeDtypeStruct((n, ATTN_WIDTH), BF16),
        jax.ShapeDtypeStruct((n, ATTN_WIDTH), F32),
        jax.ShapeDtypeStruct((n, ATTN_WIDTH), BF16),
        jax.ShapeDtypeStruct((ATTN_WIDTH, n), BF16),
        jax.ShapeDtypeStruct((n, N_HEADS), F32),
    )
    return pl.pallas_call(
        _inproj_kernel,
        grid=(n // tm,),
        in_specs=[row(D_MODEL), _resident((1, D_MODEL)), _resident((D_MODEL, MAIN_COLS)),
                  _resident((D_MODEL, LANES)), _resident((1, LANES)),
                  _resident((1, HEAD_DIM)), _resident((1, HEAD_DIM))],
        out_specs=(row(CONV_WIDTH), row(ATTN_WIDTH), row(ATTN_WIDTH), row(ATTN_WIDTH),
                   row(ATTN_WIDTH), row(ATTN_WIDTH),
                   pl.BlockSpec((ATTN_WIDTH, tm), lambda i: (0, i)), row(N_HEADS)),
        out_shape=out_shape,
        compiler_params=_params(("parallel",)),
        name="in_proj",
    )(x, g_mix, w_main, w_f, b_f, q_gain, k_gain)


def _cumsum_kernel(x_ref, o_ref):
    x = x_ref[...]
    width = x.shape[1]
    lane = lax.broadcasted_iota(jnp.int32, x.shape, 1)
    s = 1
    while s < width:
        x = x + jnp.where(lane >= s, pltpu.roll(x, s, axis=1), 0.0)
        s *= 2
    o_ref[...] = x


def _cumsum_lanes(x):
    return pl.pallas_call(
        _cumsum_kernel,
        out_shape=jax.ShapeDtypeStruct(x.shape, F32),
        name="cumsum",
    )(x)


def _conv_kernel(u_ref, halo_ref, ctx_ref, w_ref, bdw_ref, lng_ref, lnb_ref, gc_ref,
                 o_ref, ext_ref, y_ref, *, tm):
    i = pl.program_id(1)

    @pl.when(i == 0)
    def _():
        ext_ref[0:CONV_HALO, :] = ctx_ref[0]

    @pl.when(i > 0)
    def _():
        ext_ref[0:CONV_HALO, :] = halo_ref[...]

    ext_ref[CONV_HALO:CONV_HALO + tm, :] = u_ref[...]

    rows = min(64, tm)
    ch = LANES
    sub = 8
    wlen = rows + CONV_HALO
    first = CONV_HALO - CONV_STATE

    def conv_rows(r, carry):
        r0 = pl.multiple_of(r * rows, rows)
        for c in range(0, CONV_WIDTH, ch):
            acc = jnp.zeros((rows, ch), F32)
            win = ext_ref[pl.ds(r0, wlen), c:c + ch]
            for rho in range(sub):
                sh = win if rho == 0 else pltpu.roll(win, wlen - rho, axis=0)
                for a in range(wlen // sub):
                    j = sub * a + rho - first
                    if 0 <= j < CONV_KERNEL:
                        acc = acc + sh[sub * a:sub * a + rows] * w_ref[j:j + 1, c:c + ch]
            y_ref[pl.ds(r0, rows), c:c + ch] = acc
        return carry

    lax.fori_loop(0, tm // rows, conv_rows, 0)

    def norm_rows(r, carry):
        r0 = pl.multiple_of(r * rows, rows)
        y = y_ref[pl.ds(r0, rows), :] + bdw_ref[...]
        mu = jnp.mean(y, axis=-1, keepdims=True)
        yc = y - mu
        var = jnp.mean(yc * yc, axis=-1, keepdims=True)
        z = yc * lax.rsqrt(var + EPS) * lng_ref[...] + lnb_ref[...]
        s = z * jax.nn.sigmoid(z)
        ms = jnp.mean(s * s, axis=-1, keepdims=True)
        o_ref[pl.ds(r0, rows), :] = (s * lax.rsqrt(ms + EPS) * gc_ref[...]).astype(BF16)
        return carry

    lax.fori_loop(0, tm // rows, norm_rows, 0)


def _conv_module(u, ctx, w_dw, b_dw, ln_g, ln_b, gc, batch, seq, tm):
    nt = seq // tm
    hb = tm // CONV_HALO
    kernel = functools.partial(_conv_kernel, tm=tm)
    return pl.pallas_call(
        kernel,
        grid=(batch, nt),
        in_specs=[
            pl.BlockSpec((tm, CONV_WIDTH), lambda b, i: (b * nt + i, 0)),
            pl.BlockSpec((CONV_HALO, CONV_WIDTH),
                         lambda b, i: (jnp.maximum((b * nt + i) * hb - 1, 0), 0)),
            pl.BlockSpec((1, CONV_HALO, CONV_WIDTH), lambda b, i: (b, 0, 0)),
            _resident((CONV_HALO, CONV_WIDTH)),
            _resident((1, CONV_WIDTH)), _resident((1, CONV_WIDTH)),
            _resident((1, CONV_WIDTH)), _resident((1, CONV_WIDTH)),
        ],
        out_specs=pl.BlockSpec((tm, CONV_WIDTH), lambda b, i: (b * nt + i, 0)),
        out_shape=jax.ShapeDtypeStruct((batch * seq, CONV_WIDTH), BF16),
        scratch_shapes=[pltpu.VMEM((CONV_HALO + tm, CONV_WIDTH), F32),
                        pltpu.VMEM((tm, CONV_WIDTH), F32)],
        compiler_params=_params(("parallel", "arbitrary")),
        name="conv_module",
    )(u, u, ctx, w_dw, b_dw, ln_g, ln_b, gc)


def _qk(q, k):
    return lax.dot_general(q, k, (((1,), (1,)), ((), ())), preferred_element_type=F32)


AUG_TERMS = 3
QK_AHEAD = 4


def _aug_kernel(f_ref, qa_ref, ka_ref):
    f = f_ref[...] * LOG2E
    tm = f.shape[0]
    lane = lax.broadcasted_iota(jnp.int32, (tm, HEAD_DIM), 1)
    for h in range(N_HEADS):
        hs = slice(h * HEAD_DIM, (h + 1) * HEAD_DIM)
        rest = f[:, h:h + 1]
        qa = jnp.where((lane >= AUG_TERMS) & (lane < 2 * AUG_TERMS), 1.0, 0.0)
        ka = jnp.where(lane < AUG_TERMS, 1.0, 0.0)
        for t in range(AUG_TERMS):
            piece = rest.astype(BF16).astype(F32)
            rest = rest - piece
            qa = jnp.where(lane == t, piece, qa)
            ka = jnp.where(lane == AUG_TERMS + t, -piece, ka)
        qa_ref[:, hs] = qa.astype(BF16)
        ka_ref[:, hs] = ka.astype(BF16)


def _attn_aug(f_col, tm):
    n = f_col.shape[0]
    return pl.pallas_call(
        _aug_kernel,
        grid=(n // tm,),
        in_specs=[pl.BlockSpec((tm, N_HEADS), lambda i: (i, 0))],
        out_specs=(pl.BlockSpec((tm, ATTN_WIDTH), lambda i: (i, 0)),
                   pl.BlockSpec((tm, ATTN_WIDTH), lambda i: (i, 0))),
        out_shape=(jax.ShapeDtypeStruct((n, ATTN_WIDTH), BF16),
                   jax.ShapeDtypeStruct((n, ATTN_WIDTH), BF16)),
        compiler_params=_params(("parallel",)),
        name="attn_aug",
    )(f_col)


def _attn_prompt_kernel(q_ref, qa_ref, k_ref, ka_ref, vt_ref, o_ref, m_ref, l_ref, acc_ref, qt_ref,
                        sp_ref, *, tq, tk):
    i = pl.program_id(1)
    key = lax.broadcasted_iota(jnp.int32, (tk, tq), 0)
    qry = lax.broadcasted_iota(jnp.int32, (tk, tq), 1)

    m_ref[...] = jnp.full(m_ref.shape, NEG_BIG, F32)
    l_ref[...] = jnp.zeros(l_ref.shape, F32)
    acc_ref[...] = jnp.zeros(acc_ref.shape, F32)
    for h in range(N_HEADS):
        hs = slice(h * HEAD_DIM, (h + 1) * HEAD_DIM)
        qt_ref[h] = jnp.concatenate([q_ref[:, hs], qa_ref[:, hs]], axis=1).T

    def scores(ks, h):
        hs = slice(h * HEAD_DIM, (h + 1) * HEAD_DIM)
        kf = jnp.concatenate([k_ref[pl.ds(ks, tk), hs], ka_ref[pl.ds(ks, tk), hs]], axis=1)
        return _dot(kf, qt_ref[h])

    def tile_step(ks, ks_next, masked):
        pending = [sp_ref[a] for a in range(QK_AHEAD)]
        for h in range(N_HEADS):
            hs = slice(h * HEAD_DIM, (h + 1) * HEAD_DIM)
            s = pending.pop(0)
            if h + QK_AHEAD < N_HEADS:
                pending.append(scores(ks, h + QK_AHEAD))
            elif ks_next is not None:
                pending.append(scores(ks_next, h + QK_AHEAD - N_HEADS))
            if masked:
                s = jnp.where(key <= qry, s, -jnp.inf)
            m = m_ref[h]
            m_new = jnp.maximum(m, jnp.max(s, axis=0, keepdims=True))
            alpha = jnp.exp2(m - m_new)
            p = jnp.exp2(s - m_new)
            m_ref[h] = m_new
            l_ref[h] = alpha * l_ref[h] + jnp.sum(p, axis=0, keepdims=True)
            pv = _dot(vt_ref[hs, pl.ds(ks, tk)], p.astype(BF16))
            acc_ref[h] = alpha * acc_ref[h] + pv
        for a, s in enumerate(pending):
            sp_ref[a] = s

    def body(j, carry):
        tile_step(pl.multiple_of(j * tk, tk), pl.multiple_of((j + 1) * tk, tk), False)
        return carry

    for a in range(QK_AHEAD):
        sp_ref[a] = scores(0, a)
    lax.fori_loop(0, i, body, 0)
    tile_step(pl.multiple_of(i * tk, tk), None, True)
    for h in range(N_HEADS):
        o_ref[:, h * HEAD_DIM:(h + 1) * HEAD_DIM] = (acc_ref[h] / l_ref[h]).T


def _attn_prompt(q, qa, kb, ka, vbt, batch, seq, tq, tk):
    nq = seq // tq
    kernel = functools.partial(_attn_prompt_kernel, tq=tq, tk=tk)
    qblk = pl.BlockSpec((tq, ATTN_WIDTH), lambda b, i: (b * nq + i, 0))
    kblk = pl.BlockSpec((seq, ATTN_WIDTH), lambda b, i: (b, 0), pipeline_mode=pl.Buffered(1))
    vblk = pl.BlockSpec((ATTN_WIDTH, seq), lambda b, i: (0, b), pipeline_mode=pl.Buffered(1))
    return pl.pallas_call(
        kernel,
        grid=(batch, nq),
        in_specs=[qblk, qblk, kblk, kblk, vblk],
        out_specs=pl.BlockSpec((tq, ATTN_WIDTH), lambda b, i: (b * nq + i, 0)),
        out_shape=jax.ShapeDtypeStruct((batch * seq, ATTN_WIDTH), F32),
        scratch_shapes=[pltpu.VMEM((N_HEADS, 1, tq), F32), pltpu.VMEM((N_HEADS, 1, tq), F32),
                        pltpu.VMEM((N_HEADS, HEAD_DIM, tq), F32),
                        pltpu.VMEM((N_HEADS, 2 * HEAD_DIM, tq), BF16),
                        pltpu.VMEM((QK_AHEAD, tk, tq), F32)],
        compiler_params=_params(("parallel", "arbitrary")),
        name="attn_prompt",
    )(q, qa, kb, ka, vbt)


def _attn_sample_kernel(q_ref, kn_ref, vn_ref, ck_hbm, cv_hbm, fc_ref, fnrow_ref, fncol_ref,
                        o_ref, kbuf_ref, vbuf_ref, sem, *, t_new, past):
    b = pl.program_id(0)
    slot = b % 2

    def fetch(batch_idx, into):
        copies = []
        for h in range(N_HEADS):
            copies.append(pltpu.make_async_copy(ck_hbm.at[batch_idx, :, h, :], kbuf_ref.at[into, h],
                                                sem.at[into, 0]))
            copies.append(pltpu.make_async_copy(cv_hbm.at[batch_idx, :, h, :], vbuf_ref.at[into, h],
                                                sem.at[into, 1]))
        return copies

    @pl.when(b == 0)
    def _():
        for copy in fetch(0, 0):
            copy.start()

    @pl.when(b + 1 < pl.num_programs(0))
    def _():
        for copy in fetch(b + 1, 1 - slot):
            copy.start()

    for copy in fetch(b, slot):
        copy.wait()

    row = lax.broadcasted_iota(jnp.int32, (t_new, t_new), 0)
    col = lax.broadcasted_iota(jnp.int32, (t_new, t_new), 1)
    causal = col <= row
    for h in range(N_HEADS):
        hs = slice(h * HEAD_DIM, (h + 1) * HEAD_DIM)
        q = q_ref[:, hs]
        fn_q = fncol_ref[:, h:h + 1]
        fc = fc_ref[0, h:h + 1, :]
        fc_last = fc[:, past - 1:past]
        s_c = _qk(q, kbuf_ref[slot, h].astype(BF16)) + ((fc_last + fn_q) - fc) * LOG2E
        fn_k = fnrow_ref[0, h:h + 1, 0:t_new]
        s_n = _qk(q, kn_ref[:, hs]) + (fn_q - fn_k) * LOG2E
        s_n = jnp.where(causal, s_n, -jnp.inf)
        m = jnp.maximum(jnp.max(s_c, axis=-1, keepdims=True), jnp.max(s_n, axis=-1, keepdims=True))
        p_c = jnp.exp2(s_c - m)
        p_n = jnp.exp2(s_n - m)
        l = jnp.sum(p_c, axis=-1, keepdims=True) + jnp.sum(p_n, axis=-1, keepdims=True)
        acc = _dot(p_c.astype(BF16), vbuf_ref[slot, h].astype(BF16)) + _dot(p_n.astype(BF16), vn_ref[:, hs])
        o_ref[:, hs] = acc / l


def _attn_sample(q, kb, vb, cache_k, cache_v, fc_row, fn_row, fn_col, batch, t_new, past):
    kernel = functools.partial(_attn_sample_kernel, t_new=t_new, past=past)
    tok = lambda w: pl.BlockSpec((t_new, w), lambda b: (b, 0))
    return pl.pallas_call(
        kernel,
        grid=(batch,),
        in_specs=[
            tok(ATTN_WIDTH), tok(ATTN_WIDTH), tok(ATTN_WIDTH),
            pl.BlockSpec(memory_space=pl.ANY),
            pl.BlockSpec(memory_space=pl.ANY),
            pl.BlockSpec((1, N_HEADS, past), lambda b: (b, 0, 0)),
            pl.BlockSpec((1, N_HEADS, LANES), lambda b: (b, 0, 0)),
            tok(N_HEADS),
        ],
        out_specs=tok(ATTN_WIDTH),
        out_shape=jax.ShapeDtypeStruct((batch * t_new, ATTN_WIDTH), F32),
        scratch_shapes=[pltpu.VMEM((2, N_HEADS, past, HEAD_DIM), F32),
                        pltpu.VMEM((2, N_HEADS, past, HEAD_DIM), F32),
                        pltpu.SemaphoreType.DMA((2, 2))],
        compiler_params=_params(("arbitrary",)),
        name="attn_sample",
    )(q, kb, vb, cache_k, cache_v, fc_row, fn_row, fn_col)


def _pack_bf16_pair(lo, hi):
    lo_bits = lax.bitcast_convert_type(lo, jnp.uint32) >> 16
    hi_bits = lax.bitcast_convert_type(hi, jnp.uint32) & jnp.uint32(0xFFFF0000)
    return lo_bits | hi_bits


def _unpack_bf16_pair(w):
    lo = lax.bitcast_convert_type(w << 16, F32).astype(BF16)
    hi = lax.bitcast_convert_type(w & jnp.uint32(0xFFFF0000), F32).astype(BF16)
    return lo, hi


def _merge_kernel(yc_ref, ya_ref, x_ref, ga_ref, wo_ref, gf_ref, wrh_ref, wrl_ref, br_ref,
                  h_ref, xp_ref, lg_ref):
    ya = ya_ref[...]
    ya_n = (ya * lax.rsqrt(jnp.mean(ya * ya, axis=-1, keepdims=True) + EPS) * ga_ref[...]).astype(BF16)
    y = _dot(yc_ref[...], wo_ref[0:CONV_WIDTH, :]) + _dot(ya_n, wo_ref[CONV_WIDTH:, :])
    h = x_ref[...] + y
    h_ref[...] = h
    xn = h * lax.rsqrt(jnp.mean(h * h, axis=-1, keepdims=True) + EPS) * gf_ref[...]
    xn_hi = xn.astype(BF16)
    xn_hi32 = xn_hi.astype(F32)
    xn_lo = (xn - xn_hi32).astype(BF16)
    lg_ref[...] = (_dot(xn_hi, wrh_ref[...]) + _dot(xn_lo, wrh_ref[...]) + _dot(xn_hi, wrl_ref[...])
                   + br_ref[...])
    half = D_MODEL // 2
    xp_ref[...] = _pack_bf16_pair(xn_hi32[:, :half], xn_hi32[:, half:])


def _merge(yc_n, ya, x, ga, w_out, g_ffn, wr_hi, wr_lo, b_r, tm):
    n = x.shape[0]
    row = lambda w: pl.BlockSpec((tm, w), lambda i: (i, 0))
    return pl.pallas_call(
        _merge_kernel,
        grid=(n // tm,),
        in_specs=[row(CONV_WIDTH), row(ATTN_WIDTH), row(D_MODEL), _resident((1, ATTN_WIDTH)),
                  _resident((D_MODEL, D_MODEL)), _resident((1, D_MODEL)),
                  _resident((D_MODEL, LANES)), _resident((D_MODEL, LANES)), _resident((1, LANES))],
        out_specs=(row(D_MODEL), row(D_MODEL // 2), row(LANES)),
        out_shape=(jax.ShapeDtypeStruct((n, D_MODEL), F32),
                   jax.ShapeDtypeStruct((n, D_MODEL // 2), jnp.uint32),
                   jax.ShapeDtypeStruct((n, LANES), F32)),
        compiler_params=_params(("parallel",)),
        name="merge_out",
    )(yc_n, ya, x, ga, w_out, g_ffn, wr_hi, wr_lo, b_r)


def _route_kernel(lg_ref, info_ref, infot_ref, cnt_ref, carry_ref, *, tm):
    step = pl.program_id(0)

    @pl.when(step == 0)
    def _():
        carry_ref[...] = jnp.zeros_like(carry_ref)

    lg = lg_ref[...]
    lane = lax.broadcasted_iota(jnp.int32, lg.shape, 1)
    lanef = lane.astype(F32)
    big = jnp.float32(1e9)
    rmax = lambda v: jnp.max(v, axis=-1, keepdims=True)
    rmin = lambda v: jnp.min(v, axis=-1, keepdims=True)
    rsum = lambda v: jnp.sum(v, axis=-1, keepdims=True)

    is_g = (lane >= N_EXPERTS) & (lane < N_EXPERTS + N_GROUPS)
    gl = jnp.where(is_g, lg, NEG_BIG)
    gmax = rmax(gl)
    gsum = rsum(jnp.where(is_g, jnp.exp(gl - gmax), 0.0))
    pg_star = 1.0 / gsum
    g_idx = rmin(jnp.where(is_g & (gl == gmax), lanef - N_EXPERTS, big))

    e_lo = g_idx * EXPERTS_PER_GROUP
    is_e = (lanef >= e_lo) & (lanef < e_lo + EXPERTS_PER_GROUP)
    el = jnp.where(is_e, lg, NEG_BIG)
    m1 = rmax(el)
    i1 = rmin(jnp.where(is_e & (el == m1), lanef, big))
    sel1 = lanef == i1
    el2 = jnp.where(sel1, NEG_BIG, el)
    m2 = rmax(el2)
    i2 = rmin(jnp.where(is_e & (el2 == m2) & jnp.logical_not(sel1), lanef, big))
    sel2 = lanef == i2
    z = rsum(jnp.where(is_e, jnp.exp(el - m1), 0.0))
    p1 = 1.0 / z
    p2 = jnp.exp(m2 - m1) / z
    gate1 = pg_star * p1 / (p1 + p2)
    gate2 = pg_star * p2 / (p1 + p2)

    onehot = jnp.where(sel1 | sel2, 1.0, 0.0)
    r = lax.broadcasted_iota(jnp.int32, (tm, tm), 0)
    c = lax.broadcasted_iota(jnp.int32, (tm, tm), 1)
    tri = jnp.where(c < r, 1.0, 0.0).astype(BF16)
    before = _dot(tri, onehot.astype(BF16)) + carry_ref[...]
    rank1 = rsum(jnp.where(sel1, before, 0.0))
    rank2 = rsum(jnp.where(sel2, before, 0.0))
    carry_ref[...] = carry_ref[...] + jnp.sum(onehot, axis=0, keepdims=True)
    cnt_ref[...] = carry_ref[...]

    info = jnp.zeros_like(lg)
    for k, val in enumerate((i1, i2, rank1, rank2, gate1, gate2)):
        info = jnp.where(lane == k, val, info)
    info_ref[...] = info
    infot_ref[...] = info.T[0:8, :]


def _route(logits, tm):
    n = logits.shape[0]
    kernel = functools.partial(_route_kernel, tm=tm)
    return pl.pallas_call(
        kernel,
        grid=(n // tm,),
        in_specs=[pl.BlockSpec((tm, LANES), lambda i: (i, 0))],
        out_specs=(pl.BlockSpec((tm, LANES), lambda i: (i, 0)),
                   pl.BlockSpec((8, tm), lambda i: (0, i)),
                   pl.BlockSpec((1, LANES), lambda i: (0, 0))),
        out_shape=(jax.ShapeDtypeStruct((n, LANES), F32), jax.ShapeDtypeStruct((8, n), F32),
                   jax.ShapeDtypeStruct((1, LANES), F32)),
        scratch_shapes=[pltpu.VMEM((1, LANES), F32)],
        compiler_params=_params(("arbitrary",)),
        name="route",
    )(logits)


DISPATCH_CHUNK = 128


def _dispatch_kernel(dest_ref, xa_ref, xb_ref, xs_in_hbm, xs_hbm, sem):
    del xs_in_hbm
    n_all = dest_ref.shape[0] // 2

    def scatter(src_ref, t_off):
        def row_copy(t, k):
            return pltpu.make_async_copy(src_ref.at[pl.ds(t, 1)],
                                         xs_hbm.at[pl.ds(dest_ref[k * n_all + t_off + t], 1)], sem)

        def issue(c):
            def body(r, carry):
                row_copy(c * DISPATCH_CHUNK + r, 0).start()
                row_copy(c * DISPATCH_CHUNK + r, 1).start()
                return carry
            lax.fori_loop(0, DISPATCH_CHUNK, body, 0, unroll=8)

        def drain(c):
            def body(r, carry):
                row_copy(c * DISPATCH_CHUNK + r, 0).wait()
                row_copy(c * DISPATCH_CHUNK + r, 1).wait()
                return carry
            lax.fori_loop(0, DISPATCH_CHUNK, body, 0, unroll=8)

        n_chunks = src_ref.shape[0] // DISPATCH_CHUNK

        def chunk(c, carry):
            issue(c)

            @pl.when(c > 0)
            def _():
                drain(c - 1)
            return carry

        lax.fori_loop(0, n_chunks, chunk, 0)
        drain(n_chunks - 1)

    scatter(xa_ref, 0)
    scatter(xb_ref, xa_ref.shape[0])


def _dispatch(dest_flat, xa, xb, n_slots):
    w = xa.shape[1]
    xs0 = jnp.zeros((n_slots, w), xa.dtype)
    vmem = pl.BlockSpec(memory_space=pltpu.VMEM)
    grid_spec = pltpu.PrefetchScalarGridSpec(
        num_scalar_prefetch=1,
        grid=(1,),
        in_specs=[vmem, vmem, pl.BlockSpec(memory_space=pl.ANY)],
        out_specs=pl.BlockSpec(memory_space=pl.ANY),
        scratch_shapes=[pltpu.SemaphoreType.DMA(())],
    )
    return pl.pallas_call(
        _dispatch_kernel,
        grid_spec=grid_spec,
        out_shape=jax.ShapeDtypeStruct((n_slots, w), xa.dtype),
        input_output_aliases={3: 0},
        compiler_params=pltpu.CompilerParams(dimension_semantics=("arbitrary",),
                                             vmem_limit_bytes=VMEM_LIMIT, has_side_effects=True),
        name="dispatch",
    )(dest_flat, xa, xb, xs0)


CAST_CHUNK_ELEMS = 64 * 1024


def _expert_kernel(be_ref, nu_ref, nx_ref, xs_ref, w1_hbm, w3_hbm, w2_hbm, y_ref,
                   w1f_ref, w3f_ref, w2f_ref, w1b_ref, w3b_ref, w2b_ref, sem):
    b = pl.program_id(0)
    active = b < nu_ref[0]
    new_expert = jnp.logical_or(b == 0, be_ref[b] != be_ref[jnp.maximum(b - 1, 0)])

    def fetch(e):
        return (pltpu.make_async_copy(w1_hbm.at[e], w1f_ref, sem.at[0]),
                pltpu.make_async_copy(w3_hbm.at[e], w3f_ref, sem.at[1]),
                pltpu.make_async_copy(w2_hbm.at[e], w2f_ref, sem.at[2]))

    @pl.when(b == 0)
    def _():
        for copy in fetch(be_ref[0]):
            copy.start()

    @pl.when(jnp.logical_and(active, new_expert))
    def _():
        for copy in fetch(be_ref[b]):
            copy.wait()
        for src, dst in ((w1f_ref, w1b_ref), (w3f_ref, w3b_ref), (w2f_ref, w2b_ref)):
            rows, cols = src.shape
            chunk = CAST_CHUNK_ELEMS // cols

            def cast_rows(c, carry, src=src, dst=dst, chunk=chunk):
                r0 = pl.multiple_of(c * chunk, chunk)
                dst[pl.ds(r0, chunk), :] = src[pl.ds(r0, chunk), :].astype(BF16)
                return carry

            lax.fori_loop(0, rows // chunk, cast_rows, 0)

        @pl.when(nx_ref[b] >= 0)
        def _():
            for copy in fetch(nx_ref[b]):
                copy.start()

    @pl.when(active)
    def _():
        lo, hi = _unpack_bf16_pair(xs_ref[...])
        half = D_MODEL // 2
        h1 = _dot(lo, w1b_ref[:half, :]) + _dot(hi, w1b_ref[half:, :])
        h3 = _dot(lo, w3b_ref[:half, :]) + _dot(hi, w3b_ref[half:, :])
        h = (h1 * jax.nn.sigmoid(h1) * h3).astype(BF16)
        y_ref[...] = _dot(h, w2b_ref[...])

    @pl.when(jnp.logical_not(active))
    def _():
        y_ref[...] = jnp.zeros_like(y_ref)


def _experts(block_e, n_used, next_e, xs, w1, w3, w2):
    n_slots = xs.shape[0]
    nb = n_slots // MOE_BLOCK
    blk = lambda b, be, nu, nx: (jnp.minimum(b, nu[0] - 1), 0)
    hbm = pl.BlockSpec(memory_space=pl.ANY)
    grid_spec = pltpu.PrefetchScalarGridSpec(
        num_scalar_prefetch=3,
        grid=(nb,),
        in_specs=[pl.BlockSpec((MOE_BLOCK, D_MODEL // 2), blk), hbm, hbm, hbm],
        out_specs=pl.BlockSpec((MOE_BLOCK, D_MODEL), lambda b, be, nu, nx: (b, 0)),
        scratch_shapes=[pltpu.VMEM((D_MODEL, D_EXPERT), F32), pltpu.VMEM((D_MODEL, D_EXPERT), F32),
                        pltpu.VMEM((D_EXPERT, D_MODEL), F32),
                        pltpu.VMEM((D_MODEL, D_EXPERT), BF16), pltpu.VMEM((D_MODEL, D_EXPERT), BF16),
                        pltpu.VMEM((D_EXPERT, D_MODEL), BF16),
                        pltpu.SemaphoreType.DMA((3,))],
    )
    return pl.pallas_call(
        _expert_kernel,
        grid_spec=grid_spec,
        out_shape=jax.ShapeDtypeStruct((n_slots, D_MODEL), F32),
        compiler_params=_params(("arbitrary",)),
        name="experts",
    )(block_e, n_used, next_e, xs, w1, w3, w2)


def _combine_kernel(dest_ref, h_ref, info_ref, yb_hbm, o_ref, buf_ref, sem, *, tm, t_off):
    i = pl.program_id(0)
    nsteps = pl.num_programs(0)
    n_all = dest_ref.shape[0] // 2

    def row_copy(step, slot, r, k):
        t = t_off + step * tm + r
        return pltpu.make_async_copy(yb_hbm.at[pl.ds(dest_ref[k * n_all + t], 1)],
                                     buf_ref.at[slot, k, pl.ds(r, 1)], sem.at[slot])

    def issue(step, slot):
        def body(r, carry):
            row_copy(step, slot, r, 0).start()
            row_copy(step, slot, r, 1).start()
            return carry
        lax.fori_loop(0, tm, body, 0, unroll=8)

    def drain(step, slot):
        del step

        def body(r, carry):
            for k in range(2):
                pltpu.make_async_copy(yb_hbm.at[pl.ds(0, 1)], buf_ref.at[slot, k, pl.ds(r, 1)],
                                      sem.at[slot]).wait()
            return carry
        lax.fori_loop(0, tm, body, 0, unroll=8)

    slot = i % 2

    @pl.when(i == 0)
    def _():
        issue(0, 0)

    @pl.when(i + 1 < nsteps)
    def _():
        issue(i + 1, 1 - slot)

    drain(i, slot)
    g0 = info_ref[:, 4:5]
    g1 = info_ref[:, 5:6]
    o_ref[...] = h_ref[...] + (g0 * buf_ref[slot, 0] + g1 * buf_ref[slot, 1])


def _combine(dest_flat, h, info, yb, t_off, tm):
    n = h.shape[0]
    kernel = functools.partial(_combine_kernel, tm=tm, t_off=t_off)
    ob = t_off // tm
    grid_spec = pltpu.PrefetchScalarGridSpec(
        num_scalar_prefetch=1,
        grid=(n // tm,),
        in_specs=[pl.BlockSpec((tm, D_MODEL), lambda i, d: (i, 0)),
                  pl.BlockSpec((tm, LANES), lambda i, d: (i + ob, 0)),
                  pl.BlockSpec(memory_space=pl.ANY)],
        out_specs=pl.BlockSpec((tm, D_MODEL), lambda i, d: (i, 0)),
        scratch_shapes=[pltpu.VMEM((2, 2, tm, D_MODEL), F32), pltpu.SemaphoreType.DMA((2,))],
    )
    return pl.pallas_call(
        kernel,
        grid_spec=grid_spec,
        out_shape=jax.ShapeDtypeStruct((n, D_MODEL), F32),
        compiler_params=_params(("arbitrary",)),
        name="combine",
    )(dest_flat, h, info, yb)


def _ple_kernel(h_ref, p_ref, g_ref, wpg_ref, wpp_ref, o_ref):
    h = h_ref[...]
    hn = (h * lax.rsqrt(jnp.mean(h * h, axis=-1, keepdims=True) + EPS) * g_ref[...]).astype(BF16)
    pb = p_ref[...].astype(BF16)
    ch = 512
    for c in range(0, D_MODEL, ch):
        gate = jax.nn.sigmoid(_dot(hn, wpg_ref[:, c:c + ch]))
        o_ref[:, c:c + ch] = h_ref[:, c:c + ch] + gate * _dot(pb, wpp_ref[:, c:c + ch])


def _ple(h, p, g_ple, w_pg, w_pp, tm):
    n = h.shape[0]
    row = lambda w: pl.BlockSpec((tm, w), lambda i: (i, 0))
    return pl.pallas_call(
        _ple_kernel,
        grid=(n // tm,),
        in_specs=[row(D_MODEL), row(PLE_DIM), _resident((1, D_MODEL)),
                  _resident((D_MODEL, D_MODEL)), _resident((PLE_DIM, D_MODEL))],
        out_specs=row(D_MODEL),
        out_shape=jax.ShapeDtypeStruct((n, D_MODEL), F32),
        compiler_params=_params(("parallel",)),
        name="ple",
    )(h, p, g_ple, w_pg, w_pp)


def _mixer_tokens(x2d, wts, tm):
    return _in_proj(x2d, wts["g_mix"], wts["w_main"], wts["w_f"], wts["b_f"],
                    wts["q_gain"], wts["k_gain"], tm)


def kernel(x_prompt, x_sample, cache_k, cache_v, cache_logf, cache_conv, p_prompt, p_sample,
           g_mix, w_in, b_f, q_gain, k_gain, w_dw, b_dw, ln_g, ln_b, gc, ga, w_out,
           g_ffn, w_router_g, b_router_g, w_router_e, b_router_e, w1, w3, w2,
           g_ple, w_pg, w_pp):
    batch, seq, _ = x_prompt.shape
    dec_batch, t_new, _ = x_sample.shape
    past = cache_k.shape[2]
    n_p = batch * seq
    n_s = dec_batch * t_new
    n_all = n_p + n_s
    tm = 256
    li = 0

    w_in_l = w_in[li]
    pad_lanes = lambda a: jnp.pad(a, ((0, 0), (0, LANES - a.shape[1])))
    row2d = lambda a: a.reshape(1, -1)
    wts = {
        "g_mix": row2d(g_mix[li]),
        "w_main": w_in_l[:, :MAIN_COLS].astype(BF16),
        "w_f": pad_lanes(w_in_l[:, MAIN_COLS:]).astype(BF16),
        "b_f": pad_lanes(row2d(b_f[li])),
        "q_gain": row2d(q_gain[li]),
        "k_gain": row2d(k_gain[li]),
    }
    w_dw_p = jnp.pad(w_dw[li], ((0, CONV_HALO - CONV_KERNEL), (0, 0)))
    w_out_b = w_out[li].astype(BF16)
    w_r = pad_lanes(jnp.concatenate([w_router_e[li], w_router_g[li]], axis=1))
    wr_hi = w_r.astype(BF16)
    wr_lo = (w_r - wr_hi.astype(F32)).astype(BF16)
    b_r = pad_lanes(row2d(jnp.concatenate([b_router_e[li], b_router_g[li]])))
    w_pg_b, w_pp_b = w_pg[li].astype(BF16), w_pp[li].astype(BF16)

    xp = x_prompt.reshape(n_p, D_MODEL)
    u_p, q_p, k_p, kb_p, v_p, _, vbt_p, lf_p = _mixer_tokens(xp, wts, tm)
    lf_p_row = lf_p.reshape(batch, seq, N_HEADS).transpose(0, 2, 1)
    f_p_row = _cumsum_lanes(lf_p_row.reshape(batch * N_HEADS, seq)).reshape(batch, N_HEADS, seq)
    f_p_col = f_p_row.transpose(0, 2, 1).reshape(n_p, N_HEADS)
    ctx_p = jnp.zeros((batch, CONV_HALO, CONV_WIDTH), F32)
    conv_args = (w_dw_p, row2d(b_dw[li]), row2d(ln_g[li]), row2d(ln_b[li]), row2d(gc[li]))
    yc_p = _conv_module(u_p, ctx_p, *conv_args, batch, seq, tm)
    qa_p, ka_p = _attn_aug(f_p_col, tm)
    ya_p = _attn_prompt(q_p, qa_p, kb_p, ka_p, vbt_p, batch, seq, 256, 256)
    merge_args = (row2d(ga[li]), w_out_b, row2d(g_ffn[li]), wr_hi, wr_lo, b_r)
    h_p, xpk_p, lg_p = _merge(yc_p, ya_p, xp, *merge_args, tm)

    xs_ = x_sample.reshape(n_s, D_MODEL)
    u_s, q_s, k_s, kb_s, v_s, vb_s, _, lf_s = _mixer_tokens(xs_, wts, tm)
    clf_row = cache_logf[li].transpose(0, 2, 1).reshape(dec_batch * N_HEADS, past)
    fc_row = _cumsum_lanes(clf_row).reshape(dec_batch, N_HEADS, past)
    lf_s_row = lf_s.reshape(dec_batch, t_new, N_HEADS).transpose(0, 2, 1).reshape(dec_batch * N_HEADS, t_new)
    fn_row = _cumsum_lanes(jnp.pad(lf_s_row, ((0, 0), (0, LANES - t_new)))).reshape(dec_batch, N_HEADS, LANES)
    fn_col = fn_row[:, :, :t_new].transpose(0, 2, 1).reshape(n_s, N_HEADS)
    ctx_s = jnp.pad(cache_conv[li], ((0, 0), (CONV_HALO - CONV_STATE, 0), (0, 0)))
    yc_s = _conv_module(u_s, ctx_s, *conv_args, dec_batch, t_new, t_new)
    ya_s = _attn_sample(q_s, kb_s, vb_s,
                        cache_k[li], cache_v[li],
                        fc_row, fn_row, fn_col, dec_batch, t_new, past)
    h_s, xpk_s, lg_s = _merge(yc_s, ya_s, xs_, *merge_args, tm)

    info, info_t, counts = _route(jnp.concatenate([lg_p, lg_s], axis=0), tm)
    counts = counts[0, :N_EXPERTS].astype(jnp.int32)
    bcounts = (counts + MOE_BLOCK - 1) // MOE_BLOCK
    bends = jnp.cumsum(bcounts)
    pstarts = (bends - bcounts) * MOE_BLOCK
    n_rows = n_all * 2
    nb = -(-(n_rows + N_EXPERTS * (MOE_BLOCK - 1)) // MOE_BLOCK)
    e_idx = info_t[0:2].astype(jnp.int32)
    expert_ids = jnp.arange(N_EXPERTS, dtype=jnp.int32)[:, None, None]
    seg_start = jnp.sum(jnp.where(e_idx[None] == expert_ids, pstarts[:, None, None], 0), axis=0)
    dest = (seg_start + info_t[2:4].astype(jnp.int32)).reshape(n_rows)
    block_e = jnp.minimum(jnp.sum(bends[None, :] <= jnp.arange(nb, dtype=jnp.int32)[:, None], axis=1),
                          N_EXPERTS - 1).astype(jnp.int32)
    n_used = bends[N_EXPERTS - 1:].astype(jnp.int32)
    xs_sorted = _dispatch(dest, xpk_p, xpk_s, nb * MOE_BLOCK)
    seg_end = jnp.sum(jnp.where(block_e[None, :] == expert_ids[:, :, 0], bends[:, None], 0), axis=0)
    next_e = jnp.where(seg_end < n_used[0], block_e[jnp.minimum(seg_end, nb - 1)], -1).astype(jnp.int32)
    yb = _experts(block_e, n_used, next_e, xs_sorted, w1[li], w3[li], w2[li])
    h2_p = _combine(dest, h_p, info, yb, 0, tm)
    h2_s = _combine(dest, h_s, info, yb, n_p, tm)

    y_p = _ple(h2_p, p_prompt[li].reshape(n_p, PLE_DIM), row2d(g_ple[li]), w_pg_b, w_pp_b, tm)
    y_s = _ple(h2_s, p_sample[li].reshape(n_s, PLE_DIM), row2d(g_ple[li]), w_pg_b, w_pp_b, tm)

    heads = lambda a, b, t: a.reshape(1, b, t, N_HEADS, HEAD_DIM)
    return (
        y_p.reshape(batch, seq, D_MODEL),
        y_s.reshape(dec_batch, t_new, D_MODEL),
        heads(k_p, batch, seq), heads(v_p, batch, seq),
        lf_p.reshape(1, batch, seq, N_HEADS),
        u_p.reshape(batch, seq, CONV_WIDTH)[None, :, seq - CONV_STATE:, :],
        heads(k_s, dec_batch, t_new), heads(v_s, dec_batch, t_new),
        lf_s.reshape(1, dec_batch, t_new, N_HEADS),
        u_s.reshape(dec_batch, t_new, CONV_WIDTH)[None, :, t_new - CONV_STATE:, :],
    )
```

```python
import functools

import jax
import jax.numpy as jnp
from jax import lax
from jax.experimental import pallas as pl
from jax.experimental.pallas import tpu as pltpu

D_MODEL = 2048
CONV_WIDTH = 1024
ATTN_WIDTH = 1024
HEAD_DIM = 128
N_HEADS = 8
CONV_KERNEL = 31
CONV_STATE = CONV_KERNEL - 1
N_GROUPS = 4
EXPERTS_PER_GROUP = 8
N_EXPERTS = 32
D_EXPERT = 512
PLE_DIM = 256
MOE_BLOCK = 128
EPS = 1e-6
MAIN_COLS = 2 * CONV_WIDTH + 3 * ATTN_WIDTH
Q_OFF = 2 * CONV_WIDTH
K_OFF = Q_OFF + ATTN_WIDTH
V_OFF = K_OFF + ATTN_WIDTH

LANES = 128
CONV_HALO = 32
VMEM_LIMIT = 56 * 1024 * 1024

F32 = jnp.float32
BF16 = jnp.bfloat16
NEG_BIG = -1e30
LOG2E = 1.4426950408889634


def _dot(a, b):
    return jnp.dot(a, b, preferred_element_type=F32)


def _params(sem):
    return pltpu.CompilerParams(dimension_semantics=sem, vmem_limit_bytes=VMEM_LIMIT)


def _resident(shape, index=None):
    index = (0,) * len(shape) if index is None else index
    return pl.BlockSpec(shape, lambda *_: index, pipeline_mode=pl.Buffered(1))


def _inproj_kernel(x_ref, g_ref, wval_ref, wgate_ref, wq_ref, wk_ref, wv_ref, wf_ref, bf_ref, qg_ref, kg_ref,
                   u_ref, q_ref, k_ref, kb_ref, v_ref, vb_ref, vbt_ref, lf_ref):
    x = x_ref[...]
    ms = jnp.mean(x * x, axis=-1, keepdims=True)
    a = (x * lax.rsqrt(ms + EPS) * g_ref[...]).astype(BF16)

    ch = 256
    for c in range(0, CONV_WIDTH, ch):
        val = _dot(a, wval_ref[:, c:c + ch])
        gate = _dot(a, wgate_ref[:, c:c + ch])
        u_ref[:, c:c + ch] = val * jax.nn.sigmoid(gate)

    def head_norm(z, gain):
        return z * lax.rsqrt(jnp.mean(z * z, axis=-1, keepdims=True) + EPS) * gain

    scale = LOG2E * HEAD_DIM ** -0.5
    for c in range(0, ATTN_WIDTH, ch):
        zq = _dot(a, wq_ref[:, c:c + ch])
        zk = _dot(a, wk_ref[:, c:c + ch])
        zv = _dot(a, wv_ref[:, c:c + ch])
        for s in range(0, ch, HEAD_DIM):
            qn = head_norm(zq[:, s:s + HEAD_DIM], qg_ref[...])
            kn = head_norm(zk[:, s:s + HEAD_DIM], kg_ref[...])
            q_ref[:, c + s:c + s + HEAD_DIM] = (qn * scale).astype(BF16)
            k_ref[:, c + s:c + s + HEAD_DIM] = kn
            kb_ref[:, c + s:c + s + HEAD_DIM] = kn.astype(BF16)
        v_ref[:, c:c + ch] = zv
        vb_ref[:, c:c + ch] = zv.astype(BF16)
        vbt_ref[c:c + ch, :] = zv.T.astype(BF16)

    f = _dot(a, wf_ref[...]) + bf_ref[...]
    lf = jnp.minimum(f, 0.0) - jnp.log1p(jnp.exp(-jnp.abs(f)))
    lf_ref[...] = lf[:, :N_HEADS]


def _in_proj(x, g_mix, w_parts, w_f, b_f, q_gain, k_gain, tm):
    n = x.shape[0]
    row = lambda w: pl.BlockSpec((tm, w), lambda i: (i, 0))
    out_shape = (
        jax.ShapeDtypeStruct((n, CONV_WIDTH), F32),
        jax.ShapeDtypeStruct((n, ATTN_WIDTH), BF16),
        jax.ShapeDtypeStruct((n, ATTN_WIDTH), F32),
        jax.ShapeDtypeStruct((n, ATTN_WIDTH), BF16),
        jax.ShapeDtypeStruct((n, ATTN_WIDTH), F32),
        jax.ShapeDtypeStruct((n, ATTN_WIDTH), BF16),
        jax.ShapeDtypeStruct((ATTN_WIDTH, n), BF16),
        jax.ShapeDtypeStruct((n, N_HEADS), F32),
    )
    return pl.pallas_call(
        _inproj_kernel,
        grid=(n // tm,),
        in_specs=[row(D_MODEL), _resident((1, D_MODEL))] + [_resident((D_MODEL, CONV_WIDTH))] * 5 + [
                  _resident((D_MODEL, LANES)), _resident((1, LANES)),
                  _resident((1, HEAD_DIM)), _resident((1, HEAD_DIM))],
        out_specs=(row(CONV_WIDTH), row(ATTN_WIDTH), row(ATTN_WIDTH), row(ATTN_WIDTH),
                   row(ATTN_WIDTH), row(ATTN_WIDTH),
                   pl.BlockSpec((ATTN_WIDTH, tm), lambda i: (0, i)), row(N_HEADS)),
        out_shape=out_shape,
        compiler_params=_params(("parallel",)),
        name="in_proj",
    )(x, g_mix, *w_parts, w_f, b_f, q_gain, k_gain)


def _cumsum_kernel(x_ref, o_ref):
    x = x_ref[...]
    width = x.shape[1]
    lane = lax.broadcasted_iota(jnp.int32, x.shape, 1)
    s = 1
    while s < width:
        x = x + jnp.where(lane >= s, pltpu.roll(x, s, axis=1), 0.0)
        s *= 2
    o_ref[...] = x


def _cumsum_lanes(x):
    return pl.pallas_call(
        _cumsum_kernel,
        out_shape=jax.ShapeDtypeStruct(x.shape, F32),
        name="cumsum",
    )(x)


def _conv_kernel(u_ref, halo_ref, ctx_ref, w_ref, bdw_ref, lng_ref, lnb_ref, gc_ref,
                 o_ref, ext_ref, y_ref, *, tm):
    i = pl.program_id(1)

    @pl.when(i == 0)
    def _():
        ext_ref[0:CONV_HALO, :] = ctx_ref[0]

    @pl.when(i > 0)
    def _():
        ext_ref[0:CONV_HALO, :] = halo_ref[...]

    ext_ref[CONV_HALO:CONV_HALO + tm, :] = u_ref[...]

    rows = min(64, tm)
    ch = LANES
    sub = 8
    wlen = rows + CONV_HALO
    first = CONV_HALO - CONV_STATE

    def conv_rows(r, carry):
        r0 = pl.multiple_of(r * rows, rows)
        for c in range(0, CONV_WIDTH, ch):
            acc = jnp.zeros((rows, ch), F32)
            win = ext_ref[pl.ds(r0, wlen), c:c + ch]
            for rho in range(sub):
                sh = win if rho == 0 else pltpu.roll(win, wlen - rho, axis=0)
                for a in range(wlen // sub):
                    j = sub * a + rho - first
                    if 0 <= j < CONV_KERNEL:
                        acc = acc + sh[sub * a:sub * a + rows] * w_ref[j:j + 1, c:c + ch]
            y_ref[pl.ds(r0, rows), c:c + ch] = acc
        return carry

    lax.fori_loop(0, tm // rows, conv_rows, 0)

    def norm_rows(r, carry):
        r0 = pl.multiple_of(r * rows, rows)
        y = y_ref[pl.ds(r0, rows), :] + bdw_ref[...]
        mu = jnp.mean(y, axis=-1, keepdims=True)
        yc = y - mu
        var = jnp.mean(yc * yc, axis=-1, keepdims=True)
        z = yc * lax.rsqrt(var + EPS) * lng_ref[...] + lnb_ref[...]
        s = z * jax.nn.sigmoid(z)
        ms = jnp.mean(s * s, axis=-1, keepdims=True)
        o_ref[pl.ds(r0, rows), :] = (s * lax.rsqrt(ms + EPS) * gc_ref[...]).astype(BF16)
        return carry

    lax.fori_loop(0, tm // rows, norm_rows, 0)


def _conv_module(u, ctx, w_dw, b_dw, ln_g, ln_b, gc, batch, seq, tm):
    nt = seq // tm
    hb = tm // CONV_HALO
    kernel = functools.partial(_conv_kernel, tm=tm)
    return pl.pallas_call(
        kernel,
        grid=(batch, nt),
        in_specs=[
            pl.BlockSpec((tm, CONV_WIDTH), lambda b, i: (b * nt + i, 0)),
            pl.BlockSpec((CONV_HALO, CONV_WIDTH),
                         lambda b, i: (jnp.maximum((b * nt + i) * hb - 1, 0), 0)),
            pl.BlockSpec((1, CONV_HALO, CONV_WIDTH), lambda b, i: (b, 0, 0)),
            _resident((CONV_HALO, CONV_WIDTH)),
            _resident((1, CONV_WIDTH)), _resident((1, CONV_WIDTH)),
            _resident((1, CONV_WIDTH)), _resident((1, CONV_WIDTH)),
        ],
        out_specs=pl.BlockSpec((tm, CONV_WIDTH), lambda b, i: (b * nt + i, 0)),
        out_shape=jax.ShapeDtypeStruct((batch * seq, CONV_WIDTH), BF16),
        scratch_shapes=[pltpu.VMEM((CONV_HALO + tm, CONV_WIDTH), F32),
                        pltpu.VMEM((tm, CONV_WIDTH), F32)],
        compiler_params=_params(("parallel", "arbitrary")),
        name="conv_module",
    )(u, u, ctx, w_dw, b_dw, ln_g, ln_b, gc)


def _qk(q, k):
    return lax.dot_general(q, k, (((1,), (1,)), ((), ())), preferred_element_type=F32)


AUG_TERMS = 3
QK_AHEAD = 4


def _aug_kernel(f_ref, qa_ref, ka_ref):
    f = f_ref[...] * LOG2E
    tm = f.shape[0]
    lane = lax.broadcasted_iota(jnp.int32, (tm, HEAD_DIM), 1)
    for h in range(N_HEADS):
        hs = slice(h * HEAD_DIM, (h + 1) * HEAD_DIM)
        rest = f[:, h:h + 1]
        qa = jnp.where((lane >= AUG_TERMS) & (lane < 2 * AUG_TERMS), 1.0, 0.0)
        ka = jnp.where(lane < AUG_TERMS, 1.0, 0.0)
        for t in range(AUG_TERMS):
            piece = rest.astype(BF16).astype(F32)
            rest = rest - piece
            qa = jnp.where(lane == t, piece, qa)
            ka = jnp.where(lane == AUG_TERMS + t, -piece, ka)
        qa_ref[:, hs] = qa.astype(BF16)
        ka_ref[:, hs] = ka.astype(BF16)


def _attn_aug(f_col, tm):
    n = f_col.shape[0]
    return pl.pallas_call(
        _aug_kernel,
        grid=(n // tm,),
        in_specs=[pl.BlockSpec((tm, N_HEADS), lambda i: (i, 0))],
        out_specs=(pl.BlockSpec((tm, ATTN_WIDTH), lambda i: (i, 0)),
                   pl.BlockSpec((tm, ATTN_WIDTH), lambda i: (i, 0))),
        out_shape=(jax.ShapeDtypeStruct((n, ATTN_WIDTH), BF16),
                   jax.ShapeDtypeStruct((n, ATTN_WIDTH), BF16)),
        compiler_params=_params(("parallel",)),
        name="attn_aug",
    )(f_col)


def _attn_prompt_kernel(q_ref, qa_ref, k_ref, ka_ref, vt_ref, o_ref, m_ref, l_ref, acc_ref, qt_ref,
                        sp_ref, *, tq, tk):
    i = pl.program_id(1)
    key = lax.broadcasted_iota(jnp.int32, (tk, tq), 0)
    qry = lax.broadcasted_iota(jnp.int32, (tk, tq), 1)

    m_ref[...] = jnp.full(m_ref.shape, NEG_BIG, F32)
    l_ref[...] = jnp.zeros(l_ref.shape, F32)
    acc_ref[...] = jnp.zeros(acc_ref.shape, F32)
    for h in range(N_HEADS):
        hs = slice(h * HEAD_DIM, (h + 1) * HEAD_DIM)
        qt_ref[h] = jnp.concatenate([q_ref[:, hs], qa_ref[:, hs]], axis=1).T

    def scores(ks, h):
        hs = slice(h * HEAD_DIM, (h + 1) * HEAD_DIM)
        kf = jnp.concatenate([k_ref[pl.ds(ks, tk), hs], ka_ref[pl.ds(ks, tk), hs]], axis=1)
        return _dot(kf, qt_ref[h])

    def tile_step(ks, ks_next, masked):
        pending = [sp_ref[a] for a in range(QK_AHEAD)]
        for h in range(N_HEADS):
            hs = slice(h * HEAD_DIM, (h + 1) * HEAD_DIM)
            s = pending.pop(0)
            if h + QK_AHEAD < N_HEADS:
                pending.append(scores(ks, h + QK_AHEAD))
            elif ks_next is not None:
                pending.append(scores(ks_next, h + QK_AHEAD - N_HEADS))
            if masked:
                s = jnp.where(key <= qry, s, -jnp.inf)
            m = m_ref[h]
            m_new = jnp.maximum(m, jnp.max(s, axis=0, keepdims=True))
            alpha = jnp.exp2(m - m_new)
            p = jnp.exp2(s - m_new)
            m_ref[h] = m_new
            l_ref[h] = alpha * l_ref[h] + jnp.sum(p, axis=0, keepdims=True)
            pv = _dot(vt_ref[hs, pl.ds(ks, tk)], p.astype(BF16))
            acc_ref[h] = alpha * acc_ref[h] + pv
        for a, s in enumerate(pending):
            sp_ref[a] = s

    def body(j, carry):
        tile_step(pl.multiple_of(j * tk, tk), pl.multiple_of((j + 1) * tk, tk), False)
        return carry

    for a in range(QK_AHEAD):
        sp_ref[a] = scores(0, a)
    lax.fori_loop(0, i, body, 0)
    tile_step(pl.multiple_of(i * tk, tk), None, True)
    for h in range(N_HEADS):
        o_ref[:, h * HEAD_DIM:(h + 1) * HEAD_DIM] = (acc_ref[h] / l_ref[h]).T


def _attn_prompt(q, qa, kb, ka, vbt, batch, seq, tq, tk):
    nq = seq // tq
    kernel = functools.partial(_attn_prompt_kernel, tq=tq, tk=tk)
    qblk = pl.BlockSpec((tq, ATTN_WIDTH), lambda b, i: (b * nq + i, 0))
    kblk = pl.BlockSpec((seq, ATTN_WIDTH), lambda b, i: (b, 0), pipeline_mode=pl.Buffered(1))
    vblk = pl.BlockSpec((ATTN_WIDTH, seq), lambda b, i: (0, b), pipeline_mode=pl.Buffered(1))
    return pl.pallas_call(
        kernel,
        grid=(batch, nq),
        in_specs=[qblk, qblk, kblk, kblk, vblk],
        out_specs=pl.BlockSpec((tq, ATTN_WIDTH), lambda b, i: (b * nq + i, 0)),
        out_shape=jax.ShapeDtypeStruct((batch * seq, ATTN_WIDTH), F32),
        scratch_shapes=[pltpu.VMEM((N_HEADS, 1, tq), F32), pltpu.VMEM((N_HEADS, 1, tq), F32),
                        pltpu.VMEM((N_HEADS, HEAD_DIM, tq), F32),
                        pltpu.VMEM((N_HEADS, 2 * HEAD_DIM, tq), BF16),
                        pltpu.VMEM((QK_AHEAD, tk, tq), F32)],
        compiler_params=_params(("parallel", "arbitrary")),
        name="attn_prompt",
    )(q, qa, kb, ka, vbt)


def _attn_sample_kernel(q_ref, kn_ref, vn_ref, ck_hbm, cv_hbm, fc_ref, fnrow_ref, fncol_ref,
                        o_ref, kbuf_ref, vbuf_ref, sem, *, t_new, past):
    b = pl.program_id(0)
    slot = b % 2

    def fetch(batch_idx, into):
        copies = []
        for h in range(N_HEADS):
            copies.append(pltpu.make_async_copy(ck_hbm.at[batch_idx, :, h, :], kbuf_ref.at[into, h],
                                                sem.at[into, 0]))
            copies.append(pltpu.make_async_copy(cv_hbm.at[batch_idx, :, h, :], vbuf_ref.at[into, h],
                                                sem.at[into, 1]))
        return copies

    @pl.when(b == 0)
    def _():
        for copy in fetch(0, 0):
            copy.start()

    @pl.when(b + 1 < pl.num_programs(0))
    def _():
        for copy in fetch(b + 1, 1 - slot):
            copy.start()

    for copy in fetch(b, slot):
        copy.wait()

    row = lax.broadcasted_iota(jnp.int32, (t_new, t_new), 0)
    col = lax.broadcasted_iota(jnp.int32, (t_new, t_new), 1)
    causal = col <= row
    for h in range(N_HEADS):
        hs = slice(h * HEAD_DIM, (h + 1) * HEAD_DIM)
        q = q_ref[:, hs]
        fn_q = fncol_ref[:, h:h + 1]
        fc = fc_ref[0, h:h + 1, :]
        fc_last = fc[:, past - 1:past]
        s_c = _qk(q, kbuf_ref[slot, h].astype(BF16)) + ((fc_last + fn_q) - fc) * LOG2E
        fn_k = fnrow_ref[0, h:h + 1, 0:t_new]
        s_n = _qk(q, kn_ref[:, hs]) + (fn_q - fn_k) * LOG2E
        s_n = jnp.where(causal, s_n, -jnp.inf)
        m = jnp.maximum(jnp.max(s_c, axis=-1, keepdims=True), jnp.max(s_n, axis=-1, keepdims=True))
        p_c = jnp.exp2(s_c - m)
        p_n = jnp.exp2(s_n - m)
        l = jnp.sum(p_c, axis=-1, keepdims=True) + jnp.sum(p_n, axis=-1, keepdims=True)
        acc = _dot(p_c.astype(BF16), vbuf_ref[slot, h].astype(BF16)) + _dot(p_n.astype(BF16), vn_ref[:, hs])
        o_ref[:, hs] = acc / l


def _attn_sample(q, kb, vb, cache_k, cache_v, fc_row, fn_row, fn_col, batch, t_new, past):
    kernel = functools.partial(_attn_sample_kernel, t_new=t_new, past=past)
    tok = lambda w: pl.BlockSpec((t_new, w), lambda b: (b, 0))
    return pl.pallas_call(
        kernel,
        grid=(batch,),
        in_specs=[
            tok(ATTN_WIDTH), tok(ATTN_WIDTH), tok(ATTN_WIDTH),
            pl.BlockSpec(memory_space=pl.ANY),
            pl.BlockSpec(memory_space=pl.ANY),
            pl.BlockSpec((1, N_HEADS, past), lambda b: (b, 0, 0)),
            pl.BlockSpec((1, N_HEADS, LANES), lambda b: (b, 0, 0)),
            tok(N_HEADS),
        ],
        out_specs=tok(ATTN_WIDTH),
        out_shape=jax.ShapeDtypeStruct((batch * t_new, ATTN_WIDTH), F32),
        scratch_shapes=[pltpu.VMEM((2, N_HEADS, past, HEAD_DIM), F32),
                        pltpu.VMEM((2, N_HEADS, past, HEAD_DIM), F32),
                        pltpu.SemaphoreType.DMA((2, 2))],
        compiler_params=_params(("arbitrary",)),
        name="attn_sample",
    )(q, kb, vb, cache_k, cache_v, fc_row, fn_row, fn_col)


def _pack_bf16_pair(lo, hi):
    lo_bits = lax.bitcast_convert_type(lo, jnp.uint32) >> 16
    hi_bits = lax.bitcast_convert_type(hi, jnp.uint32) & jnp.uint32(0xFFFF0000)
    return lo_bits | hi_bits


def _unpack_bf16_pair(w):
    lo = lax.bitcast_convert_type(w << 16, F32).astype(BF16)
    hi = lax.bitcast_convert_type(w & jnp.uint32(0xFFFF0000), F32).astype(BF16)
    return lo, hi


def _merge_kernel(yc_ref, ya_ref, x_ref, ga_ref, woc_ref, woa_ref, gf_ref, wrh_ref, wrl_ref, br_ref,
                  h_ref, xp_ref, lg_ref):
    ya = ya_ref[...]
    ya_n = (ya * lax.rsqrt(jnp.mean(ya * ya, axis=-1, keepdims=True) + EPS) * ga_ref[...]).astype(BF16)
    y = _dot(yc_ref[...], woc_ref[...]) + _dot(ya_n, woa_ref[...])
    h = x_ref[...] + y
    h_ref[...] = h
    xn = h * lax.rsqrt(jnp.mean(h * h, axis=-1, keepdims=True) + EPS) * gf_ref[...]
    xn_hi = xn.astype(BF16)
    xn_hi32 = xn_hi.astype(F32)
    xn_lo = (xn - xn_hi32).astype(BF16)
    lg_ref[...] = (_dot(xn_hi, wrh_ref[...]) + _dot(xn_lo, wrh_ref[...]) + _dot(xn_hi, wrl_ref[...])
                   + br_ref[...])
    half = D_MODEL // 2
    xp_ref[...] = _pack_bf16_pair(xn_hi32[:, :half], xn_hi32[:, half:])


def _merge(yc_n, ya, x, ga, w_out, g_ffn, wr_hi, wr_lo, b_r, tm):
    n = x.shape[0]
    row = lambda w: pl.BlockSpec((tm, w), lambda i: (i, 0))
    return pl.pallas_call(
        _merge_kernel,
        grid=(n // tm,),
        in_specs=[row(CONV_WIDTH), row(ATTN_WIDTH), row(D_MODEL), _resident((1, ATTN_WIDTH)),
                  _resident((CONV_WIDTH, D_MODEL), (0, 0)), _resident((ATTN_WIDTH, D_MODEL), (1, 0)),
                  _resident((1, D_MODEL)),
                  _resident((D_MODEL, LANES)), _resident((D_MODEL, LANES)), _resident((1, LANES))],
        out_specs=(row(D_MODEL), row(D_MODEL // 2), row(LANES)),
        out_shape=(jax.ShapeDtypeStruct((n, D_MODEL), F32),
                   jax.ShapeDtypeStruct((n, D_MODEL // 2), jnp.uint32),
                   jax.ShapeDtypeStruct((n, LANES), F32)),
        compiler_params=_params(("parallel",)),
        name="merge_out",
    )(yc_n, ya, x, ga, w_out, w_out, g_ffn, wr_hi, wr_lo, b_r)


def _route_kernel(lg_ref, info_ref, infot_ref, cnt_ref, carry_ref, *, tm):
    step = pl.program_id(0)

    @pl.when(step == 0)
    def _():
        carry_ref[...] = jnp.zeros_like(carry_ref)

    lg = lg_ref[...]
    lane = lax.broadcasted_iota(jnp.int32, lg.shape, 1)
    lanef = lane.astype(F32)
    big = jnp.float32(1e9)
    rmax = lambda v: jnp.max(v, axis=-1, keepdims=True)
    rmin = lambda v: jnp.min(v, axis=-1, keepdims=True)
    rsum = lambda v: jnp.sum(v, axis=-1, keepdims=True)

    is_g = (lane >= N_EXPERTS) & (lane < N_EXPERTS + N_GROUPS)
    gl = jnp.where(is_g, lg, NEG_BIG)
    gmax = rmax(gl)
    gsum = rsum(jnp.where(is_g, jnp.exp(gl - gmax), 0.0))
    pg_star = 1.0 / gsum
    g_idx = rmin(jnp.where(is_g & (gl == gmax), lanef - N_EXPERTS, big))

    e_lo = g_idx * EXPERTS_PER_GROUP
    is_e = (lanef >= e_lo) & (lanef < e_lo + EXPERTS_PER_GROUP)
    el = jnp.where(is_e, lg, NEG_BIG)
    m1 = rmax(el)
    i1 = rmin(jnp.where(is_e & (el == m1), lanef, big))
    sel1 = lanef == i1
    el2 = jnp.where(sel1, NEG_BIG, el)
    m2 = rmax(el2)
    i2 = rmin(jnp.where(is_e & (el2 == m2) & jnp.logical_not(sel1), lanef, big))
    sel2 = lanef == i2
    z = rsum(jnp.where(is_e, jnp.exp(el - m1), 0.0))
    p1 = 1.0 / z
    p2 = jnp.exp(m2 - m1) / z
    gate1 = pg_star * p1 / (p1 + p2)
    gate2 = pg_star * p2 / (p1 + p2)

    onehot = jnp.where(sel1 | sel2, 1.0, 0.0)
    r = lax.broadcasted_iota(jnp.int32, (tm, tm), 0)
    c = lax.broadcasted_iota(jnp.int32, (tm, tm), 1)
    tri = jnp.where(c < r, 1.0, 0.0).astype(BF16)
    before = _dot(tri, onehot.astype(BF16)) + carry_ref[...]
    rank1 = rsum(jnp.where(sel1, before, 0.0))
    rank2 = rsum(jnp.where(sel2, before, 0.0))
    carry_ref[...] = carry_ref[...] + jnp.sum(onehot, axis=0, keepdims=True)
    cnt_ref[...] = carry_ref[...]

    info = jnp.zeros_like(lg)
    for k, val in enumerate((i1, i2, rank1, rank2, gate1, gate2)):
        info = jnp.where(lane == k, val, info)
    info_ref[...] = info
    infot_ref[...] = info.T[0:8, :]


def _route(logits, tm):
    n = logits.shape[0]
    kernel = functools.partial(_route_kernel, tm=tm)
    return pl.pallas_call(
        kernel,
        grid=(n // tm,),
        in_specs=[pl.BlockSpec((tm, LANES), lambda i: (i, 0))],
        out_specs=(pl.BlockSpec((tm, LANES), lambda i: (i, 0)),
                   pl.BlockSpec((8, tm), lambda i: (0, i)),
                   pl.BlockSpec((1, LANES), lambda i: (0, 0))),
        out_shape=(jax.ShapeDtypeStruct((n, LANES), F32), jax.ShapeDtypeStruct((8, n), F32),
                   jax.ShapeDtypeStruct((1, LANES), F32)),
        scratch_shapes=[pltpu.VMEM((1, LANES), F32)],
        compiler_params=_params(("arbitrary",)),
        name="route",
    )(logits)


DISPATCH_CHUNK = 128


def _dispatch_kernel(dest_ref, xa_ref, xb_ref, xs_in_hbm, xs_hbm, sem):
    del xs_in_hbm
    n_all = dest_ref.shape[0] // 2

    def scatter(src_ref, t_off):
        def row_copy(t, k):
            return pltpu.make_async_copy(src_ref.at[pl.ds(t, 1)],
                                         xs_hbm.at[pl.ds(dest_ref[k * n_all + t_off + t], 1)], sem)

        def issue(c):
            def body(r, carry):
                row_copy(c * DISPATCH_CHUNK + r, 0).start()
                row_copy(c * DISPATCH_CHUNK + r, 1).start()
                return carry
            lax.fori_loop(0, DISPATCH_CHUNK, body, 0, unroll=8)

        def drain(c):
            def body(r, carry):
                row_copy(c * DISPATCH_CHUNK + r, 0).wait()
                row_copy(c * DISPATCH_CHUNK + r, 1).wait()
                return carry
            lax.fori_loop(0, DISPATCH_CHUNK, body, 0, unroll=8)

        n_chunks = src_ref.shape[0] // DISPATCH_CHUNK

        def chunk(c, carry):
            issue(c)

            @pl.when(c > 0)
            def _():
                drain(c - 1)
            return carry

        lax.fori_loop(0, n_chunks, chunk, 0)
        drain(n_chunks - 1)

    scatter(xa_ref, 0)
    scatter(xb_ref, xa_ref.shape[0])


def _dispatch(dest_flat, xa, xb, n_slots):
    w = xa.shape[1]
    xs0 = jnp.zeros((n_slots, w), xa.dtype)
    vmem = pl.BlockSpec(memory_space=pltpu.VMEM)
    grid_spec = pltpu.PrefetchScalarGridSpec(
        num_scalar_prefetch=1,
        grid=(1,),
        in_specs=[vmem, vmem, pl.BlockSpec(memory_space=pl.ANY)],
        out_specs=pl.BlockSpec(memory_space=pl.ANY),
        scratch_shapes=[pltpu.SemaphoreType.DMA(())],
    )
    return pl.pallas_call(
        _dispatch_kernel,
        grid_spec=grid_spec,
        out_shape=jax.ShapeDtypeStruct((n_slots, w), xa.dtype),
        input_output_aliases={3: 0},
        compiler_params=pltpu.CompilerParams(dimension_semantics=("arbitrary",),
                                             vmem_limit_bytes=VMEM_LIMIT, has_side_effects=True),
        name="dispatch",
    )(dest_flat, xa, xb, xs0)


CAST_CHUNK_ELEMS = 64 * 1024
WEIGHT_DMA_SPLIT = 4


def _expert_kernel(be_ref, nu_ref, nx_ref, xs_ref, w1_hbm, w3_hbm, w2_hbm, y_ref,
                   w1f_ref, w3f_ref, w2f_ref, w1b_ref, w3b_ref, w2b_ref, sem):
    b = pl.program_id(0)
    active = b < nu_ref[0]
    new_expert = jnp.logical_or(b == 0, be_ref[b] != be_ref[jnp.maximum(b - 1, 0)])

    def fetch(e):
        copies = []
        for k, (src, dst) in enumerate(((w1_hbm, w1f_ref), (w3_hbm, w3f_ref), (w2_hbm, w2f_ref))):
            slab = dst.shape[0] // WEIGHT_DMA_SPLIT
            for c in range(WEIGHT_DMA_SPLIT):
                rows = pl.ds(c * slab, slab)
                copies.append(pltpu.make_async_copy(src.at[e, rows], dst.at[rows], sem.at[k]))
        return copies

    @pl.when(b == 0)
    def _():
        for copy in fetch(be_ref[0]):
            copy.start()

    @pl.when(jnp.logical_and(active, new_expert))
    def _():
        for copy in fetch(be_ref[b]):
            copy.wait()
        for src, dst in ((w1f_ref, w1b_ref), (w3f_ref, w3b_ref), (w2f_ref, w2b_ref)):
            rows, cols = src.shape
            chunk = CAST_CHUNK_ELEMS // cols

            def cast_rows(c, carry, src=src, dst=dst, chunk=chunk):
                r0 = pl.multiple_of(c * chunk, chunk)
                dst[pl.ds(r0, chunk), :] = src[pl.ds(r0, chunk), :].astype(BF16)
                return carry

            lax.fori_loop(0, rows // chunk, cast_rows, 0)

        @pl.when(nx_ref[b] >= 0)
        def _():
            for copy in fetch(nx_ref[b]):
                copy.start()

    @pl.when(active)
    def _():
        lo, hi = _unpack_bf16_pair(xs_ref[...])
        half = D_MODEL // 2
        h1 = _dot(lo, w1b_ref[:half, :]) + _dot(hi, w1b_ref[half:, :])
        h3 = _dot(lo, w3b_ref[:half, :]) + _dot(hi, w3b_ref[half:, :])
        h = (h1 * jax.nn.sigmoid(h1) * h3).astype(BF16)
        y_ref[...] = _dot(h, w2b_ref[...])

    @pl.when(jnp.logical_not(active))
    def _():
        y_ref[...] = jnp.zeros_like(y_ref)


def _experts(block_e, n_used, next_e, xs, w1, w3, w2):
    n_slots = xs.shape[0]
    nb = n_slots // MOE_BLOCK
    blk = lambda b, be, nu, nx: (jnp.minimum(b, nu[0] - 1), 0)
    hbm = pl.BlockSpec(memory_space=pl.ANY)
    grid_spec = pltpu.PrefetchScalarGridSpec(
        num_scalar_prefetch=3,
        grid=(nb,),
        in_specs=[pl.BlockSpec((MOE_BLOCK, D_MODEL // 2), blk), hbm, hbm, hbm],
        out_specs=pl.BlockSpec((MOE_BLOCK, D_MODEL), lambda b, be, nu, nx: (b, 0)),
        scratch_shapes=[pltpu.VMEM((D_MODEL, D_EXPERT), F32), pltpu.VMEM((D_MODEL, D_EXPERT), F32),
                        pltpu.VMEM((D_EXPERT, D_MODEL), F32),
                        pltpu.VMEM((D_MODEL, D_EXPERT), BF16), pltpu.VMEM((D_MODEL, D_EXPERT), BF16),
                        pltpu.VMEM((D_EXPERT, D_MODEL), BF16),
                        pltpu.SemaphoreType.DMA((3,))],
    )
    return pl.pallas_call(
        _expert_kernel,
        grid_spec=grid_spec,
        out_shape=jax.ShapeDtypeStruct((n_slots, D_MODEL), F32),
        compiler_params=_params(("arbitrary",)),
        name="experts",
    )(block_e, n_used, next_e, xs, w1, w3, w2)


PLE_SPLIT = 4


def _combine_ple_kernel(dest_ref, h_ref, info_ref, p_ref, g_ref, *rest, tm, t_off):
    wpg_refs = rest[:PLE_SPLIT]
    wpp_ref, yb_hbm, o_ref, buf_ref, sem = rest[PLE_SPLIT:]
    i = pl.program_id(0)
    last = pl.num_programs(0) - 1
    n_all = dest_ref.shape[0] // 2
    slot = i % 2

    def row_copy(step, into, r, k):
        t = t_off + step * tm + r
        return pltpu.make_async_copy(yb_hbm.at[pl.ds(dest_ref[k * n_all + t], 1)],
                                     buf_ref.at[into, k, pl.ds(r, 1)], sem.at[into])

    def drain(into):
        def body(r, carry):
            for k in range(2):
                pltpu.make_async_copy(yb_hbm.at[pl.ds(0, 1)], buf_ref.at[into, k, pl.ds(r, 1)],
                                      sem.at[into]).wait()
            return carry
        lax.fori_loop(0, tm, body, 0, unroll=8)

    @pl.when(i == 0)
    def _():
        def body(r, carry):
            row_copy(0, 0, r, 0).start()
            row_copy(0, 0, r, 1).start()
            return carry
        lax.fori_loop(0, tm, body, 0, unroll=8)

    drain(slot)
    h2 = h_ref[...] + (info_ref[:, 4:5] * buf_ref[slot, 0] + info_ref[:, 5:6] * buf_ref[slot, 1])
    hn = (h2 * lax.rsqrt(jnp.mean(h2 * h2, axis=-1, keepdims=True) + EPS) * g_ref[...]).astype(BF16)
    pb = p_ref[...].astype(BF16)

    nxt = jnp.minimum(i + 1, last)
    ch = D_MODEL // PLE_SPLIT
    per = tm // PLE_SPLIT
    for k in range(PLE_SPLIT):
        for r in range(k * per, (k + 1) * per):
            row_copy(nxt, 1 - slot, r, 0).start()
            row_copy(nxt, 1 - slot, r, 1).start()
        cs = slice(k * ch, (k + 1) * ch)
        gate = jax.nn.sigmoid(_dot(hn, wpg_refs[k][...]))
        o_ref[:, cs] = h2[:, cs] + gate * _dot(pb, wpp_ref[:, cs])

    @pl.when(i == last)
    def _():
        drain(1 - slot)


def _combine_ple(dest_flat, h, info, yb, p, g_ple, w_pg, w_pp, t_off, tm):
    n = h.shape[0]
    kernel = functools.partial(_combine_ple_kernel, tm=tm, t_off=t_off)
    ob = t_off // tm
    row = lambda w: pl.BlockSpec((tm, w), lambda i, d: (i, 0))
    grid_spec = pltpu.PrefetchScalarGridSpec(
        num_scalar_prefetch=1,
        grid=(n // tm,),
        in_specs=[row(D_MODEL), pl.BlockSpec((tm, LANES), lambda i, d: (i + ob, 0)), row(PLE_DIM),
                  _resident((1, D_MODEL))]
        + [_resident((D_MODEL, D_MODEL // PLE_SPLIT), (0, k)) for k in range(PLE_SPLIT)]
        + [_resident((PLE_DIM, D_MODEL)), pl.BlockSpec(memory_space=pl.ANY)],
        out_specs=row(D_MODEL),
        scratch_shapes=[pltpu.VMEM((2, 2, tm, D_MODEL), F32), pltpu.SemaphoreType.DMA((2,))],
    )
    return pl.pallas_call(
        kernel,
        grid_spec=grid_spec,
        out_shape=jax.ShapeDtypeStruct((n, D_MODEL), F32),
        compiler_params=_params(("arbitrary",)),
        name="combine_ple",
    )(dest_flat, h, info, p, g_ple, *([w_pg] * PLE_SPLIT), w_pp, yb)


def _mixer_tokens(x2d, wts, tm):
    return _in_proj(x2d, wts["g_mix"], wts["w_parts"], wts["w_f"], wts["b_f"],
                    wts["q_gain"], wts["k_gain"], tm)


def kernel(x_prompt, x_sample, cache_k, cache_v, cache_logf, cache_conv, p_prompt, p_sample,
           g_mix, w_in, b_f, q_gain, k_gain, w_dw, b_dw, ln_g, ln_b, gc, ga, w_out,
           g_ffn, w_router_g, b_router_g, w_router_e, b_router_e, w1, w3, w2,
           g_ple, w_pg, w_pp):
    batch, seq, _ = x_prompt.shape
    dec_batch, t_new, _ = x_sample.shape
    past = cache_k.shape[2]
    n_p = batch * seq
    n_s = dec_batch * t_new
    n_all = n_p + n_s
    tm = 256
    li = 0

    w_in_l = w_in[li]
    pad_lanes = lambda a: jnp.pad(a, ((0, 0), (0, LANES - a.shape[1])))
    row2d = lambda a: a.reshape(1, -1)
    wts = {
        "g_mix": row2d(g_mix[li]),
        "w_parts": [w_in_l[:, c:c + CONV_WIDTH].astype(BF16) for c in range(0, MAIN_COLS, CONV_WIDTH)],
        "w_f": pad_lanes(w_in_l[:, MAIN_COLS:]).astype(BF16),
        "b_f": pad_lanes(row2d(b_f[li])),
        "q_gain": row2d(q_gain[li]),
        "k_gain": row2d(k_gain[li]),
    }
    w_dw_p = jnp.pad(w_dw[li], ((0, CONV_HALO - CONV_KERNEL), (0, 0)))
    w_out_b = w_out[li].astype(BF16)
    w_r = pad_lanes(jnp.concatenate([w_router_e[li], w_router_g[li]], axis=1))
    wr_hi = w_r.astype(BF16)
    wr_lo = (w_r - wr_hi.astype(F32)).astype(BF16)
    b_r = pad_lanes(row2d(jnp.concatenate([b_router_e[li], b_router_g[li]])))
    w_pg_b, w_pp_b = w_pg[li].astype(BF16), w_pp[li].astype(BF16)

    xp = x_prompt.reshape(n_p, D_MODEL)
    u_p, q_p, k_p, kb_p, v_p, _, vbt_p, lf_p = _mixer_tokens(xp, wts, tm)
    lf_p_row = lf_p.reshape(batch, seq, N_HEADS).transpose(0, 2, 1)
    f_p_row = _cumsum_lanes(lf_p_row.reshape(batch * N_HEADS, seq)).reshape(batch, N_HEADS, seq)
    f_p_col = f_p_row.transpose(0, 2, 1).reshape(n_p, N_HEADS)
    ctx_p = jnp.zeros((batch, CONV_HALO, CONV_WIDTH), F32)
    conv_args = (w_dw_p, row2d(b_dw[li]), row2d(ln_g[li]), row2d(ln_b[li]), row2d(gc[li]))
    yc_p = _conv_module(u_p, ctx_p, *conv_args, batch, seq, tm)
    qa_p, ka_p = _attn_aug(f_p_col, tm)
    ya_p = _attn_prompt(q_p, qa_p, kb_p, ka_p, vbt_p, batch, seq, 256, 256)
    merge_args = (row2d(ga[li]), w_out_b, row2d(g_ffn[li]), wr_hi, wr_lo, b_r)
    h_p, xpk_p, lg_p = _merge(yc_p, ya_p, xp, *merge_args, tm)

    xs_ = x_sample.reshape(n_s, D_MODEL)
    u_s, q_s, k_s, kb_s, v_s, vb_s, _, lf_s = _mixer_tokens(xs_, wts, tm)
    clf_row = cache_logf[li].transpose(0, 2, 1).reshape(dec_batch * N_HEADS, past)
    fc_row = _cumsum_lanes(clf_row).reshape(dec_batch, N_HEADS, past)
    lf_s_row = lf_s.reshape(dec_batch, t_new, N_HEADS).transpose(0, 2, 1).reshape(dec_batch * N_HEADS, t_new)
    fn_row = _cumsum_lanes(jnp.pad(lf_s_row, ((0, 0), (0, LANES - t_new)))).reshape(dec_batch, N_HEADS, LANES)
    fn_col = fn_row[:, :, :t_new].transpose(0, 2, 1).reshape(n_s, N_HEADS)
    ctx_s = jnp.pad(cache_conv[li], ((0, 0), (CONV_HALO - CONV_STATE, 0), (0, 0)))
    yc_s = _conv_module(u_s, ctx_s, *conv_args, dec_batch, t_new, t_new)
    ya_s = _attn_sample(q_s, kb_s, vb_s,
                        cache_k[li], cache_v[li],
                        fc_row, fn_row, fn_col, dec_batch, t_new, past)
    h_s, xpk_s, lg_s = _merge(yc_s, ya_s, xs_, *merge_args, tm)

    info, info_t, counts = _route(jnp.concatenate([lg_p, lg_s], axis=0), tm)
    counts = counts[0, :N_EXPERTS].astype(jnp.int32)
    bcounts = (counts + MOE_BLOCK - 1) // MOE_BLOCK
    bends = jnp.cumsum(bcounts)
    pstarts = (bends - bcounts) * MOE_BLOCK
    n_rows = n_all * 2
    nb = -(-(n_rows + N_EXPERTS * (MOE_BLOCK - 1)) // MOE_BLOCK)
    e_idx = info_t[0:2].astype(jnp.int32)
    expert_ids = jnp.arange(N_EXPERTS, dtype=jnp.int32)[:, None, None]
    seg_start = jnp.sum(jnp.where(e_idx[None] == expert_ids, pstarts[:, None, None], 0), axis=0)
    dest = (seg_start + info_t[2:4].astype(jnp.int32)).reshape(n_rows)
    block_e = jnp.minimum(jnp.sum(bends[None, :] <= jnp.arange(nb, dtype=jnp.int32)[:, None], axis=1),
                          N_EXPERTS - 1).astype(jnp.int32)
    n_used = bends[N_EXPERTS - 1:].astype(jnp.int32)
    xs_sorted = _dispatch(dest, xpk_p, xpk_s, nb * MOE_BLOCK)
    seg_end = jnp.sum(jnp.where(block_e[None, :] == expert_ids[:, :, 0], bends[:, None], 0), axis=0)
    next_e = jnp.where(seg_end < n_used[0], block_e[jnp.minimum(seg_end, nb - 1)], -1).astype(jnp.int32)
    yb = _experts(block_e, n_used, next_e, xs_sorted, w1[li], w3[li], w2[li])

    ple_args = (row2d(g_ple[li]), w_pg_b, w_pp_b)
    y_p = _combine_ple(dest, h_p, info, yb, p_prompt[li].reshape(n_p, PLE_DIM), *ple_args, 0, tm)
    y_s = _combine_ple(dest, h_s, info, yb, p_sample[li].reshape(n_s, PLE_DIM), *ple_args, n_p, tm)

    heads = lambda a, b, t: a.reshape(1, b, t, N_HEADS, HEAD_DIM)
    return (
        y_p.reshape(batch, seq, D_MODEL),
        y_s.reshape(dec_batch, t_new, D_MODEL),
        heads(k_p, batch, seq), heads(v_p, batch, seq),
        lf_p.reshape(1, batch, seq, N_HEADS),
        u_p.reshape(batch, seq, CONV_WIDTH)[None, :, seq - CONV_STATE:, :],
        heads(k_s, dec_batch, t_new), heads(v_s, dec_batch, t_new),
        lf_s.reshape(1, dec_batch, t_new, N_HEADS),
        u_s.reshape(dec_batch, t_new, CONV_WIDTH)[None, :, t_new - CONV_STATE:, :],
    )
```

```python
import functools

import jax
import jax.numpy as jnp
from jax import lax
from jax.experimental import pallas as pl
from jax.experimental.pallas import tpu as pltpu

D_MODEL = 2048
CONV_WIDTH = 1024
ATTN_WIDTH = 1024
HEAD_DIM = 128
N_HEADS = 8
CONV_KERNEL = 31
CONV_STATE = CONV_KERNEL - 1
N_GROUPS = 4
EXPERTS_PER_GROUP = 8
N_EXPERTS = 32
D_EXPERT = 512
PLE_DIM = 256
MOE_BLOCK = 128
EPS = 1e-6
MAIN_COLS = 2 * CONV_WIDTH + 3 * ATTN_WIDTH

LANES = 128
CONV_HALO = 32
VMEM_LIMIT = 56 * 1024 * 1024

F32 = jnp.float32
BF16 = jnp.bfloat16
NEG_BIG = -1e30
LOG2E = 1.4426950408889634


def _dot(a, b):
    return jnp.dot(a, b, preferred_element_type=F32)


def _params(sem):
    return pltpu.CompilerParams(dimension_semantics=sem, vmem_limit_bytes=VMEM_LIMIT)


def _resident(shape, index=None):
    index = (0,) * len(shape) if index is None else index
    return pl.BlockSpec(shape, lambda *_: index, pipeline_mode=pl.Buffered(1))


def _inproj_kernel(x_ref, g_ref, wval_ref, wgate_ref, wq_ref, wk_ref, wv_ref, wf_ref, bf_ref, qg_ref, kg_ref,
                   u_ref, q_ref, k_ref, kb_ref, v_ref, vb_ref, vbt_ref, lf_ref):
    x = x_ref[...]
    ms = jnp.mean(x * x, axis=-1, keepdims=True)
    a = (x * lax.rsqrt(ms + EPS) * g_ref[...]).astype(BF16)

    ch = 256
    for c in range(0, CONV_WIDTH, ch):
        val = _dot(a, wval_ref[:, c:c + ch])
        gate = _dot(a, wgate_ref[:, c:c + ch])
        u_ref[:, c:c + ch] = val * jax.nn.sigmoid(gate)

    def head_norm(z, gain):
        return z * lax.rsqrt(jnp.mean(z * z, axis=-1, keepdims=True) + EPS) * gain

    scale = LOG2E * HEAD_DIM ** -0.5
    for c in range(0, ATTN_WIDTH, ch):
        zq = _dot(a, wq_ref[:, c:c + ch])
        zk = _dot(a, wk_ref[:, c:c + ch])
        zv = _dot(a, wv_ref[:, c:c + ch])
        for s in range(0, ch, HEAD_DIM):
            qn = head_norm(zq[:, s:s + HEAD_DIM], qg_ref[...])
            kn = head_norm(zk[:, s:s + HEAD_DIM], kg_ref[...])
            q_ref[:, c + s:c + s + HEAD_DIM] = (qn * scale).astype(BF16)
            k_ref[:, c + s:c + s + HEAD_DIM] = kn
            kb_ref[:, c + s:c + s + HEAD_DIM] = kn.astype(BF16)
        v_ref[:, c:c + ch] = zv
        vb_ref[:, c:c + ch] = zv.astype(BF16)
        vbt_ref[c:c + ch, :] = zv.T.astype(BF16)

    f = _dot(a, wf_ref[...]) + bf_ref[...]
    lf = jnp.minimum(f, 0.0) - jnp.log1p(jnp.exp(-jnp.abs(f)))
    lf_ref[...] = lf[:, :N_HEADS]


def _in_proj(x, g_mix, w_parts, w_f, b_f, q_gain, k_gain, tm):
    n = x.shape[0]
    row = lambda w: pl.BlockSpec((tm, w), lambda i: (i, 0))
    out_shape = (
        jax.ShapeDtypeStruct((n, CONV_WIDTH), F32),
        jax.ShapeDtypeStruct((n, ATTN_WIDTH), BF16),
        jax.ShapeDtypeStruct((n, ATTN_WIDTH), F32),
        jax.ShapeDtypeStruct((n, ATTN_WIDTH), BF16),
        jax.ShapeDtypeStruct((n, ATTN_WIDTH), F32),
        jax.ShapeDtypeStruct((n, ATTN_WIDTH), BF16),
        jax.ShapeDtypeStruct((ATTN_WIDTH, n), BF16),
        jax.ShapeDtypeStruct((n, N_HEADS), F32),
    )
    return pl.pallas_call(
        _inproj_kernel,
        grid=(n // tm,),
        in_specs=[row(D_MODEL), _resident((1, D_MODEL))] + [_resident((D_MODEL, CONV_WIDTH))] * 5 + [
                  _resident((D_MODEL, LANES)), _resident((1, LANES)),
                  _resident((1, HEAD_DIM)), _resident((1, HEAD_DIM))],
        out_specs=(row(CONV_WIDTH), row(ATTN_WIDTH), row(ATTN_WIDTH), row(ATTN_WIDTH),
                   row(ATTN_WIDTH), row(ATTN_WIDTH),
                   pl.BlockSpec((ATTN_WIDTH, tm), lambda i: (0, i)), row(N_HEADS)),
        out_shape=out_shape,
        compiler_params=_params(("parallel",)),
        name="in_proj",
    )(x, g_mix, *w_parts, w_f, b_f, q_gain, k_gain)


def _cumsum_kernel(x_ref, o_ref):
    x = x_ref[...]
    width = x.shape[1]
    lane = lax.broadcasted_iota(jnp.int32, x.shape, 1)
    s = 1
    while s < width:
        x = x + jnp.where(lane >= s, pltpu.roll(x, s, axis=1), 0.0)
        s *= 2
    o_ref[...] = x


def _cumsum_lanes(x):
    return pl.pallas_call(
        _cumsum_kernel,
        out_shape=jax.ShapeDtypeStruct(x.shape, F32),
        name="cumsum",
    )(x)


def _conv_kernel(u_ref, halo_ref, ctx_ref, w_ref, bdw_ref, lng_ref, lnb_ref, gc_ref,
                 o_ref, ext_ref, y_ref, *, tm):
    i = pl.program_id(1)

    @pl.when(i == 0)
    def _():
        ext_ref[0:CONV_HALO, :] = ctx_ref[0]

    @pl.when(i > 0)
    def _():
        ext_ref[0:CONV_HALO, :] = halo_ref[...]

    ext_ref[CONV_HALO:CONV_HALO + tm, :] = u_ref[...]

    rows = min(64, tm)
    ch = LANES
    sub = 8
    wlen = rows + CONV_HALO
    first = CONV_HALO - CONV_STATE

    def conv_rows(r, carry):
        r0 = pl.multiple_of(r * rows, rows)
        for c in range(0, CONV_WIDTH, ch):
            acc = jnp.zeros((rows, ch), F32)
            win = ext_ref[pl.ds(r0, wlen), c:c + ch]
            for rho in range(sub):
                sh = win if rho == 0 else pltpu.roll(win, wlen - rho, axis=0)
                for a in range(wlen // sub):
                    j = sub * a + rho - first
                    if 0 <= j < CONV_KERNEL:
                        acc = acc + sh[sub * a:sub * a + rows] * w_ref[j:j + 1, c:c + ch]
            y_ref[pl.ds(r0, rows), c:c + ch] = acc
        return carry

    lax.fori_loop(0, tm // rows, conv_rows, 0)

    def norm_rows(r, carry):
        r0 = pl.multiple_of(r * rows, rows)
        y = y_ref[pl.ds(r0, rows), :] + bdw_ref[...]
        mu = jnp.mean(y, axis=-1, keepdims=True)
        yc = y - mu
        var = jnp.mean(yc * yc, axis=-1, keepdims=True)
        z = yc * lax.rsqrt(var + EPS) * lng_ref[...] + lnb_ref[...]
        s = z * jax.nn.sigmoid(z)
        ms = jnp.mean(s * s, axis=-1, keepdims=True)
        o_ref[pl.ds(r0, rows), :] = (s * lax.rsqrt(ms + EPS) * gc_ref[...]).astype(BF16)
        return carry

    lax.fori_loop(0, tm // rows, norm_rows, 0)


def _conv_module(u, ctx, w_dw, b_dw, ln_g, ln_b, gc, batch, seq, tm):
    nt = seq // tm
    hb = tm // CONV_HALO
    kernel = functools.partial(_conv_kernel, tm=tm)
    return pl.pallas_call(
        kernel,
        grid=(batch, nt),
        in_specs=[
            pl.BlockSpec((tm, CONV_WIDTH), lambda b, i: (b * nt + i, 0)),
            pl.BlockSpec((CONV_HALO, CONV_WIDTH),
                         lambda b, i: (jnp.maximum((b * nt + i) * hb - 1, 0), 0)),
            pl.BlockSpec((1, CONV_HALO, CONV_WIDTH), lambda b, i: (b, 0, 0)),
            _resident((CONV_HALO, CONV_WIDTH)),
            _resident((1, CONV_WIDTH)), _resident((1, CONV_WIDTH)),
            _resident((1, CONV_WIDTH)), _resident((1, CONV_WIDTH)),
        ],
        out_specs=pl.BlockSpec((tm, CONV_WIDTH), lambda b, i: (b * nt + i, 0)),
        out_shape=jax.ShapeDtypeStruct((batch * seq, CONV_WIDTH), BF16),
        scratch_shapes=[pltpu.VMEM((CONV_HALO + tm, CONV_WIDTH), F32),
                        pltpu.VMEM((tm, CONV_WIDTH), F32)],
        compiler_params=_params(("parallel", "arbitrary")),
        name="conv_module",
    )(u, u, ctx, w_dw, b_dw, ln_g, ln_b, gc)


def _qk(q, k):
    return lax.dot_general(q, k, (((1,), (1,)), ((), ())), preferred_element_type=F32)


AUG_TERMS = 3
QK_AHEAD = 4


def _aug_kernel(f_ref, qa_ref, ka_ref):
    f = f_ref[...] * LOG2E
    tm = f.shape[0]
    lane = lax.broadcasted_iota(jnp.int32, (tm, HEAD_DIM), 1)
    for h in range(N_HEADS):
        hs = slice(h * HEAD_DIM, (h + 1) * HEAD_DIM)
        rest = f[:, h:h + 1]
        qa = jnp.where((lane >= AUG_TERMS) & (lane < 2 * AUG_TERMS), 1.0, 0.0)
        ka = jnp.where(lane < AUG_TERMS, 1.0, 0.0)
        for t in range(AUG_TERMS):
            piece = rest.astype(BF16).astype(F32)
            rest = rest - piece
            qa = jnp.where(lane == t, piece, qa)
            ka = jnp.where(lane == AUG_TERMS + t, -piece, ka)
        qa_ref[:, hs] = qa.astype(BF16)
        ka_ref[:, hs] = ka.astype(BF16)


def _attn_aug(f_col, tm):
    n = f_col.shape[0]
    return pl.pallas_call(
        _aug_kernel,
        grid=(n // tm,),
        in_specs=[pl.BlockSpec((tm, N_HEADS), lambda i: (i, 0))],
        out_specs=(pl.BlockSpec((tm, ATTN_WIDTH), lambda i: (i, 0)),
                   pl.BlockSpec((tm, ATTN_WIDTH), lambda i: (i, 0))),
        out_shape=(jax.ShapeDtypeStruct((n, ATTN_WIDTH), BF16),
                   jax.ShapeDtypeStruct((n, ATTN_WIDTH), BF16)),
        compiler_params=_params(("parallel",)),
        name="attn_aug",
    )(f_col)


def _attn_prompt_kernel(q_ref, qa_ref, k_ref, ka_ref, vt_ref, o_ref, m_ref, l_ref, acc_ref, qt_ref,
                        sp_ref, *, tq, tk):
    i = pl.program_id(1)
    key = lax.broadcasted_iota(jnp.int32, (tk, tq), 0)
    qry = lax.broadcasted_iota(jnp.int32, (tk, tq), 1)

    m_ref[...] = jnp.full(m_ref.shape, NEG_BIG, F32)
    l_ref[...] = jnp.zeros(l_ref.shape, F32)
    acc_ref[...] = jnp.zeros(acc_ref.shape, F32)
    for h in range(N_HEADS):
        hs = slice(h * HEAD_DIM, (h + 1) * HEAD_DIM)
        qt_ref[h] = jnp.concatenate([q_ref[:, hs], qa_ref[:, hs]], axis=1).T

    def scores(ks, h):
        hs = slice(h * HEAD_DIM, (h + 1) * HEAD_DIM)
        kf = jnp.concatenate([k_ref[pl.ds(ks, tk), hs], ka_ref[pl.ds(ks, tk), hs]], axis=1)
        return _dot(kf, qt_ref[h])

    def tile_step(ks, ks_next, masked):
        pending = [sp_ref[a] for a in range(QK_AHEAD)]
        for h in range(N_HEADS):
            hs = slice(h * HEAD_DIM, (h + 1) * HEAD_DIM)
            s = pending.pop(0)
            if h + QK_AHEAD < N_HEADS:
                pending.append(scores(ks, h + QK_AHEAD))
            elif ks_next is not None:
                pending.append(scores(ks_next, h + QK_AHEAD - N_HEADS))
            if masked:
                s = jnp.where(key <= qry, s, -jnp.inf)
            m = m_ref[h]
            m_new = jnp.maximum(m, jnp.max(s, axis=0, keepdims=True))
            alpha = jnp.exp2(m - m_new)
            p = jnp.exp2(s - m_new)
            m_ref[h] = m_new
            l_ref[h] = alpha * l_ref[h] + jnp.sum(p, axis=0, keepdims=True)
            pv = _dot(vt_ref[hs, pl.ds(ks, tk)], p.astype(BF16))
            acc_ref[h] = alpha * acc_ref[h] + pv
        for a, s in enumerate(pending):
            sp_ref[a] = s

    def body(j, carry):
        tile_step(pl.multiple_of(j * tk, tk), pl.multiple_of((j + 1) * tk, tk), False)
        return carry

    for a in range(QK_AHEAD):
        sp_ref[a] = scores(0, a)
    lax.fori_loop(0, i, body, 0)
    tile_step(pl.multiple_of(i * tk, tk), None, True)
    for h in range(N_HEADS):
        o_ref[:, h * HEAD_DIM:(h + 1) * HEAD_DIM] = (acc_ref[h] / l_ref[h]).T


def _attn_prompt(q, qa, kb, ka, vbt, batch, seq, tq, tk):
    nq = seq // tq
    kernel = functools.partial(_attn_prompt_kernel, tq=tq, tk=tk)
    qblk = pl.BlockSpec((tq, ATTN_WIDTH), lambda b, i: (b * nq + i, 0))
    kblk = pl.BlockSpec((seq, ATTN_WIDTH), lambda b, i: (b, 0), pipeline_mode=pl.Buffered(1))
    vblk = pl.BlockSpec((ATTN_WIDTH, seq), lambda b, i: (0, b), pipeline_mode=pl.Buffered(1))
    return pl.pallas_call(
        kernel,
        grid=(batch, nq),
        in_specs=[qblk, qblk, kblk, kblk, vblk],
        out_specs=pl.BlockSpec((tq, ATTN_WIDTH), lambda b, i: (b * nq + i, 0)),
        out_shape=jax.ShapeDtypeStruct((batch * seq, ATTN_WIDTH), F32),
        scratch_shapes=[pltpu.VMEM((N_HEADS, 1, tq), F32), pltpu.VMEM((N_HEADS, 1, tq), F32),
                        pltpu.VMEM((N_HEADS, HEAD_DIM, tq), F32),
                        pltpu.VMEM((N_HEADS, 2 * HEAD_DIM, tq), BF16),
                        pltpu.VMEM((QK_AHEAD, tk, tq), F32)],
        compiler_params=_params(("parallel", "arbitrary")),
        name="attn_prompt",
    )(q, qa, kb, ka, vbt)


def _attn_sample_kernel(q_ref, kn_ref, vn_ref, ck_hbm, cv_hbm, fc_ref, fnrow_ref, fncol_ref,
                        o_ref, kbuf_ref, vbuf_ref, sem, *, t_new, past):
    b = pl.program_id(0)
    slot = b % 2

    def fetch(batch_idx, into):
        copies = []
        for h in range(N_HEADS):
            copies.append(pltpu.make_async_copy(ck_hbm.at[batch_idx, :, h, :], kbuf_ref.at[into, h],
                                                sem.at[into, 0]))
            copies.append(pltpu.make_async_copy(cv_hbm.at[batch_idx, :, h, :], vbuf_ref.at[into, h],
                                                sem.at[into, 1]))
        return copies

    @pl.when(b == 0)
    def _():
        for copy in fetch(0, 0):
            copy.start()

    @pl.when(b + 1 < pl.num_programs(0))
    def _():
        for copy in fetch(b + 1, 1 - slot):
            copy.start()

    for copy in fetch(b, slot):
        copy.wait()

    row = lax.broadcasted_iota(jnp.int32, (t_new, t_new), 0)
    col = lax.broadcasted_iota(jnp.int32, (t_new, t_new), 1)
    causal = col <= row
    for h in range(N_HEADS):
        hs = slice(h * HEAD_DIM, (h + 1) * HEAD_DIM)
        q = q_ref[:, hs]
        fn_q = fncol_ref[:, h:h + 1]
        fc = fc_ref[0, h:h + 1, :]
        fc_last = fc[:, past - 1:past]
        s_c = _qk(q, kbuf_ref[slot, h].astype(BF16)) + ((fc_last + fn_q) - fc) * LOG2E
        fn_k = fnrow_ref[0, h:h + 1, 0:t_new]
        s_n = _qk(q, kn_ref[:, hs]) + (fn_q - fn_k) * LOG2E
        s_n = jnp.where(causal, s_n, -jnp.inf)
        m = jnp.maximum(jnp.max(s_c, axis=-1, keepdims=True), jnp.max(s_n, axis=-1, keepdims=True))
        p_c = jnp.exp2(s_c - m)
        p_n = jnp.exp2(s_n - m)
        l = jnp.sum(p_c, axis=-1, keepdims=True) + jnp.sum(p_n, axis=-1, keepdims=True)
        acc = _dot(p_c.astype(BF16), vbuf_ref[slot, h].astype(BF16)) + _dot(p_n.astype(BF16), vn_ref[:, hs])
        o_ref[:, hs] = acc / l


def _attn_sample(q, kb, vb, cache_k, cache_v, fc_row, fn_row, fn_col, batch, t_new, past):
    kernel = functools.partial(_attn_sample_kernel, t_new=t_new, past=past)
    tok = lambda w: pl.BlockSpec((t_new, w), lambda b: (b, 0))
    return pl.pallas_call(
        kernel,
        grid=(batch,),
        in_specs=[
            tok(ATTN_WIDTH), tok(ATTN_WIDTH), tok(ATTN_WIDTH),
            pl.BlockSpec(memory_space=pl.ANY),
            pl.BlockSpec(memory_space=pl.ANY),
            pl.BlockSpec((1, N_HEADS, past), lambda b: (b, 0, 0)),
            pl.BlockSpec((1, N_HEADS, LANES), lambda b: (b, 0, 0)),
            tok(N_HEADS),
        ],
        out_specs=tok(ATTN_WIDTH),
        out_shape=jax.ShapeDtypeStruct((batch * t_new, ATTN_WIDTH), F32),
        scratch_shapes=[pltpu.VMEM((2, N_HEADS, past, HEAD_DIM), F32),
                        pltpu.VMEM((2, N_HEADS, past, HEAD_DIM), F32),
                        pltpu.SemaphoreType.DMA((2, 2))],
        compiler_params=_params(("arbitrary",)),
        name="attn_sample",
    )(q, kb, vb, cache_k, cache_v, fc_row, fn_row, fn_col)


def _pack_bf16_pair(lo, hi):
    lo_bits = lax.bitcast_convert_type(lo, jnp.uint32) >> 16
    hi_bits = lax.bitcast_convert_type(hi, jnp.uint32) & jnp.uint32(0xFFFF0000)
    return lo_bits | hi_bits


def _unpack_bf16_pair(w):
    lo = lax.bitcast_convert_type(w << 16, F32).astype(BF16)
    hi = lax.bitcast_convert_type(w & jnp.uint32(0xFFFF0000), F32).astype(BF16)
    return lo, hi


def _merge_kernel(yc_ref, ya_ref, x_ref, ga_ref, woc_ref, woa_ref, gf_ref, wrh_ref, wrl_ref, br_ref,
                  *rest, n_steps):
    tail_ref = rest[0] if len(rest) == 4 else None
    h_ref, xp_ref, lg_ref = rest[-3:]
    i = pl.program_id(0)

    @pl.when(i < n_steps)
    def _():
        ya = ya_ref[...]
        ya_n = (ya * lax.rsqrt(jnp.mean(ya * ya, axis=-1, keepdims=True) + EPS) * ga_ref[...]).astype(BF16)
        y = _dot(yc_ref[...], woc_ref[...]) + _dot(ya_n, woa_ref[...])
        h = x_ref[...] + y
        h_ref[...] = h
        xn = h * lax.rsqrt(jnp.mean(h * h, axis=-1, keepdims=True) + EPS) * gf_ref[...]
        xn_hi = xn.astype(BF16)
        xn_hi32 = xn_hi.astype(F32)
        xn_lo = (xn - xn_hi32).astype(BF16)
        lg_ref[...] = (_dot(xn_hi, wrh_ref[...]) + _dot(xn_lo, wrh_ref[...]) + _dot(xn_hi, wrl_ref[...])
                       + br_ref[...])
        half = D_MODEL // 2
        xp_ref[...] = _pack_bf16_pair(xn_hi32[:, :half], xn_hi32[:, half:])

    if tail_ref is not None:
        @pl.when(i == n_steps)
        def _():
            xp_ref[...] = tail_ref[...]


def _merge(yc_n, ya, x, ga, w_out, g_ffn, wr_hi, wr_lo, b_r, tm, tail=None):
    n = x.shape[0]
    n_steps = n // tm
    extra = 0 if tail is None else 1
    row = lambda w: pl.BlockSpec((tm, w), lambda i: (jnp.minimum(i, n_steps - 1), 0))
    in_specs = [row(CONV_WIDTH), row(ATTN_WIDTH), row(D_MODEL), _resident((1, ATTN_WIDTH)),
                _resident((CONV_WIDTH, D_MODEL), (0, 0)), _resident((ATTN_WIDTH, D_MODEL), (1, 0)),
                _resident((1, D_MODEL)),
                _resident((D_MODEL, LANES)), _resident((D_MODEL, LANES)), _resident((1, LANES))]
    args = [yc_n, ya, x, ga, w_out, w_out, g_ffn, wr_hi, wr_lo, b_r]
    if tail is not None:
        assert tail.shape == (tm, D_MODEL // 2)
        in_specs.append(_resident(tail.shape))
        args.append(tail)
    return pl.pallas_call(
        functools.partial(_merge_kernel, n_steps=n_steps),
        grid=(n_steps + extra,),
        in_specs=in_specs,
        out_specs=(row(D_MODEL), pl.BlockSpec((tm, D_MODEL // 2), lambda i: (i, 0)), row(LANES)),
        out_shape=(jax.ShapeDtypeStruct((n, D_MODEL), F32),
                   jax.ShapeDtypeStruct((n + extra * tm, D_MODEL // 2), jnp.uint32),
                   jax.ShapeDtypeStruct((n, LANES), F32)),
        compiler_params=_params(("arbitrary",)),
        name="merge_out",
    )(*args)


def _route_kernel(lg_ref, info_ref, infot_ref, cnt_ref, carry_ref, *, tm):
    step = pl.program_id(0)

    @pl.when(step == 0)
    def _():
        carry_ref[...] = jnp.zeros_like(carry_ref)

    lg = lg_ref[...]
    lane = lax.broadcasted_iota(jnp.int32, lg.shape, 1)
    lanef = lane.astype(F32)
    big = jnp.float32(1e9)
    rmax = lambda v: jnp.max(v, axis=-1, keepdims=True)
    rmin = lambda v: jnp.min(v, axis=-1, keepdims=True)
    rsum = lambda v: jnp.sum(v, axis=-1, keepdims=True)

    is_g = (lane >= N_EXPERTS) & (lane < N_EXPERTS + N_GROUPS)
    gl = jnp.where(is_g, lg, NEG_BIG)
    gmax = rmax(gl)
    gsum = rsum(jnp.where(is_g, jnp.exp(gl - gmax), 0.0))
    pg_star = 1.0 / gsum
    g_idx = rmin(jnp.where(is_g & (gl == gmax), lanef - N_EXPERTS, big))

    e_lo = g_idx * EXPERTS_PER_GROUP
    is_e = (lanef >= e_lo) & (lanef < e_lo + EXPERTS_PER_GROUP)
    el = jnp.where(is_e, lg, NEG_BIG)
    m1 = rmax(el)
    i1 = rmin(jnp.where(is_e & (el == m1), lanef, big))
    sel1 = lanef == i1
    el2 = jnp.where(sel1, NEG_BIG, el)
    m2 = rmax(el2)
    i2 = rmin(jnp.where(is_e & (el2 == m2) & jnp.logical_not(sel1), lanef, big))
    sel2 = lanef == i2
    z = rsum(jnp.where(is_e, jnp.exp(el - m1), 0.0))
    p1 = 1.0 / z
    p2 = jnp.exp(m2 - m1) / z
    gate1 = pg_star * p1 / (p1 + p2)
    gate2 = pg_star * p2 / (p1 + p2)

    onehot = jnp.where(sel1 | sel2, 1.0, 0.0)
    r = lax.broadcasted_iota(jnp.int32, (tm, tm), 0)
    c = lax.broadcasted_iota(jnp.int32, (tm, tm), 1)
    tri = jnp.where(c < r, 1.0, 0.0).astype(BF16)
    before = _dot(tri, onehot.astype(BF16)) + carry_ref[...]
    rank1 = rsum(jnp.where(sel1, before, 0.0))
    rank2 = rsum(jnp.where(sel2, before, 0.0))
    carry_ref[...] = carry_ref[...] + jnp.sum(onehot, axis=0, keepdims=True)
    cnt_ref[...] = carry_ref[...]

    info = jnp.zeros_like(lg)
    for k, val in enumerate((i1, i2, rank1, rank2, gate1, gate2)):
        info = jnp.where(lane == k, val, info)
    info_ref[...] = info
    infot_ref[...] = info.T[0:8, :]


def _route(logits, tm):
    n = logits.shape[0]
    kernel = functools.partial(_route_kernel, tm=tm)
    return pl.pallas_call(
        kernel,
        grid=(n // tm,),
        in_specs=[pl.BlockSpec((tm, LANES), lambda i: (i, 0))],
        out_specs=(pl.BlockSpec((tm, LANES), lambda i: (i, 0)),
                   pl.BlockSpec((8, tm), lambda i: (0, i)),
                   pl.BlockSpec((1, LANES), lambda i: (0, 0))),
        out_shape=(jax.ShapeDtypeStruct((n, LANES), F32), jax.ShapeDtypeStruct((8, n), F32),
                   jax.ShapeDtypeStruct((1, LANES), F32)),
        scratch_shapes=[pltpu.VMEM((1, LANES), F32)],
        compiler_params=_params(("arbitrary",)),
        name="route",
    )(logits)


def _slot_tokens_kernel(dest_ref, tok_ref):
    n_all = dest_ref.shape[0] // 2

    def clear(s, carry):
        tok_ref[s] = 0
        return carry

    def place(t, carry):
        tok_ref[dest_ref[t]] = t
        tok_ref[dest_ref[n_all + t]] = t
        return carry

    lax.fori_loop(0, tok_ref.shape[0], clear, 0, unroll=8)
    lax.fori_loop(0, n_all, place, 0, unroll=8)


def _slot_tokens(dest_flat, n_slots):
    grid_spec = pltpu.PrefetchScalarGridSpec(
        num_scalar_prefetch=1, grid=(1,), in_specs=[],
        out_specs=pl.BlockSpec(memory_space=pltpu.SMEM))
    return pl.pallas_call(
        _slot_tokens_kernel,
        grid_spec=grid_spec,
        out_shape=jax.ShapeDtypeStruct((n_slots,), jnp.int32),
        compiler_params=pltpu.CompilerParams(dimension_semantics=("arbitrary",)),
        name="slot_tokens",
    )(dest_flat)


CAST_CHUNK_ELEMS = 64 * 1024
WEIGHT_DMA_SPLIT = 4


def _expert_kernel(be_ref, nu_ref, nx_ref, tok_ref, x_hbm, w1_hbm, w3_hbm, w2_hbm, y_ref,
                   xbuf_ref, w1f_ref, w3f_ref, w2f_ref, w1b_ref, w3b_ref, w2b_ref, sem, gsem):
    b = pl.program_id(0)
    n_used = nu_ref[0]
    active = b < n_used
    slot = b % 2
    new_expert = jnp.logical_or(b == 0, be_ref[b] != be_ref[jnp.maximum(b - 1, 0)])

    def fetch(e):
        copies = []
        for k, (src, dst) in enumerate(((w1_hbm, w1f_ref), (w3_hbm, w3f_ref), (w2_hbm, w2f_ref))):
            slab = dst.shape[0] // WEIGHT_DMA_SPLIT
            for c in range(WEIGHT_DMA_SPLIT):
                rows = pl.ds(c * slab, slab)
                copies.append(pltpu.make_async_copy(src.at[e, rows], dst.at[rows], sem.at[k]))
        return copies

    def row_copy(blk, into, r):
        return pltpu.make_async_copy(x_hbm.at[pl.ds(tok_ref[blk * MOE_BLOCK + r], 1)],
                                     xbuf_ref.at[into, pl.ds(r, 1)], gsem.at[into])

    def drain_rows(into):
        def body(r, carry):
            pltpu.make_async_copy(x_hbm.at[pl.ds(0, 1)], xbuf_ref.at[into, pl.ds(r, 1)],
                                  gsem.at[into]).wait()
            return carry
        lax.fori_loop(0, MOE_BLOCK, body, 0, unroll=8)

    @pl.when(b == 0)
    def _():
        for copy in fetch(be_ref[0]):
            copy.start()

        def body(r, carry):
            row_copy(0, 0, r).start()
            return carry
        lax.fori_loop(0, MOE_BLOCK, body, 0, unroll=8)

    @pl.when(jnp.logical_and(active, new_expert))
    def _():
        for copy in fetch(be_ref[b]):
            copy.wait()
        for src, dst in ((w1f_ref, w1b_ref), (w3f_ref, w3b_ref), (w2f_ref, w2b_ref)):
            rows, cols = src.shape
            chunk = CAST_CHUNK_ELEMS // cols

            def cast_rows(c, carry, src=src, dst=dst, chunk=chunk):
                r0 = pl.multiple_of(c * chunk, chunk)
                dst[pl.ds(r0, chunk), :] = src[pl.ds(r0, chunk), :].astype(BF16)
                return carry

            lax.fori_loop(0, rows // chunk, cast_rows, 0)

        @pl.when(nx_ref[b] >= 0)
        def _():
            for copy in fetch(nx_ref[b]):
                copy.start()

    @pl.when(active)
    def _():
        drain_rows(slot)
        lo, hi = _unpack_bf16_pair(xbuf_ref[slot])
        half = D_MODEL // 2
        nxt = jnp.minimum(b + 1, n_used - 1)
        parts = [range(p * MOE_BLOCK // 3, (p + 1) * MOE_BLOCK // 3) for p in range(3)]
        for r in parts[0]:
            row_copy(nxt, 1 - slot, r).start()
        h1 = _dot(lo, w1b_ref[:half, :]) + _dot(hi, w1b_ref[half:, :])
        for r in parts[1]:
            row_copy(nxt, 1 - slot, r).start()
        h3 = _dot(lo, w3b_ref[:half, :]) + _dot(hi, w3b_ref[half:, :])
        for r in parts[2]:
            row_copy(nxt, 1 - slot, r).start()
        h = (h1 * jax.nn.sigmoid(h1) * h3).astype(BF16)
        y_ref[...] = _dot(h, w2b_ref[...])

        @pl.when(b == n_used - 1)
        def _():
            drain_rows(1 - slot)

    @pl.when(jnp.logical_not(active))
    def _():
        y_ref[...] = jnp.zeros_like(y_ref)


def _experts(block_e, n_used, next_e, slot_tok, x_packed, w1, w3, w2):
    n_slots = slot_tok.shape[0]
    nb = n_slots // MOE_BLOCK
    hbm = pl.BlockSpec(memory_space=pl.ANY)
    grid_spec = pltpu.PrefetchScalarGridSpec(
        num_scalar_prefetch=4,
        grid=(nb,),
        in_specs=[hbm, hbm, hbm, hbm],
        out_specs=pl.BlockSpec((MOE_BLOCK, D_MODEL), lambda b, be, nu, nx, tok: (b, 0)),
        scratch_shapes=[pltpu.VMEM((2, MOE_BLOCK, D_MODEL // 2), jnp.uint32),
                        pltpu.VMEM((D_MODEL, D_EXPERT), F32), pltpu.VMEM((D_MODEL, D_EXPERT), F32),
                        pltpu.VMEM((D_EXPERT, D_MODEL), F32),
                        pltpu.VMEM((D_MODEL, D_EXPERT), BF16), pltpu.VMEM((D_MODEL, D_EXPERT), BF16),
                        pltpu.VMEM((D_EXPERT, D_MODEL), BF16),
                        pltpu.SemaphoreType.DMA((3,)), pltpu.SemaphoreType.DMA((2,))],
    )
    return pl.pallas_call(
        _expert_kernel,
        grid_spec=grid_spec,
        out_shape=jax.ShapeDtypeStruct((n_slots, D_MODEL), F32),
        compiler_params=_params(("arbitrary",)),
        name="experts",
    )(block_e, n_used, next_e, slot_tok, x_packed, w1, w3, w2)


PLE_SPLIT = 4


def _combine_ple_kernel(dest_ref, h_ref, info_ref, p_ref, g_ref, *rest, tm, t_off):
    wpg_refs = rest[:PLE_SPLIT]
    wpp_ref, yb_hbm, o_ref, buf_ref, sem = rest[PLE_SPLIT:]
    i = pl.program_id(0)
    last = pl.num_programs(0) - 1
    n_all = dest_ref.shape[0] // 2
    slot = i % 2

    def row_copy(step, into, r, k):
        t = t_off + step * tm + r
        return pltpu.make_async_copy(yb_hbm.at[pl.ds(dest_ref[k * n_all + t], 1)],
                                     buf_ref.at[into, k, pl.ds(r, 1)], sem.at[into])

    def drain(into):
        def body(r, carry):
            for k in range(2):
                pltpu.make_async_copy(yb_hbm.at[pl.ds(0, 1)], buf_ref.at[into, k, pl.ds(r, 1)],
                                      sem.at[into]).wait()
            return carry
        lax.fori_loop(0, tm, body, 0, unroll=8)

    @pl.when(i == 0)
    def _():
        def body(r, carry):
            row_copy(0, 0, r, 0).start()
            row_copy(0, 0, r, 1).start()
            return carry
        lax.fori_loop(0, tm, body, 0, unroll=8)

    drain(slot)
    h2 = h_ref[...] + (info_ref[:, 4:5] * buf_ref[slot, 0] + info_ref[:, 5:6] * buf_ref[slot, 1])
    hn = (h2 * lax.rsqrt(jnp.mean(h2 * h2, axis=-1, keepdims=True) + EPS) * g_ref[...]).astype(BF16)
    pb = p_ref[...].astype(BF16)

    nxt = jnp.minimum(i + 1, last)
    ch = D_MODEL // PLE_SPLIT
    per = tm // PLE_SPLIT
    for k in range(PLE_SPLIT):
        for r in range(k * per, (k + 1) * per):
            row_copy(nxt, 1 - slot, r, 0).start()
            row_copy(nxt, 1 - slot, r, 1).start()
        cs = slice(k * ch, (k + 1) * ch)
        gate = jax.nn.sigmoid(_dot(hn, wpg_refs[k][...]))
        o_ref[:, cs] = h2[:, cs] + gate * _dot(pb, wpp_ref[:, cs])

    @pl.when(i == last)
    def _():
        drain(1 - slot)


def _combine_ple(dest_flat, h, info, yb, p, g_ple, w_pg, w_pp, t_off, tm):
    n = h.shape[0]
    kernel = functools.partial(_combine_ple_kernel, tm=tm, t_off=t_off)
    ob = t_off // tm
    row = lambda w: pl.BlockSpec((tm, w), lambda i, d: (i, 0))
    grid_spec = pltpu.PrefetchScalarGridSpec(
        num_scalar_prefetch=1,
        grid=(n // tm,),
        in_specs=[row(D_MODEL), pl.BlockSpec((tm, LANES), lambda i, d: (i + ob, 0)), row(PLE_DIM),
                  _resident((1, D_MODEL))]
        + [_resident((D_MODEL, D_MODEL // PLE_SPLIT), (0, k)) for k in range(PLE_SPLIT)]
        + [_resident((PLE_DIM, D_MODEL)), pl.BlockSpec(memory_space=pl.ANY)],
        out_specs=row(D_MODEL),
        scratch_shapes=[pltpu.VMEM((2, 2, tm, D_MODEL), F32), pltpu.SemaphoreType.DMA((2,))],
    )
    return pl.pallas_call(
        kernel,
        grid_spec=grid_spec,
        out_shape=jax.ShapeDtypeStruct((n, D_MODEL), F32),
        compiler_params=_params(("arbitrary",)),
        name="combine_ple",
    )(dest_flat, h, info, p, g_ple, *([w_pg] * PLE_SPLIT), w_pp, yb)


def _mixer_tokens(x2d, wts, tm):
    return _in_proj(x2d, wts["g_mix"], wts["w_parts"], wts["w_f"], wts["b_f"],
                    wts["q_gain"], wts["k_gain"], tm)


def kernel(x_prompt, x_sample, cache_k, cache_v, cache_logf, cache_conv, p_prompt, p_sample,
           g_mix, w_in, b_f, q_gain, k_gain, w_dw, b_dw, ln_g, ln_b, gc, ga, w_out,
           g_ffn, w_router_g, b_router_g, w_router_e, b_router_e, w1, w3, w2,
           g_ple, w_pg, w_pp):
    batch, seq, _ = x_prompt.shape
    dec_batch, t_new, _ = x_sample.shape
    past = cache_k.shape[2]
    n_p = batch * seq
    n_s = dec_batch * t_new
    n_all = n_p + n_s
    tm = 256
    li = 0

    w_in_l = w_in[li]
    pad_lanes = lambda a: jnp.pad(a, ((0, 0), (0, LANES - a.shape[1])))
    row2d = lambda a: a.reshape(1, -1)
    wts = {
        "g_mix": row2d(g_mix[li]),
        "w_parts": [w_in_l[:, c:c + CONV_WIDTH].astype(BF16) for c in range(0, MAIN_COLS, CONV_WIDTH)],
        "w_f": pad_lanes(w_in_l[:, MAIN_COLS:]).astype(BF16),
        "b_f": pad_lanes(row2d(b_f[li])),
        "q_gain": row2d(q_gain[li]),
        "k_gain": row2d(k_gain[li]),
    }
    w_dw_p = jnp.pad(w_dw[li], ((0, CONV_HALO - CONV_KERNEL), (0, 0)))
    w_out_b = w_out[li].astype(BF16)
    w_r = pad_lanes(jnp.concatenate([w_router_e[li], w_router_g[li]], axis=1))
    wr_hi = w_r.astype(BF16)
    wr_lo = (w_r - wr_hi.astype(F32)).astype(BF16)
    b_r = pad_lanes(row2d(jnp.concatenate([b_router_e[li], b_router_g[li]])))
    w_pg_b, w_pp_b = w_pg[li].astype(BF16), w_pp[li].astype(BF16)

    xp = x_prompt.reshape(n_p, D_MODEL)
    u_p, q_p, k_p, kb_p, v_p, _, vbt_p, lf_p = _mixer_tokens(xp, wts, tm)
    lf_p_row = lf_p.reshape(batch, seq, N_HEADS).transpose(0, 2, 1)
    f_p_row = _cumsum_lanes(lf_p_row.reshape(batch * N_HEADS, seq)).reshape(batch, N_HEADS, seq)
    f_p_col = f_p_row.transpose(0, 2, 1).reshape(n_p, N_HEADS)
    ctx_p = jnp.zeros((batch, CONV_HALO, CONV_WIDTH), F32)
    conv_args = (w_dw_p, row2d(b_dw[li]), row2d(ln_g[li]), row2d(ln_b[li]), row2d(gc[li]))
    yc_p = _conv_module(u_p, ctx_p, *conv_args, batch, seq, tm)
    qa_p, ka_p = _attn_aug(f_p_col, tm)
    ya_p = _attn_prompt(q_p, qa_p, kb_p, ka_p, vbt_p, batch, seq, 256, 256)
    merge_args = (row2d(ga[li]), w_out_b, row2d(g_ffn[li]), wr_hi, wr_lo, b_r)

    xs_ = x_sample.reshape(n_s, D_MODEL)
    u_s, q_s, k_s, kb_s, v_s, vb_s, _, lf_s = _mixer_tokens(xs_, wts, tm)
    clf_row = cache_logf[li].transpose(0, 2, 1).reshape(dec_batch * N_HEADS, past)
    fc_row = _cumsum_lanes(clf_row).reshape(dec_batch, N_HEADS, past)
    lf_s_row = lf_s.reshape(dec_batch, t_new, N_HEADS).transpose(0, 2, 1).reshape(dec_batch * N_HEADS, t_new)
    fn_row = _cumsum_lanes(jnp.pad(lf_s_row, ((0, 0), (0, LANES - t_new)))).reshape(dec_batch, N_HEADS, LANES)
    fn_col = fn_row[:, :, :t_new].transpose(0, 2, 1).reshape(n_s, N_HEADS)
    ctx_s = jnp.pad(cache_conv[li], ((0, 0), (CONV_HALO - CONV_STATE, 0), (0, 0)))
    yc_s = _conv_module(u_s, ctx_s, *conv_args, dec_batch, t_new, t_new)
    ya_s = _attn_sample(q_s, kb_s, vb_s,
                        cache_k[li], cache_v[li],
                        fc_row, fn_row, fn_col, dec_batch, t_new, past)
    h_s, xpk_s, lg_s = _merge(yc_s, ya_s, xs_, *merge_args, tm)
    h_p, xpk, lg_p = _merge(yc_p, ya_p, xp, *merge_args, tm, tail=xpk_s)

    info, info_t, counts = _route(jnp.concatenate([lg_p, lg_s], axis=0), tm)
    counts = counts[0, :N_EXPERTS].astype(jnp.int32)
    bcounts = (counts + MOE_BLOCK - 1) // MOE_BLOCK
    bends = jnp.cumsum(bcounts)
    pstarts = (bends - bcounts) * MOE_BLOCK
    n_rows = n_all * 2
    nb = -(-(n_rows + N_EXPERTS * (MOE_BLOCK - 1)) // MOE_BLOCK)
    e_idx = info_t[0:2].astype(jnp.int32)
    expert_ids = jnp.arange(N_EXPERTS, dtype=jnp.int32)[:, None, None]
    seg_start = jnp.sum(jnp.where(e_idx[None] == expert_ids, pstarts[:, None, None], 0), axis=0)
    dest = (seg_start + info_t[2:4].astype(jnp.int32)).reshape(n_rows)
    block_e = jnp.minimum(jnp.sum(bends[None, :] <= jnp.arange(nb, dtype=jnp.int32)[:, None], axis=1),
                          N_EXPERTS - 1).astype(jnp.int32)
    n_used = bends[N_EXPERTS - 1:].astype(jnp.int32)
    slot_tok = _slot_tokens(dest, nb * MOE_BLOCK)
    seg_end = jnp.sum(jnp.where(block_e[None, :] == expert_ids[:, :, 0], bends[:, None], 0), axis=0)
    next_e = jnp.where(seg_end < n_used[0], block_e[jnp.minimum(seg_end, nb - 1)], -1).astype(jnp.int32)
    yb = _experts(block_e, n_used, next_e, slot_tok, xpk, w1[li], w3[li], w2[li])

    ple_args = (row2d(g_ple[li]), w_pg_b, w_pp_b)
    y_p = _combine_ple(dest, h_p, info, yb, p_prompt[li].reshape(n_p, PLE_DIM), *ple_args, 0, tm)
    y_s = _combine_ple(dest, h_s, info, yb, p_sample[li].reshape(n_s, PLE_DIM), *ple_args, n_p, tm)

    heads = lambda a, b, t: a.reshape(1, b, t, N_HEADS, HEAD_DIM)
    return (
        y_p.reshape(batch, seq, D_MODEL),
        y_s.reshape(dec_batch, t_new, D_MODEL),
        heads(k_p, batch, seq), heads(v_p, batch, seq),
        lf_p.reshape(1, batch, seq, N_HEADS),
        u_p.reshape(batch, seq, CONV_WIDTH)[None, :, seq - CONV_STATE:, :],
        heads(k_s, dec_batch, t_new), heads(v_s, dec_batch, t_new),
        lf_s.reshape(1, dec_batch, t_new, N_HEADS),
        u_s.reshape(dec_batch, t_new, CONV_WIDTH)[None, :, t_new - CONV_STATE:, :],
    )
```

```python
import functools

import jax
import jax.numpy as jnp
from jax import lax
from jax.experimental import pallas as pl
from jax.experimental.pallas import tpu as pltpu

D_MODEL = 2048
CONV_WIDTH = 1024
ATTN_WIDTH = 1024
HEAD_DIM = 128
N_HEADS = 8
CONV_KERNEL = 31
CONV_STATE = CONV_KERNEL - 1
N_GROUPS = 4
EXPERTS_PER_GROUP = 8
N_EXPERTS = 32
D_EXPERT = 512
PLE_DIM = 256
MOE_BLOCK = 128
EPS = 1e-6
MAIN_COLS = 2 * CONV_WIDTH + 3 * ATTN_WIDTH

LANES = 128
CONV_HALO = 32
VMEM_LIMIT = 56 * 1024 * 1024

F32 = jnp.float32
BF16 = jnp.bfloat16
NEG_BIG = -1e30
LOG2E = 1.4426950408889634


def _dot(a, b):
    return jnp.dot(a, b, preferred_element_type=F32)


def _params(sem):
    return pltpu.CompilerParams(dimension_semantics=sem, vmem_limit_bytes=VMEM_LIMIT)


def _resident(shape, index=None):
    index = (0,) * len(shape) if index is None else index
    return pl.BlockSpec(shape, lambda *_: index, pipeline_mode=pl.Buffered(1))


def _inproj_kernel(x_ref, g_ref, wval_ref, wgate_ref, wq_ref, wk_ref, wv_ref, wf_ref, bf_ref, qg_ref, kg_ref,
                   u_ref, q_ref, k_ref, kb_ref, v_ref, vb_ref, vbt_ref, lf_ref):
    x = x_ref[...]
    ms = jnp.mean(x * x, axis=-1, keepdims=True)
    a = (x * lax.rsqrt(ms + EPS) * g_ref[...]).astype(BF16)

    ch = 256
    for c in range(0, CONV_WIDTH, ch):
        val = _dot(a, wval_ref[:, c:c + ch])
        gate = _dot(a, wgate_ref[:, c:c + ch])
        u_ref[:, c:c + ch] = val * jax.nn.sigmoid(gate)

    def head_norm(z, gain):
        return z * lax.rsqrt(jnp.mean(z * z, axis=-1, keepdims=True) + EPS) * gain

    scale = LOG2E * HEAD_DIM ** -0.5
    for c in range(0, ATTN_WIDTH, ch):
        zq = _dot(a, wq_ref[:, c:c + ch])
        zk = _dot(a, wk_ref[:, c:c + ch])
        zv = _dot(a, wv_ref[:, c:c + ch])
        for s in range(0, ch, HEAD_DIM):
            qn = head_norm(zq[:, s:s + HEAD_DIM], qg_ref[...])
            kn = head_norm(zk[:, s:s + HEAD_DIM], kg_ref[...])
            q_ref[:, c + s:c + s + HEAD_DIM] = (qn * scale).astype(BF16)
            k_ref[:, c + s:c + s + HEAD_DIM] = kn
            kb_ref[:, c + s:c + s + HEAD_DIM] = kn.astype(BF16)
        v_ref[:, c:c + ch] = zv
        vb_ref[:, c:c + ch] = zv.astype(BF16)
        vbt_ref[c:c + ch, :] = zv.T.astype(BF16)

    f = _dot(a, wf_ref[...]) + bf_ref[...]
    lf = jnp.minimum(f, 0.0) - jnp.log1p(jnp.exp(-jnp.abs(f)))
    lf_ref[...] = lf[:, :N_HEADS]


def _in_proj(x, g_mix, w_main, w_f, b_f, q_gain, k_gain, tm):
    n = x.shape[0]
    row = lambda w: pl.BlockSpec((tm, w), lambda i: (i, 0))
    out_shape = (
        jax.ShapeDtypeStruct((n, CONV_WIDTH), F32),
        jax.ShapeDtypeStruct((n, ATTN_WIDTH), BF16),
        jax.ShapeDtypeStruct((n, ATTN_WIDTH), F32),
        jax.ShapeDtypeStruct((n, ATTN_WIDTH), BF16),
        jax.ShapeDtypeStruct((n, ATTN_WIDTH), F32),
        jax.ShapeDtypeStruct((n, ATTN_WIDTH), BF16),
        jax.ShapeDtypeStruct((ATTN_WIDTH, n), BF16),
        jax.ShapeDtypeStruct((n, N_HEADS), F32),
    )
    return pl.pallas_call(
        _inproj_kernel,
        grid=(n // tm,),
        in_specs=[row(D_MODEL), _resident((1, D_MODEL))]
        + [_resident((D_MODEL, CONV_WIDTH), (0, k)) for k in range(MAIN_COLS // CONV_WIDTH)] + [
                  _resident((D_MODEL, LANES)), _resident((1, LANES)),
                  _resident((1, HEAD_DIM)), _resident((1, HEAD_DIM))],
        out_specs=(row(CONV_WIDTH), row(ATTN_WIDTH), row(ATTN_WIDTH), row(ATTN_WIDTH),
                   row(ATTN_WIDTH), row(ATTN_WIDTH),
                   pl.BlockSpec((ATTN_WIDTH, tm), lambda i: (0, i)), row(N_HEADS)),
        out_shape=out_shape,
        compiler_params=_params(("parallel",)),
        name="in_proj",
    )(x, g_mix, *([w_main] * (MAIN_COLS // CONV_WIDTH)), w_f, b_f, q_gain, k_gain)


def _cast_kernel(w_ref, o_ref):
    o_ref[...] = w_ref[0].astype(BF16)


def _cast_columns_bf16(w, n_cols, block_cols):
    rows = w.shape[1]
    return pl.pallas_call(
        _cast_kernel,
        grid=(n_cols // block_cols,),
        in_specs=[pl.BlockSpec((1, rows, block_cols), lambda j: (0, 0, j))],
        out_specs=pl.BlockSpec((rows, block_cols), lambda j: (0, j)),
        out_shape=jax.ShapeDtypeStruct((rows, n_cols), BF16),
        compiler_params=_params(("parallel",)),
        name="cast_bf16",
    )(w)


def _cumsum_kernel(x_ref, o_ref):
    x = x_ref[...]
    width = x.shape[1]
    lane = lax.broadcasted_iota(jnp.int32, x.shape, 1)
    s = 1
    while s < width:
        x = x + jnp.where(lane >= s, pltpu.roll(x, s, axis=1), 0.0)
        s *= 2
    o_ref[...] = x


def _cumsum_lanes(x):
    return pl.pallas_call(
        _cumsum_kernel,
        out_shape=jax.ShapeDtypeStruct(x.shape, F32),
        name="cumsum",
    )(x)


def _conv_kernel(u_ref, halo_ref, ctx_ref, w_ref, bdw_ref, lng_ref, lnb_ref, gc_ref,
                 o_ref, ext_ref, y_ref, *, tm):
    i = pl.program_id(1)

    @pl.when(i == 0)
    def _():
        ext_ref[0:CONV_HALO, :] = ctx_ref[0]

    @pl.when(i > 0)
    def _():
        ext_ref[0:CONV_HALO, :] = halo_ref[...]

    ext_ref[CONV_HALO:CONV_HALO + tm, :] = u_ref[...]

    rows = min(64, tm)
    ch = LANES
    sub = 8
    wlen = rows + CONV_HALO
    first = CONV_HALO - CONV_STATE

    def conv_rows(r, carry):
        r0 = pl.multiple_of(r * rows, rows)
        for c in range(0, CONV_WIDTH, ch):
            acc = jnp.zeros((rows, ch), F32)
            win = ext_ref[pl.ds(r0, wlen), c:c + ch]
            for rho in range(sub):
                sh = win if rho == 0 else pltpu.roll(win, wlen - rho, axis=0)
                for a in range(wlen // sub):
                    j = sub * a + rho - first
                    if 0 <= j < CONV_KERNEL:
                        acc = acc + sh[sub * a:sub * a + rows] * w_ref[j:j + 1, c:c + ch]
            y_ref[pl.ds(r0, rows), c:c + ch] = acc
        return carry

    lax.fori_loop(0, tm // rows, conv_rows, 0)

    def norm_rows(r, carry):
        r0 = pl.multiple_of(r * rows, rows)
        y = y_ref[pl.ds(r0, rows), :] + bdw_ref[...]
        mu = jnp.mean(y, axis=-1, keepdims=True)
        yc = y - mu
        var = jnp.mean(yc * yc, axis=-1, keepdims=True)
        z = yc * lax.rsqrt(var + EPS) * lng_ref[...] + lnb_ref[...]
        s = z * jax.nn.sigmoid(z)
        ms = jnp.mean(s * s, axis=-1, keepdims=True)
        o_ref[pl.ds(r0, rows), :] = (s * lax.rsqrt(ms + EPS) * gc_ref[...]).astype(BF16)
        return carry

    lax.fori_loop(0, tm // rows, norm_rows, 0)


def _conv_module(u, ctx, w_dw, b_dw, ln_g, ln_b, gc, batch, seq, tm):
    nt = seq // tm
    hb = tm // CONV_HALO
    kernel = functools.partial(_conv_kernel, tm=tm)
    return pl.pallas_call(
        kernel,
        grid=(batch, nt),
        in_specs=[
            pl.BlockSpec((tm, CONV_WIDTH), lambda b, i: (b * nt + i, 0)),
            pl.BlockSpec((CONV_HALO, CONV_WIDTH),
                         lambda b, i: (jnp.maximum((b * nt + i) * hb - 1, 0), 0)),
            pl.BlockSpec((1, CONV_HALO, CONV_WIDTH), lambda b, i: (b, 0, 0)),
            _resident((CONV_HALO, CONV_WIDTH)),
            _resident((1, CONV_WIDTH)), _resident((1, CONV_WIDTH)),
            _resident((1, CONV_WIDTH)), _resident((1, CONV_WIDTH)),
        ],
        out_specs=pl.BlockSpec((tm, CONV_WIDTH), lambda b, i: (b * nt + i, 0)),
        out_shape=jax.ShapeDtypeStruct((batch * seq, CONV_WIDTH), BF16),
        scratch_shapes=[pltpu.VMEM((CONV_HALO + tm, CONV_WIDTH), F32),
                        pltpu.VMEM((tm, CONV_WIDTH), F32)],
        compiler_params=_params(("parallel", "arbitrary")),
        name="conv_module",
    )(u, u, ctx, w_dw, b_dw, ln_g, ln_b, gc)


def _qk(q, k):
    return lax.dot_general(q, k, (((1,), (1,)), ((), ())), preferred_element_type=F32)


AUG_TERMS = 3
QK_AHEAD = 4


def _aug_kernel(f_ref, qa_ref, ka_ref):
    f = f_ref[...] * LOG2E
    tm = f.shape[0]
    lane = lax.broadcasted_iota(jnp.int32, (tm, HEAD_DIM), 1)
    for h in range(N_HEADS):
        hs = slice(h * HEAD_DIM, (h + 1) * HEAD_DIM)
        rest = f[:, h:h + 1]
        qa = jnp.where((lane >= AUG_TERMS) & (lane < 2 * AUG_TERMS), 1.0, 0.0)
        ka = jnp.where(lane < AUG_TERMS, 1.0, 0.0)
        for t in range(AUG_TERMS):
            piece = rest.astype(BF16).astype(F32)
            rest = rest - piece
            qa = jnp.where(lane == t, piece, qa)
            ka = jnp.where(lane == AUG_TERMS + t, -piece, ka)
        qa_ref[:, hs] = qa.astype(BF16)
        ka_ref[:, hs] = ka.astype(BF16)


def _attn_aug(f_col, tm):
    n = f_col.shape[0]
    return pl.pallas_call(
        _aug_kernel,
        grid=(n // tm,),
        in_specs=[pl.BlockSpec((tm, N_HEADS), lambda i: (i, 0))],
        out_specs=(pl.BlockSpec((tm, ATTN_WIDTH), lambda i: (i, 0)),
                   pl.BlockSpec((tm, ATTN_WIDTH), lambda i: (i, 0))),
        out_shape=(jax.ShapeDtypeStruct((n, ATTN_WIDTH), BF16),
                   jax.ShapeDtypeStruct((n, ATTN_WIDTH), BF16)),
        compiler_params=_params(("parallel",)),
        name="attn_aug",
    )(f_col)


def _attn_prompt_kernel(q_ref, qa_ref, k_ref, ka_ref, vt_ref, o_ref, m_ref, l_ref, acc_ref, qt_ref,
                        sp_ref, *, tq, tk):
    i = pl.program_id(1)
    key = lax.broadcasted_iota(jnp.int32, (tk, tq), 0)
    qry = lax.broadcasted_iota(jnp.int32, (tk, tq), 1)

    m_ref[...] = jnp.full(m_ref.shape, NEG_BIG, F32)
    l_ref[...] = jnp.zeros(l_ref.shape, F32)
    acc_ref[...] = jnp.zeros(acc_ref.shape, F32)
    for h in range(N_HEADS):
        hs = slice(h * HEAD_DIM, (h + 1) * HEAD_DIM)
        qt_ref[h] = jnp.concatenate([q_ref[:, hs], qa_ref[:, hs]], axis=1).T

    def scores(ks, h):
        hs = slice(h * HEAD_DIM, (h + 1) * HEAD_DIM)
        kf = jnp.concatenate([k_ref[pl.ds(ks, tk), hs], ka_ref[pl.ds(ks, tk), hs]], axis=1)
        return _dot(kf, qt_ref[h])

    def tile_step(ks, ks_next, masked):
        pending = [sp_ref[a] for a in range(QK_AHEAD)]
        for h in range(N_HEADS):
            hs = slice(h * HEAD_DIM, (h + 1) * HEAD_DIM)
            s = pending.pop(0)
            if h + QK_AHEAD < N_HEADS:
                pending.append(scores(ks, h + QK_AHEAD))
            elif ks_next is not None:
                pending.append(scores(ks_next, h + QK_AHEAD - N_HEADS))
            if masked:
                s = jnp.where(key <= qry, s, -jnp.inf)
            m = m_ref[h]
            m_new = jnp.maximum(m, jnp.max(s, axis=0, keepdims=True))
            alpha = jnp.exp2(m - m_new)
            p = jnp.exp2(s - m_new)
            m_ref[h] = m_new
            l_ref[h] = alpha * l_ref[h] + jnp.sum(p, axis=0, keepdims=True)
            pv = _dot(vt_ref[hs, pl.ds(ks, tk)], p.astype(BF16))
            acc_ref[h] = alpha * acc_ref[h] + pv
        for a, s in enumerate(pending):
            sp_ref[a] = s

    def body(j, carry):
        tile_step(pl.multiple_of(j * tk, tk), pl.multiple_of((j + 1) * tk, tk), False)
        return carry

    for a in range(QK_AHEAD):
        sp_ref[a] = scores(0, a)
    lax.fori_loop(0, i, body, 0)
    tile_step(pl.multiple_of(i * tk, tk), None, True)
    for h in range(N_HEADS):
        o_ref[:, h * HEAD_DIM:(h + 1) * HEAD_DIM] = (acc_ref[h] / l_ref[h]).T


def _attn_prompt(q, qa, kb, ka, vbt, batch, seq, tq, tk):
    nq = seq // tq
    kernel = functools.partial(_attn_prompt_kernel, tq=tq, tk=tk)
    qblk = pl.BlockSpec((tq, ATTN_WIDTH), lambda b, i: (b * nq + i, 0))
    kblk = pl.BlockSpec((seq, ATTN_WIDTH), lambda b, i: (b, 0), pipeline_mode=pl.Buffered(1))
    vblk = pl.BlockSpec((ATTN_WIDTH, seq), lambda b, i: (0, b), pipeline_mode=pl.Buffered(1))
    return pl.pallas_call(
        kernel,
        grid=(batch, nq),
        in_specs=[qblk, qblk, kblk, kblk, vblk],
        out_specs=pl.BlockSpec((tq, ATTN_WIDTH), lambda b, i: (b * nq + i, 0)),
        out_shape=jax.ShapeDtypeStruct((batch * seq, ATTN_WIDTH), F32),
        scratch_shapes=[pltpu.VMEM((N_HEADS, 1, tq), F32), pltpu.VMEM((N_HEADS, 1, tq), F32),
                        pltpu.VMEM((N_HEADS, HEAD_DIM, tq), F32),
                        pltpu.VMEM((N_HEADS, 2 * HEAD_DIM, tq), BF16),
                        pltpu.VMEM((QK_AHEAD, tk, tq), F32)],
        compiler_params=_params(("parallel", "arbitrary")),
        name="attn_prompt",
    )(q, qa, kb, ka, vbt)


def _attn_sample_kernel(q_ref, kn_ref, vn_ref, ck_hbm, cv_hbm, fc_ref, fnrow_ref, fncol_ref,
                        o_ref, kbuf_ref, vbuf_ref, sem, *, t_new, past):
    b = pl.program_id(0)
    slot = b % 2

    def fetch(batch_idx, into):
        copies = []
        for h in range(N_HEADS):
            copies.append(pltpu.make_async_copy(ck_hbm.at[batch_idx, :, h, :], kbuf_ref.at[into, h],
                                                sem.at[into, 0]))
            copies.append(pltpu.make_async_copy(cv_hbm.at[batch_idx, :, h, :], vbuf_ref.at[into, h],
                                                sem.at[into, 1]))
        return copies

    @pl.when(b == 0)
    def _():
        for copy in fetch(0, 0):
            copy.start()

    @pl.when(b + 1 < pl.num_programs(0))
    def _():
        for copy in fetch(b + 1, 1 - slot):
            copy.start()

    for copy in fetch(b, slot):
        copy.wait()

    row = lax.broadcasted_iota(jnp.int32, (t_new, t_new), 0)
    col = lax.broadcasted_iota(jnp.int32, (t_new, t_new), 1)
    causal = col <= row
    for h in range(N_HEADS):
        hs = slice(h * HEAD_DIM, (h + 1) * HEAD_DIM)
        q = q_ref[:, hs]
        fn_q = fncol_ref[:, h:h + 1]
        fc = fc_ref[0, h:h + 1, :]
        fc_last = fc[:, past - 1:past]
        s_c = _qk(q, kbuf_ref[slot, h].astype(BF16)) + ((fc_last + fn_q) - fc) * LOG2E
        fn_k = fnrow_ref[0, h:h + 1, 0:t_new]
        s_n = _qk(q, kn_ref[:, hs]) + (fn_q - fn_k) * LOG2E
        s_n = jnp.where(causal, s_n, -jnp.inf)
        m = jnp.maximum(jnp.max(s_c, axis=-1, keepdims=True), jnp.max(s_n, axis=-1, keepdims=True))
        p_c = jnp.exp2(s_c - m)
        p_n = jnp.exp2(s_n - m)
        l = jnp.sum(p_c, axis=-1, keepdims=True) + jnp.sum(p_n, axis=-1, keepdims=True)
        acc = _dot(p_c.astype(BF16), vbuf_ref[slot, h].astype(BF16)) + _dot(p_n.astype(BF16), vn_ref[:, hs])
        o_ref[:, hs] = acc / l


def _attn_sample(q, kb, vb, cache_k, cache_v, fc_row, fn_row, fn_col, batch, t_new, past):
    kernel = functools.partial(_attn_sample_kernel, t_new=t_new, past=past)
    tok = lambda w: pl.BlockSpec((t_new, w), lambda b: (b, 0))
    return pl.pallas_call(
        kernel,
        grid=(batch,),
        in_specs=[
            tok(ATTN_WIDTH), tok(ATTN_WIDTH), tok(ATTN_WIDTH),
            pl.BlockSpec(memory_space=pl.ANY),
            pl.BlockSpec(memory_space=pl.ANY),
            pl.BlockSpec((1, N_HEADS, past), lambda b: (b, 0, 0)),
            pl.BlockSpec((1, N_HEADS, LANES), lambda b: (b, 0, 0)),
            tok(N_HEADS),
        ],
        out_specs=tok(ATTN_WIDTH),
        out_shape=jax.ShapeDtypeStruct((batch * t_new, ATTN_WIDTH), F32),
        scratch_shapes=[pltpu.VMEM((2, N_HEADS, past, HEAD_DIM), F32),
                        pltpu.VMEM((2, N_HEADS, past, HEAD_DIM), F32),
                        pltpu.SemaphoreType.DMA((2, 2))],
        compiler_params=_params(("arbitrary",)),
        name="attn_sample",
    )(q, kb, vb, cache_k, cache_v, fc_row, fn_row, fn_col)


def _pack_bf16_pair(lo, hi):
    lo_bits = lax.bitcast_convert_type(lo, jnp.uint32) >> 16
    hi_bits = lax.bitcast_convert_type(hi, jnp.uint32) & jnp.uint32(0xFFFF0000)
    return lo_bits | hi_bits


def _unpack_bf16_pair(w):
    lo = lax.bitcast_convert_type(w << 16, F32).astype(BF16)
    hi = lax.bitcast_convert_type(w & jnp.uint32(0xFFFF0000), F32).astype(BF16)
    return lo, hi


def _merge_kernel(yc_ref, ya_ref, x_ref, ga_ref, woc_ref, woa_ref, gf_ref, wrh_ref, wrl_ref, br_ref,
                  *rest, n_steps):
    tail_ref = rest[0] if len(rest) == 4 else None
    h_ref, xp_ref, lg_ref = rest[-3:]
    i = pl.program_id(0)

    @pl.when(i < n_steps)
    def _():
        ya = ya_ref[...]
        ya_n = (ya * lax.rsqrt(jnp.mean(ya * ya, axis=-1, keepdims=True) + EPS) * ga_ref[...]).astype(BF16)
        y = _dot(yc_ref[...], woc_ref[...]) + _dot(ya_n, woa_ref[...])
        h = x_ref[...] + y
        h_ref[...] = h
        xn = h * lax.rsqrt(jnp.mean(h * h, axis=-1, keepdims=True) + EPS) * gf_ref[...]
        xn_hi = xn.astype(BF16)
        xn_hi32 = xn_hi.astype(F32)
        xn_lo = (xn - xn_hi32).astype(BF16)
        lg_ref[...] = (_dot(xn_hi, wrh_ref[...]) + _dot(xn_lo, wrh_ref[...]) + _dot(xn_hi, wrl_ref[...])
                       + br_ref[...])
        half = D_MODEL // 2
        xp_ref[...] = _pack_bf16_pair(xn_hi32[:, :half], xn_hi32[:, half:])

    if tail_ref is not None:
        @pl.when(i == n_steps)
        def _():
            xp_ref[...] = tail_ref[...]


def _merge(yc_n, ya, x, ga, w_out, g_ffn, wr_hi, wr_lo, b_r, tm, tail=None):
    n = x.shape[0]
    n_steps = n // tm
    extra = 0 if tail is None else 1
    row = lambda w: pl.BlockSpec((tm, w), lambda i: (jnp.minimum(i, n_steps - 1), 0))
    in_specs = [row(CONV_WIDTH), row(ATTN_WIDTH), row(D_MODEL), _resident((1, ATTN_WIDTH)),
                _resident((CONV_WIDTH, D_MODEL), (0, 0)), _resident((ATTN_WIDTH, D_MODEL), (1, 0)),
                _resident((1, D_MODEL)),
                _resident((D_MODEL, LANES)), _resident((D_MODEL, LANES)), _resident((1, LANES))]
    args = [yc_n, ya, x, ga, w_out, w_out, g_ffn, wr_hi, wr_lo, b_r]
    if tail is not None:
        assert tail.shape == (tm, D_MODEL // 2)
        in_specs.append(_resident(tail.shape))
        args.append(tail)
    return pl.pallas_call(
        functools.partial(_merge_kernel, n_steps=n_steps),
        grid=(n_steps + extra,),
        in_specs=in_specs,
        out_specs=(row(D_MODEL), pl.BlockSpec((tm, D_MODEL // 2), lambda i: (i, 0)), row(LANES)),
        out_shape=(jax.ShapeDtypeStruct((n, D_MODEL), F32),
                   jax.ShapeDtypeStruct((n + extra * tm, D_MODEL // 2), jnp.uint32),
                   jax.ShapeDtypeStruct((n, LANES), F32)),
        compiler_params=_params(("arbitrary",)),
        name="merge_out",
    )(*args)


def _route_kernel(lg_ref, info_ref, infot_ref, cnt_ref, carry_ref, *, tm):
    step = pl.program_id(0)

    @pl.when(step == 0)
    def _():
        carry_ref[...] = jnp.zeros_like(carry_ref)

    lg = lg_ref[...]
    lane = lax.broadcasted_iota(jnp.int32, lg.shape, 1)
    lanef = lane.astype(F32)
    big = jnp.float32(1e9)
    rmax = lambda v: jnp.max(v, axis=-1, keepdims=True)
    rmin = lambda v: jnp.min(v, axis=-1, keepdims=True)
    rsum = lambda v: jnp.sum(v, axis=-1, keepdims=True)

    is_g = (lane >= N_EXPERTS) & (lane < N_EXPERTS + N_GROUPS)
    gl = jnp.where(is_g, lg, NEG_BIG)
    gmax = rmax(gl)
    gsum = rsum(jnp.where(is_g, jnp.exp(gl - gmax), 0.0))
    pg_star = 1.0 / gsum
    g_idx = rmin(jnp.where(is_g & (gl == gmax), lanef - N_EXPERTS, big))

    e_lo = g_idx * EXPERTS_PER_GROUP
    is_e = (lanef >= e_lo) & (lanef < e_lo + EXPERTS_PER_GROUP)
    el = jnp.where(is_e, lg, NEG_BIG)
    m1 = rmax(el)
    i1 = rmin(jnp.where(is_e & (el == m1), lanef, big))
    sel1 = lanef == i1
    el2 = jnp.where(sel1, NEG_BIG, el)
    m2 = rmax(el2)
    i2 = rmin(jnp.where(is_e & (el2 == m2) & jnp.logical_not(sel1), lanef, big))
    sel2 = lanef == i2
    z = rsum(jnp.where(is_e, jnp.exp(el - m1), 0.0))
    p1 = 1.0 / z
    p2 = jnp.exp(m2 - m1) / z
    gate1 = pg_star * p1 / (p1 + p2)
    gate2 = pg_star * p2 / (p1 + p2)

    onehot = jnp.where(sel1 | sel2, 1.0, 0.0)
    r = lax.broadcasted_iota(jnp.int32, (tm, tm), 0)
    c = lax.broadcasted_iota(jnp.int32, (tm, tm), 1)
    tri = jnp.where(c < r, 1.0, 0.0).astype(BF16)
    before = _dot(tri, onehot.astype(BF16)) + carry_ref[...]
    rank1 = rsum(jnp.where(sel1, before, 0.0))
    rank2 = rsum(jnp.where(sel2, before, 0.0))
    carry_ref[...] = carry_ref[...] + jnp.sum(onehot, axis=0, keepdims=True)
    cnt_ref[...] = carry_ref[...]

    info = jnp.zeros_like(lg)
    for k, val in enumerate((i1, i2, rank1, rank2, gate1, gate2)):
        info = jnp.where(lane == k, val, info)
    info_ref[...] = info
    infot_ref[...] = info.T[0:8, :]


def _route(logits, tm):
    n = logits.shape[0]
    kernel = functools.partial(_route_kernel, tm=tm)
    return pl.pallas_call(
        kernel,
        grid=(n // tm,),
        in_specs=[pl.BlockSpec((tm, LANES), lambda i: (i, 0))],
        out_specs=(pl.BlockSpec((tm, LANES), lambda i: (i, 0)),
                   pl.BlockSpec((8, tm), lambda i: (0, i)),
                   pl.BlockSpec((1, LANES), lambda i: (0, 0))),
        out_shape=(jax.ShapeDtypeStruct((n, LANES), F32), jax.ShapeDtypeStruct((8, n), F32),
                   jax.ShapeDtypeStruct((1, LANES), F32)),
        scratch_shapes=[pltpu.VMEM((1, LANES), F32)],
        compiler_params=_params(("arbitrary",)),
        name="route",
    )(logits)


def _slot_tokens_kernel(dest_ref, tok_ref):
    n_all = dest_ref.shape[0] // 2

    def clear(s, carry):
        tok_ref[s] = 0
        return carry

    def place(t, carry):
        tok_ref[dest_ref[t]] = t
        tok_ref[dest_ref[n_all + t]] = t
        return carry

    lax.fori_loop(0, tok_ref.shape[0], clear, 0, unroll=8)
    lax.fori_loop(0, n_all, place, 0, unroll=8)


def _slot_tokens(dest_flat, n_slots):
    grid_spec = pltpu.PrefetchScalarGridSpec(
        num_scalar_prefetch=1, grid=(1,), in_specs=[],
        out_specs=pl.BlockSpec(memory_space=pltpu.SMEM))
    return pl.pallas_call(
        _slot_tokens_kernel,
        grid_spec=grid_spec,
        out_shape=jax.ShapeDtypeStruct((n_slots,), jnp.int32),
        compiler_params=pltpu.CompilerParams(dimension_semantics=("arbitrary",)),
        name="slot_tokens",
    )(dest_flat)


CAST_CHUNK_ELEMS = 64 * 1024
WEIGHT_DMA_SPLIT = 4
WEIGHT_DMA_PRIORITY = 1


def _expert_kernel(be_ref, nu_ref, nx_ref, tok_ref, x_hbm, w1_hbm, w3_hbm, w2_hbm, y_ref,
                   xbuf_ref, w1f_ref, w3f_ref, w2f_ref, w1b_ref, w3b_ref, w2b_ref, sem, gsem):
    b = pl.program_id(0)
    n_used = nu_ref[0]
    active = b < n_used
    slot = b % 2
    new_expert = jnp.logical_or(b == 0, be_ref[b] != be_ref[jnp.maximum(b - 1, 0)])

    def fetch(e):
        copies = []
        for k, (src, dst) in enumerate(((w1_hbm, w1f_ref), (w3_hbm, w3f_ref), (w2_hbm, w2f_ref))):
            slab = dst.shape[0] // WEIGHT_DMA_SPLIT
            for c in range(WEIGHT_DMA_SPLIT):
                rows = pl.ds(c * slab, slab)
                copies.append(pltpu.make_async_copy(src.at[e, rows], dst.at[rows], sem.at[k]))
        return copies

    def row_copy(blk, into, r):
        return pltpu.make_async_copy(x_hbm.at[pl.ds(tok_ref[blk * MOE_BLOCK + r], 1)],
                                     xbuf_ref.at[into, pl.ds(r, 1)], gsem.at[into])

    def drain_rows(into):
        def body(r, carry):
            pltpu.make_async_copy(x_hbm.at[pl.ds(0, 1)], xbuf_ref.at[into, pl.ds(r, 1)],
                                  gsem.at[into]).wait()
            return carry
        lax.fori_loop(0, MOE_BLOCK, body, 0, unroll=8)

    @pl.when(b == 0)
    def _():
        for copy in fetch(be_ref[0]):
            copy.start(priority=WEIGHT_DMA_PRIORITY)

        def body(r, carry):
            row_copy(0, 0, r).start()
            return carry
        lax.fori_loop(0, MOE_BLOCK, body, 0, unroll=8)

    @pl.when(jnp.logical_and(active, new_expert))
    def _():
        for copy in fetch(be_ref[b]):
            copy.wait()
        for src, dst in ((w1f_ref, w1b_ref), (w3f_ref, w3b_ref), (w2f_ref, w2b_ref)):
            rows, cols = src.shape
            chunk = CAST_CHUNK_ELEMS // cols

            def cast_rows(c, carry, src=src, dst=dst, chunk=chunk):
                r0 = pl.multiple_of(c * chunk, chunk)
                dst[pl.ds(r0, chunk), :] = src[pl.ds(r0, chunk), :].astype(BF16)
                return carry

            lax.fori_loop(0, rows // chunk, cast_rows, 0)

        @pl.when(nx_ref[b] >= 0)
        def _():
            for copy in fetch(nx_ref[b]):
                copy.start(priority=WEIGHT_DMA_PRIORITY)

    @pl.when(active)
    def _():
        drain_rows(slot)
        lo, hi = _unpack_bf16_pair(xbuf_ref[slot])
        half = D_MODEL // 2
        nxt = jnp.minimum(b + 1, n_used - 1)
        for r in range(0, MOE_BLOCK // 2):
            row_copy(nxt, 1 - slot, r).start()
        h1 = _dot(lo, w1b_ref[:half, :]) + _dot(hi, w1b_ref[half:, :])
        for r in range(MOE_BLOCK // 2, MOE_BLOCK):
            row_copy(nxt, 1 - slot, r).start()
        h3 = _dot(lo, w3b_ref[:half, :]) + _dot(hi, w3b_ref[half:, :])
        h = (h1 * jax.nn.sigmoid(h1) * h3).astype(BF16)
        y_ref[...] = _dot(h, w2b_ref[...])

        @pl.when(b == n_used - 1)
        def _():
            drain_rows(1 - slot)

    @pl.when(jnp.logical_not(active))
    def _():
        y_ref[...] = jnp.zeros_like(y_ref)


def _experts(block_e, n_used, next_e, slot_tok, x_packed, w1, w3, w2):
    n_slots = slot_tok.shape[0]
    nb = n_slots // MOE_BLOCK
    hbm = pl.BlockSpec(memory_space=pl.ANY)
    grid_spec = pltpu.PrefetchScalarGridSpec(
        num_scalar_prefetch=4,
        grid=(nb,),
        in_specs=[hbm, hbm, hbm, hbm],
        out_specs=pl.BlockSpec((MOE_BLOCK, D_MODEL), lambda b, be, nu, nx, tok: (b, 0)),
        scratch_shapes=[pltpu.VMEM((2, MOE_BLOCK, D_MODEL // 2), jnp.uint32),
                        pltpu.VMEM((D_MODEL, D_EXPERT), F32), pltpu.VMEM((D_MODEL, D_EXPERT), F32),
                        pltpu.VMEM((D_EXPERT, D_MODEL), F32),
                        pltpu.VMEM((D_MODEL, D_EXPERT), BF16), pltpu.VMEM((D_MODEL, D_EXPERT), BF16),
                        pltpu.VMEM((D_EXPERT, D_MODEL), BF16),
                        pltpu.SemaphoreType.DMA((3,)), pltpu.SemaphoreType.DMA((2,))],
    )
    return pl.pallas_call(
        _expert_kernel,
        grid_spec=grid_spec,
        out_shape=jax.ShapeDtypeStruct((n_slots, D_MODEL), F32),
        compiler_params=_params(("arbitrary",)),
        name="experts",
    )(block_e, n_used, next_e, slot_tok, x_packed, w1, w3, w2)


PLE_SPLIT = 4


def _combine_ple_kernel(dest_ref, h_ref, info_ref, p_ref, g_ref, *rest, tm, t_off):
    wpg_refs = rest[:PLE_SPLIT]
    wpp_ref, yb_hbm, o_ref, buf_ref, sem = rest[PLE_SPLIT:]
    i = pl.program_id(0)
    last = pl.num_programs(0) - 1
    n_all = dest_ref.shape[0] // 2
    slot = i % 2

    def row_copy(step, into, r, k):
        t = t_off + step * tm + r
        return pltpu.make_async_copy(yb_hbm.at[pl.ds(dest_ref[k * n_all + t], 1)],
                                     buf_ref.at[into, k, pl.ds(r, 1)], sem.at[into])

    def drain(into):
        def body(r, carry):
            for k in range(2):
                pltpu.make_async_copy(yb_hbm.at[pl.ds(0, 1)], buf_ref.at[into, k, pl.ds(r, 1)],
                                      sem.at[into]).wait()
            return carry
        lax.fori_loop(0, tm, body, 0, unroll=8)

    @pl.when(i == 0)
    def _():
        def body(r, carry):
            row_copy(0, 0, r, 0).start()
            row_copy(0, 0, r, 1).start()
            return carry
        lax.fori_loop(0, tm, body, 0, unroll=8)

    drain(slot)
    h2 = h_ref[...] + (info_ref[:, 4:5] * buf_ref[slot, 0] + info_ref[:, 5:6] * buf_ref[slot, 1])
    hn = (h2 * lax.rsqrt(jnp.mean(h2 * h2, axis=-1, keepdims=True) + EPS) * g_ref[...]).astype(BF16)
    pb = p_ref[...].astype(BF16)

    nxt = jnp.minimum(i + 1, last)
    ch = D_MODEL // PLE_SPLIT
    per = tm // PLE_SPLIT
    for k in range(PLE_SPLIT):
        for r in range(k * per, (k + 1) * per):
            row_copy(nxt, 1 - slot, r, 0).start(priority=0)
            row_copy(nxt, 1 - slot, r, 1).start(priority=1)
        cs = slice(k * ch, (k + 1) * ch)
        gate = jax.nn.sigmoid(_dot(hn, wpg_refs[k][...]))
        o_ref[:, cs] = h2[:, cs] + gate * _dot(pb, wpp_ref[:, cs])

    @pl.when(i == last)
    def _():
        drain(1 - slot)


def _combine_ple(dest_flat, h, info, yb, p, g_ple, w_pg, w_pp, t_off, tm):
    n = h.shape[0]
    kernel = functools.partial(_combine_ple_kernel, tm=tm, t_off=t_off)
    ob = t_off // tm
    row = lambda w: pl.BlockSpec((tm, w), lambda i, d: (i, 0))
    grid_spec = pltpu.PrefetchScalarGridSpec(
        num_scalar_prefetch=1,
        grid=(n // tm,),
        in_specs=[row(D_MODEL), pl.BlockSpec((tm, LANES), lambda i, d: (i + ob, 0)), row(PLE_DIM),
                  _resident((1, D_MODEL))]
        + [_resident((D_MODEL, D_MODEL // PLE_SPLIT), (0, k)) for k in range(PLE_SPLIT)]
        + [_resident((PLE_DIM, D_MODEL)), pl.BlockSpec(memory_space=pl.ANY)],
        out_specs=row(D_MODEL),
        scratch_shapes=[pltpu.VMEM((2, 2, tm, D_MODEL), F32), pltpu.SemaphoreType.DMA((2,))],
    )
    return pl.pallas_call(
        kernel,
        grid_spec=grid_spec,
        out_shape=jax.ShapeDtypeStruct((n, D_MODEL), F32),
        compiler_params=_params(("arbitrary",)),
        name="combine_ple",
    )(dest_flat, h, info, p, g_ple, *([w_pg] * PLE_SPLIT), w_pp, yb)


def _mixer_tokens(x2d, wts, tm):
    return _in_proj(x2d, wts["g_mix"], wts["w_main"], wts["w_f"], wts["b_f"],
                    wts["q_gain"], wts["k_gain"], tm)


def kernel(x_prompt, x_sample, cache_k, cache_v, cache_logf, cache_conv, p_prompt, p_sample,
           g_mix, w_in, b_f, q_gain, k_gain, w_dw, b_dw, ln_g, ln_b, gc, ga, w_out,
           g_ffn, w_router_g, b_router_g, w_router_e, b_router_e, w1, w3, w2,
           g_ple, w_pg, w_pp):
    batch, seq, _ = x_prompt.shape
    dec_batch, t_new, _ = x_sample.shape
    past = cache_k.shape[2]
    n_p = batch * seq
    n_s = dec_batch * t_new
    n_all = n_p + n_s
    tm = 256
    li = 0

    w_in_l = w_in[li]
    pad_lanes = lambda a: jnp.pad(a, ((0, 0), (0, LANES - a.shape[1])))
    row2d = lambda a: a.reshape(1, -1)
    wts = {
        "g_mix": row2d(g_mix[li]),
        "w_main": _cast_columns_bf16(w_in[li:li + 1], MAIN_COLS, 512),
        "w_f": pad_lanes(w_in_l[:, MAIN_COLS:]).astype(BF16),
        "b_f": pad_lanes(row2d(b_f[li])),
        "q_gain": row2d(q_gain[li]),
        "k_gain": row2d(k_gain[li]),
    }
    w_dw_p = jnp.pad(w_dw[li], ((0, CONV_HALO - CONV_KERNEL), (0, 0)))
    w_out_b = w_out[li].astype(BF16)
    w_r = pad_lanes(jnp.concatenate([w_router_e[li], w_router_g[li]], axis=1))
    wr_hi = w_r.astype(BF16)
    wr_lo = (w_r - wr_hi.astype(F32)).astype(BF16)
    b_r = pad_lanes(row2d(jnp.concatenate([b_router_e[li], b_router_g[li]])))
    w_pg_b, w_pp_b = w_pg[li].astype(BF16), w_pp[li].astype(BF16)

    xp = x_prompt.reshape(n_p, D_MODEL)
    u_p, q_p, k_p, kb_p, v_p, _, vbt_p, lf_p = _mixer_tokens(xp, wts, tm)
    lf_p_row = lf_p.reshape(batch, seq, N_HEADS).transpose(0, 2, 1)
    f_p_row = _cumsum_lanes(lf_p_row.reshape(batch * N_HEADS, seq)).reshape(batch, N_HEADS, seq)
    f_p_col = f_p_row.transpose(0, 2, 1).reshape(n_p, N_HEADS)
    ctx_p = jnp.zeros((batch, CONV_HALO, CONV_WIDTH), F32)
    conv_args = (w_dw_p, row2d(b_dw[li]), row2d(ln_g[li]), row2d(ln_b[li]), row2d(gc[li]))
    yc_p = _conv_module(u_p, ctx_p, *conv_args, batch, seq, tm)
    qa_p, ka_p = _attn_aug(f_p_col, tm)
    ya_p = _attn_prompt(q_p, qa_p, kb_p, ka_p, vbt_p, batch, seq, 256, 256)
    merge_args = (row2d(ga[li]), w_out_b, row2d(g_ffn[li]), wr_hi, wr_lo, b_r)

    xs_ = x_sample.reshape(n_s, D_MODEL)
    u_s, q_s, k_s, kb_s, v_s, vb_s, _, lf_s = _mixer_tokens(xs_, wts, tm)
    clf_row = cache_logf[li].transpose(0, 2, 1).reshape(dec_batch * N_HEADS, past)
    fc_row = _cumsum_lanes(clf_row).reshape(dec_batch, N_HEADS, past)
    lf_s_row = lf_s.reshape(dec_batch, t_new, N_HEADS).transpose(0, 2, 1).reshape(dec_batch * N_HEADS, t_new)
    fn_row = _cumsum_lanes(jnp.pad(lf_s_row, ((0, 0), (0, LANES - t_new)))).reshape(dec_batch, N_HEADS, LANES)
    fn_col = fn_row[:, :, :t_new].transpose(0, 2, 1).reshape(n_s, N_HEADS)
    ctx_s = jnp.pad(cache_conv[li], ((0, 0), (CONV_HALO - CONV_STATE, 0), (0, 0)))
    yc_s = _conv_module(u_s, ctx_s, *conv_args, dec_batch, t_new, t_new)
    ya_s = _attn_sample(q_s, kb_s, vb_s,
                        cache_k[li], cache_v[li],
                        fc_row, fn_row, fn_col, dec_batch, t_new, past)
    h_s, xpk_s, lg_s = _merge(yc_s, ya_s, xs_, *merge_args, tm)
    h_p, xpk, lg_p = _merge(yc_p, ya_p, xp, *merge_args, tm, tail=xpk_s)

    info, info_t, counts = _route(jnp.concatenate([lg_p, lg_s], axis=0), tm)
    counts = counts[0, :N_EXPERTS].astype(jnp.int32)
    bcounts = (counts + MOE_BLOCK - 1) // MOE_BLOCK
    bends = jnp.cumsum(bcounts)
    pstarts = (bends - bcounts) * MOE_BLOCK
    n_rows = n_all * 2
    nb = -(-(n_rows + N_EXPERTS * (MOE_BLOCK - 1)) // MOE_BLOCK)
    e_idx = info_t[0:2].astype(jnp.int32)
    expert_ids = jnp.arange(N_EXPERTS, dtype=jnp.int32)[:, None, None]
    seg_start = jnp.sum(jnp.where(e_idx[None] == expert_ids, pstarts[:, None, None], 0), axis=0)
    dest = (seg_start + info_t[2:4].astype(jnp.int32)).reshape(n_rows)
    block_e = jnp.minimum(jnp.sum(bends[None, :] <= jnp.arange(nb, dtype=jnp.int32)[:, None], axis=1),
                          N_EXPERTS - 1).astype(jnp.int32)
    n_used = bends[N_EXPERTS - 1:].astype(jnp.int32)
    slot_tok = _slot_tokens(dest, nb * MOE_BLOCK)
    seg_end = jnp.sum(jnp.where(block_e[None, :] == expert_ids[:, :, 0], bends[:, None], 0), axis=0)
    next_e = jnp.where(seg_end < n_used[0], block_e[jnp.minimum(seg_end, nb - 1)], -1).astype(jnp.int32)
    yb = _experts(block_e, n_used, next_e, slot_tok, xpk, w1[li], w3[li], w2[li])

    ple_args = (row2d(g_ple[li]), w_pg_b, w_pp_b)
    y_p = _combine_ple(dest, h_p, info, yb, p_prompt[li].reshape(n_p, PLE_DIM), *ple_args, 0, tm)
    y_s = _combine_ple(dest, h_s, info, yb, p_sample[li].reshape(n_s, PLE_DIM), *ple_args, n_p, tm)

    heads = lambda a, b, t: a.reshape(1, b, t, N_HEADS, HEAD_DIM)
    return (
        y_p.reshape(batch, seq, D_MODEL),
        y_s.reshape(dec_batch, t_new, D_MODEL),
        heads(k_p, batch, seq), heads(v_p, batch, seq),
        lf_p.reshape(1, batch, seq, N_HEADS),
        u_p.reshape(batch, seq, CONV_WIDTH)[None, :, seq - CONV_STATE:, :],
        heads(k_s, dec_batch, t_new), heads(v_s, dec_batch, t_new),
        lf_s.reshape(1, dec_batch, t_new, N_HEADS),
        u_s.reshape(dec_batch, t_new, CONV_WIDTH)[None, :, t_new - CONV_STATE:, :],
    )
```

```python
import functools

import jax
import jax.numpy as jnp
from jax import lax
from jax.experimental import pallas as pl
from jax.experimental.pallas import tpu as pltpu

D_MODEL = 2048
CONV_WIDTH = 1024
ATTN_WIDTH = 1024
HEAD_DIM = 128
N_HEADS = 8
CONV_KERNEL = 31
CONV_STATE = CONV_KERNEL - 1
N_GROUPS = 4
EXPERTS_PER_GROUP = 8
N_EXPERTS = 32
D_EXPERT = 512
PLE_DIM = 256
MOE_BLOCK = 128
EPS = 1e-6
MAIN_COLS = 2 * CONV_WIDTH + 3 * ATTN_WIDTH

LANES = 128
CONV_HALO = 32
VMEM_LIMIT = 56 * 1024 * 1024

F32 = jnp.float32
BF16 = jnp.bfloat16
NEG_BIG = -1e30
LOG2E = 1.4426950408889634


def _dot(a, b):
    return jnp.dot(a, b, preferred_element_type=F32)


def _params(sem):
    return pltpu.CompilerParams(dimension_semantics=sem, vmem_limit_bytes=VMEM_LIMIT)


def _resident(shape, index=None):
    index = (0,) * len(shape) if index is None else index
    return pl.BlockSpec(shape, lambda *_: index, pipeline_mode=pl.Buffered(1))


def _inproj_kernel(x_ref, g_ref, wval_ref, wgate_ref, wq_ref, wk_ref, wv_ref, wf_ref, bf_ref, qg_ref, kg_ref,
                   u_ref, q_ref, k_ref, kb_ref, v_ref, vb_ref, vbt_ref, lf_ref):
    x = x_ref[...]
    ms = jnp.mean(x * x, axis=-1, keepdims=True)
    a = (x * lax.rsqrt(ms + EPS) * g_ref[...]).astype(BF16)

    ch = 256
    for c in range(0, CONV_WIDTH, ch):
        val = _dot(a, wval_ref[:, c:c + ch])
        gate = _dot(a, wgate_ref[:, c:c + ch])
        u_ref[:, c:c + ch] = val * jax.nn.sigmoid(gate)

    def head_norm(z, gain):
        return z * lax.rsqrt(jnp.mean(z * z, axis=-1, keepdims=True) + EPS) * gain

    scale = LOG2E * HEAD_DIM ** -0.5
    for c in range(0, ATTN_WIDTH, ch):
        zq = _dot(a, wq_ref[:, c:c + ch])
        zk = _dot(a, wk_ref[:, c:c + ch])
        zv = _dot(a, wv_ref[:, c:c + ch])
        for s in range(0, ch, HEAD_DIM):
            qn = head_norm(zq[:, s:s + HEAD_DIM], qg_ref[...])
            kn = head_norm(zk[:, s:s + HEAD_DIM], kg_ref[...])
            q_ref[:, c + s:c + s + HEAD_DIM] = (qn * scale).astype(BF16)
            k_ref[:, c + s:c + s + HEAD_DIM] = kn
            kb_ref[:, c + s:c + s + HEAD_DIM] = kn.astype(BF16)
        v_ref[:, c:c + ch] = zv
        vb_ref[:, c:c + ch] = zv.astype(BF16)
        vbt_ref[c:c + ch, :] = zv.T.astype(BF16)

    f = _dot(a, wf_ref[...]) + bf_ref[...]
    lf = jnp.minimum(f, 0.0) - jnp.log1p(jnp.exp(-jnp.abs(f)))
    lf_ref[...] = lf[:, :N_HEADS]


def _in_proj(x, g_mix, w_parts, w_f, b_f, q_gain, k_gain, tm):
    n = x.shape[0]
    row = lambda w: pl.BlockSpec((tm, w), lambda i: (i, 0))
    out_shape = (
        jax.ShapeDtypeStruct((n, CONV_WIDTH), F32),
        jax.ShapeDtypeStruct((n, ATTN_WIDTH), BF16),
        jax.ShapeDtypeStruct((n, ATTN_WIDTH), F32),
        jax.ShapeDtypeStruct((n, ATTN_WIDTH), BF16),
        jax.ShapeDtypeStruct((n, ATTN_WIDTH), F32),
        jax.ShapeDtypeStruct((n, ATTN_WIDTH), BF16),
        jax.ShapeDtypeStruct((ATTN_WIDTH, n), BF16),
        jax.ShapeDtypeStruct((n, N_HEADS), F32),
    )
    return pl.pallas_call(
        _inproj_kernel,
        grid=(n // tm,),
        in_specs=[row(D_MODEL), _resident((1, D_MODEL))] + [_resident((D_MODEL, CONV_WIDTH))] * 5 + [
                  _resident((D_MODEL, LANES)), _resident((1, LANES)),
                  _resident((1, HEAD_DIM)), _resident((1, HEAD_DIM))],
        out_specs=(row(CONV_WIDTH), row(ATTN_WIDTH), row(ATTN_WIDTH), row(ATTN_WIDTH),
                   row(ATTN_WIDTH), row(ATTN_WIDTH),
                   pl.BlockSpec((ATTN_WIDTH, tm), lambda i: (0, i)), row(N_HEADS)),
        out_shape=out_shape,
        compiler_params=_params(("parallel",)),
        name="in_proj",
    )(x, g_mix, *w_parts, w_f, b_f, q_gain, k_gain)


def _cumsum_kernel(x_ref, o_ref):
    x = x_ref[...]
    width = x.shape[1]
    lane = lax.broadcasted_iota(jnp.int32, x.shape, 1)
    s = 1
    while s < width:
        x = x + jnp.where(lane >= s, pltpu.roll(x, s, axis=1), 0.0)
        s *= 2
    o_ref[...] = x


def _cumsum_lanes(x):
    return pl.pallas_call(
        _cumsum_kernel,
        out_shape=jax.ShapeDtypeStruct(x.shape, F32),
        name="cumsum",
    )(x)


def _conv_kernel(u_ref, halo_ref, ctx_ref, w_ref, bdw_ref, lng_ref, lnb_ref, gc_ref,
                 o_ref, ext_ref, y_ref, *, tm):
    i = pl.program_id(1)

    @pl.when(i == 0)
    def _():
        ext_ref[0:CONV_HALO, :] = ctx_ref[0]

    @pl.when(i > 0)
    def _():
        ext_ref[0:CONV_HALO, :] = halo_ref[...]

    ext_ref[CONV_HALO:CONV_HALO + tm, :] = u_ref[...]

    rows = min(64, tm)
    ch = LANES
    sub = 8
    wlen = rows + CONV_HALO
    first = CONV_HALO - CONV_STATE

    def conv_rows(r, carry):
        r0 = pl.multiple_of(r * rows, rows)
        for c in range(0, CONV_WIDTH, ch):
            acc = jnp.zeros((rows, ch), F32)
            win = ext_ref[pl.ds(r0, wlen), c:c + ch]
            for rho in range(sub):
                sh = win if rho == 0 else pltpu.roll(win, wlen - rho, axis=0)
                for a in range(wlen // sub):
                    j = sub * a + rho - first
                    if 0 <= j < CONV_KERNEL:
                        acc = acc + sh[sub * a:sub * a + rows] * w_ref[j:j + 1, c:c + ch]
            y_ref[pl.ds(r0, rows), c:c + ch] = acc
        return carry

    lax.fori_loop(0, tm // rows, conv_rows, 0)

    def norm_rows(r, carry):
        r0 = pl.multiple_of(r * rows, rows)
        y = y_ref[pl.ds(r0, rows), :] + bdw_ref[...]
        mu = jnp.mean(y, axis=-1, keepdims=True)
        yc = y - mu
        var = jnp.mean(yc * yc, axis=-1, keepdims=True)
        z = yc * lax.rsqrt(var + EPS) * lng_ref[...] + lnb_ref[...]
        s = z * jax.nn.sigmoid(z)
        ms = jnp.mean(s * s, axis=-1, keepdims=True)
        o_ref[pl.ds(r0, rows), :] = (s * lax.rsqrt(ms + EPS) * gc_ref[...]).astype(BF16)
        return carry

    lax.fori_loop(0, tm // rows, norm_rows, 0)


def _conv_module(u, ctx, w_dw, b_dw, ln_g, ln_b, gc, batch, seq, tm):
    nt = seq // tm
    hb = tm // CONV_HALO
    kernel = functools.partial(_conv_kernel, tm=tm)
    return pl.pallas_call(
        kernel,
        grid=(batch, nt),
        in_specs=[
            pl.BlockSpec((tm, CONV_WIDTH), lambda b, i: (b * nt + i, 0)),
            pl.BlockSpec((CONV_HALO, CONV_WIDTH),
                         lambda b, i: (jnp.maximum((b * nt + i) * hb - 1, 0), 0)),
            pl.BlockSpec((1, CONV_HALO, CONV_WIDTH), lambda b, i: (b, 0, 0)),
            _resident((CONV_HALO, CONV_WIDTH)),
            _resident((1, CONV_WIDTH)), _resident((1, CONV_WIDTH)),
            _resident((1, CONV_WIDTH)), _resident((1, CONV_WIDTH)),
        ],
        out_specs=pl.BlockSpec((tm, CONV_WIDTH), lambda b, i: (b * nt + i, 0)),
        out_shape=jax.ShapeDtypeStruct((batch * seq, CONV_WIDTH), BF16),
        scratch_shapes=[pltpu.VMEM((CONV_HALO + tm, CONV_WIDTH), F32),
                        pltpu.VMEM((tm, CONV_WIDTH), F32)],
        compiler_params=_params(("parallel", "arbitrary")),
        name="conv_module",
    )(u, u, ctx, w_dw, b_dw, ln_g, ln_b, gc)


def _qk(q, k):
    return lax.dot_general(q, k, (((1,), (1,)), ((), ())), preferred_element_type=F32)


AUG_TERMS = 3
QK_AHEAD = 4


def _aug_kernel(f_ref, qa_ref, ka_ref):
    f = f_ref[...] * LOG2E
    tm = f.shape[0]
    lane = lax.broadcasted_iota(jnp.int32, (tm, HEAD_DIM), 1)
    for h in range(N_HEADS):
        hs = slice(h * HEAD_DIM, (h + 1) * HEAD_DIM)
        rest = f[:, h:h + 1]
        qa = jnp.where((lane >= AUG_TERMS) & (lane < 2 * AUG_TERMS), 1.0, 0.0)
        ka = jnp.where(lane < AUG_TERMS, 1.0, 0.0)
        for t in range(AUG_TERMS):
            piece = rest.astype(BF16).astype(F32)
            rest = rest - piece
            qa = jnp.where(lane == t, piece, qa)
            ka = jnp.where(lane == AUG_TERMS + t, -piece, ka)
        qa_ref[:, hs] = qa.astype(BF16)
        ka_ref[:, hs] = ka.astype(BF16)


def _attn_aug(f_col, tm):
    n = f_col.shape[0]
    return pl.pallas_call(
        _aug_kernel,
        grid=(n // tm,),
        in_specs=[pl.BlockSpec((tm, N_HEADS), lambda i: (i, 0))],
        out_specs=(pl.BlockSpec((tm, ATTN_WIDTH), lambda i: (i, 0)),
                   pl.BlockSpec((tm, ATTN_WIDTH), lambda i: (i, 0))),
        out_shape=(jax.ShapeDtypeStruct((n, ATTN_WIDTH), BF16),
                   jax.ShapeDtypeStruct((n, ATTN_WIDTH), BF16)),
        compiler_params=_params(("parallel",)),
        name="attn_aug",
    )(f_col)


def _attn_prompt_kernel(q_ref, qa_ref, k_ref, ka_ref, vt_ref, o_ref, m_ref, l_ref, acc_ref, qt_ref,
                        sp_ref, *, tq, tk):
    i = pl.program_id(1)
    key = lax.broadcasted_iota(jnp.int32, (tk, tq), 0)
    qry = lax.broadcasted_iota(jnp.int32, (tk, tq), 1)

    m_ref[...] = jnp.full(m_ref.shape, NEG_BIG, F32)
    l_ref[...] = jnp.zeros(l_ref.shape, F32)
    acc_ref[...] = jnp.zeros(acc_ref.shape, F32)
    for h in range(N_HEADS):
        hs = slice(h * HEAD_DIM, (h + 1) * HEAD_DIM)
        qt_ref[h] = jnp.concatenate([q_ref[:, hs], qa_ref[:, hs]], axis=1).T

    def scores(ks, h):
        hs = slice(h * HEAD_DIM, (h + 1) * HEAD_DIM)
        kf = jnp.concatenate([k_ref[pl.ds(ks, tk), hs], ka_ref[pl.ds(ks, tk), hs]], axis=1)
        return _dot(kf, qt_ref[h])

    def tile_step(ks, ks_next, masked):
        pending = [sp_ref[a] for a in range(QK_AHEAD)]
        for h in range(N_HEADS):
            hs = slice(h * HEAD_DIM, (h + 1) * HEAD_DIM)
            s = pending.pop(0)
            if h + QK_AHEAD < N_HEADS:
                pending.append(scores(ks, h + QK_AHEAD))
            elif ks_next is not None:
                pending.append(scores(ks_next, h + QK_AHEAD - N_HEADS))
            if masked:
                s = jnp.where(key <= qry, s, -jnp.inf)
            m = m_ref[h]
            m_new = jnp.maximum(m, jnp.max(s, axis=0, keepdims=True))
            alpha = jnp.exp2(m - m_new)
            p = jnp.exp2(s - m_new)
            m_ref[h] = m_new
            l_ref[h] = alpha * l_ref[h] + jnp.sum(p, axis=0, keepdims=True)
            pv = _dot(vt_ref[hs, pl.ds(ks, tk)], p.astype(BF16))
            acc_ref[h] = alpha * acc_ref[h] + pv
        for a, s in enumerate(pending):
            sp_ref[a] = s

    def body(j, carry):
        tile_step(pl.multiple_of(j * tk, tk), pl.multiple_of((j + 1) * tk, tk), False)
        return carry

    for a in range(QK_AHEAD):
        sp_ref[a] = scores(0, a)
    lax.fori_loop(0, i, body, 0)
    tile_step(pl.multiple_of(i * tk, tk), None, True)
    for h in range(N_HEADS):
        o_ref[:, h * HEAD_DIM:(h + 1) * HEAD_DIM] = (acc_ref[h] / l_ref[h]).T


def _attn_prompt(q, qa, kb, ka, vbt, batch, seq, tq, tk):
    nq = seq // tq
    kernel = functools.partial(_attn_prompt_kernel, tq=tq, tk=tk)
    qblk = pl.BlockSpec((tq, ATTN_WIDTH), lambda b, i: (b * nq + i, 0))
    kblk = pl.BlockSpec((seq, ATTN_WIDTH), lambda b, i: (b, 0), pipeline_mode=pl.Buffered(1))
    vblk = pl.BlockSpec((ATTN_WIDTH, seq), lambda b, i: (0, b), pipeline_mode=pl.Buffered(1))
    return pl.pallas_call(
        kernel,
        grid=(batch, nq),
        in_specs=[qblk, qblk, kblk, kblk, vblk],
        out_specs=pl.BlockSpec((tq, ATTN_WIDTH), lambda b, i: (b * nq + i, 0)),
        out_shape=jax.ShapeDtypeStruct((batch * seq, ATTN_WIDTH), F32),
        scratch_shapes=[pltpu.VMEM((N_HEADS, 1, tq), F32), pltpu.VMEM((N_HEADS, 1, tq), F32),
                        pltpu.VMEM((N_HEADS, HEAD_DIM, tq), F32),
                        pltpu.VMEM((N_HEADS, 2 * HEAD_DIM, tq), BF16),
                        pltpu.VMEM((QK_AHEAD, tk, tq), F32)],
        compiler_params=_params(("parallel", "arbitrary")),
        name="attn_prompt",
    )(q, qa, kb, ka, vbt)


def _attn_sample_kernel(q_ref, kn_ref, vn_ref, ck_hbm, cv_hbm, fc_ref, fnrow_ref, fncol_ref,
                        o_ref, kbuf_ref, vbuf_ref, sem, *, t_new, past):
    b = pl.program_id(0)
    slot = b % 2

    def fetch(batch_idx, into):
        copies = []
        for h in range(N_HEADS):
            copies.append(pltpu.make_async_copy(ck_hbm.at[batch_idx, :, h, :], kbuf_ref.at[into, h],
                                                sem.at[into, 0]))
            copies.append(pltpu.make_async_copy(cv_hbm.at[batch_idx, :, h, :], vbuf_ref.at[into, h],
                                                sem.at[into, 1]))
        return copies

    @pl.when(b == 0)
    def _():
        for copy in fetch(0, 0):
            copy.start()

    @pl.when(b + 1 < pl.num_programs(0))
    def _():
        for copy in fetch(b + 1, 1 - slot):
            copy.start()

    for copy in fetch(b, slot):
        copy.wait()

    row = lax.broadcasted_iota(jnp.int32, (t_new, t_new), 0)
    col = lax.broadcasted_iota(jnp.int32, (t_new, t_new), 1)
    causal = col <= row
    for h in range(N_HEADS):
        hs = slice(h * HEAD_DIM, (h + 1) * HEAD_DIM)
        q = q_ref[:, hs]
        fn_q = fncol_ref[:, h:h + 1]
        fc = fc_ref[0, h:h + 1, :]
        fc_last = fc[:, past - 1:past]
        s_c = _qk(q, kbuf_ref[slot, h].astype(BF16)) + ((fc_last + fn_q) - fc) * LOG2E
        fn_k = fnrow_ref[0, h:h + 1, 0:t_new]
        s_n = _qk(q, kn_ref[:, hs]) + (fn_q - fn_k) * LOG2E
        s_n = jnp.where(causal, s_n, -jnp.inf)
        m = jnp.maximum(jnp.max(s_c, axis=-1, keepdims=True), jnp.max(s_n, axis=-1, keepdims=True))
        p_c = jnp.exp2(s_c - m)
        p_n = jnp.exp2(s_n - m)
        l = jnp.sum(p_c, axis=-1, keepdims=True) + jnp.sum(p_n, axis=-1, keepdims=True)
        acc = _dot(p_c.astype(BF16), vbuf_ref[slot, h].astype(BF16)) + _dot(p_n.astype(BF16), vn_ref[:, hs])
        o_ref[:, hs] = acc / l


def _attn_sample(q, kb, vb, cache_k, cache_v, fc_row, fn_row, fn_col, batch, t_new, past):
    kernel = functools.partial(_attn_sample_kernel, t_new=t_new, past=past)
    tok = lambda w: pl.BlockSpec((t_new, w), lambda b: (b, 0))
    return pl.pallas_call(
        kernel,
        grid=(batch,),
        in_specs=[
            tok(ATTN_WIDTH), tok(ATTN_WIDTH), tok(ATTN_WIDTH),
            pl.BlockSpec(memory_space=pl.ANY),
            pl.BlockSpec(memory_space=pl.ANY),
            pl.BlockSpec((1, N_HEADS, past), lambda b: (b, 0, 0)),
            pl.BlockSpec((1, N_HEADS, LANES), lambda b: (b, 0, 0)),
            tok(N_HEADS),
        ],
        out_specs=tok(ATTN_WIDTH),
        out_shape=jax.ShapeDtypeStruct((batch * t_new, ATTN_WIDTH), F32),
        scratch_shapes=[pltpu.VMEM((2, N_HEADS, past, HEAD_DIM), F32),
                        pltpu.VMEM((2, N_HEADS, past, HEAD_DIM), F32),
                        pltpu.SemaphoreType.DMA((2, 2))],
        compiler_params=_params(("arbitrary",)),
        name="attn_sample",
    )(q, kb, vb, cache_k, cache_v, fc_row, fn_row, fn_col)


def _pack_bf16_pair(lo, hi):
    lo_bits = lax.bitcast_convert_type(lo, jnp.uint32) >> 16
    hi_bits = lax.bitcast_convert_type(hi, jnp.uint32) & jnp.uint32(0xFFFF0000)
    return lo_bits | hi_bits


def _unpack_bf16_pair(w):
    lo = lax.bitcast_convert_type(w << 16, F32).astype(BF16)
    hi = lax.bitcast_convert_type(w & jnp.uint32(0xFFFF0000), F32).astype(BF16)
    return lo, hi


def _merge_kernel(yc_ref, ya_ref, x_ref, ga_ref, woc_ref, woa_ref, gf_ref, wrh_ref, wrl_ref, br_ref,
                  h_ref, xp_ref, lg_ref):
    ya = ya_ref[...]
    ya_n = (ya * lax.rsqrt(jnp.mean(ya * ya, axis=-1, keepdims=True) + EPS) * ga_ref[...]).astype(BF16)
    y = _dot(yc_ref[...], woc_ref[...]) + _dot(ya_n, woa_ref[...])
    h = x_ref[...] + y
    h_ref[...] = h
    xn = h * lax.rsqrt(jnp.mean(h * h, axis=-1, keepdims=True) + EPS) * gf_ref[...]
    xn_hi = xn.astype(BF16)
    xn_hi32 = xn_hi.astype(F32)
    xn_lo = (xn - xn_hi32).astype(BF16)
    lg_ref[...] = (_dot(xn_hi, wrh_ref[...]) + _dot(xn_lo, wrh_ref[...]) + _dot(xn_hi, wrl_ref[...])
                   + br_ref[...])
    half = D_MODEL // 2
    xp_ref[...] = _pack_bf16_pair(xn_hi32[:, :half], xn_hi32[:, half:])


def _merge(yc_n, ya, x, ga, w_out, g_ffn, wr_hi, wr_lo, b_r, tm):
    n = x.shape[0]
    row = lambda w: pl.BlockSpec((tm, w), lambda i: (i, 0))
    return pl.pallas_call(
        _merge_kernel,
        grid=(n // tm,),
        in_specs=[row(CONV_WIDTH), row(ATTN_WIDTH), row(D_MODEL), _resident((1, ATTN_WIDTH)),
                  _resident((CONV_WIDTH, D_MODEL), (0, 0)), _resident((ATTN_WIDTH, D_MODEL), (1, 0)),
                  _resident((1, D_MODEL)),
                  _resident((D_MODEL, LANES)), _resident((D_MODEL, LANES)), _resident((1, LANES))],
        out_specs=(row(D_MODEL), row(D_MODEL // 2), row(LANES)),
        out_shape=(jax.ShapeDtypeStruct((n, D_MODEL), F32),
                   jax.ShapeDtypeStruct((n, D_MODEL // 2), jnp.uint32),
                   jax.ShapeDtypeStruct((n, LANES), F32)),
        compiler_params=_params(("parallel",)),
        name="merge_out",
    )(yc_n, ya, x, ga, w_out, w_out, g_ffn, wr_hi, wr_lo, b_r)


def _route_kernel(lg_ref, info_ref, infot_ref, cnt_ref, carry_ref, *, tm):
    step = pl.program_id(0)

    @pl.when(step == 0)
    def _():
        carry_ref[...] = jnp.zeros_like(carry_ref)

    lg = lg_ref[...]
    lane = lax.broadcasted_iota(jnp.int32, lg.shape, 1)
    lanef = lane.astype(F32)
    big = jnp.float32(1e9)
    rmax = lambda v: jnp.max(v, axis=-1, keepdims=True)
    rmin = lambda v: jnp.min(v, axis=-1, keepdims=True)
    rsum = lambda v: jnp.sum(v, axis=-1, keepdims=True)

    is_g = (lane >= N_EXPERTS) & (lane < N_EXPERTS + N_GROUPS)
    gl = jnp.where(is_g, lg, NEG_BIG)
    gmax = rmax(gl)
    gsum = rsum(jnp.where(is_g, jnp.exp(gl - gmax), 0.0))
    pg_star = 1.0 / gsum
    g_idx = rmin(jnp.where(is_g & (gl == gmax), lanef - N_EXPERTS, big))

    e_lo = g_idx * EXPERTS_PER_GROUP
    is_e = (lanef >= e_lo) & (lanef < e_lo + EXPERTS_PER_GROUP)
    el = jnp.where(is_e, lg, NEG_BIG)
    m1 = rmax(el)
    i1 = rmin(jnp.where(is_e & (el == m1), lanef, big))
    sel1 = lanef == i1
    el2 = jnp.where(sel1, NEG_BIG, el)
    m2 = rmax(el2)
    i2 = rmin(jnp.where(is_e & (el2 == m2) & jnp.logical_not(sel1), lanef, big))
    sel2 = lanef == i2
    z = rsum(jnp.where(is_e, jnp.exp(el - m1), 0.0))
    p1 = 1.0 / z
    p2 = jnp.exp(m2 - m1) / z
    gate1 = pg_star * p1 / (p1 + p2)
    gate2 = pg_star * p2 / (p1 + p2)

    onehot = jnp.where(sel1 | sel2, 1.0, 0.0)
    r = lax.broadcasted_iota(jnp.int32, (tm, tm), 0)
    c = lax.broadcasted_iota(jnp.int32, (tm, tm), 1)
    tri = jnp.where(c < r, 1.0, 0.0).astype(BF16)
    before = _dot(tri, onehot.astype(BF16)) + carry_ref[...]
    rank1 = rsum(jnp.where(sel1, before, 0.0))
    rank2 = rsum(jnp.where(sel2, before, 0.0))
    carry_ref[...] = carry_ref[...] + jnp.sum(onehot, axis=0, keepdims=True)
    cnt_ref[...] = carry_ref[...]

    info = jnp.zeros_like(lg)
    for k, val in enumerate((i1, i2, rank1, rank2, gate1, gate2)):
        info = jnp.where(lane == k, val, info)
    info_ref[...] = info
    infot_ref[...] = info.T[0:8, :]


def _route(logits, tm):
    n = logits.shape[0]
    kernel = functools.partial(_route_kernel, tm=tm)
    return pl.pallas_call(
        kernel,
        grid=(n // tm,),
        in_specs=[pl.BlockSpec((tm, LANES), lambda i: (i, 0))],
        out_specs=(pl.BlockSpec((tm, LANES), lambda i: (i, 0)),
                   pl.BlockSpec((8, tm), lambda i: (0, i)),
                   pl.BlockSpec((1, LANES), lambda i: (0, 0))),
        out_shape=(jax.ShapeDtypeStruct((n, LANES), F32), jax.ShapeDtypeStruct((8, n), F32),
                   jax.ShapeDtypeStruct((1, LANES), F32)),
        scratch_shapes=[pltpu.VMEM((1, LANES), F32)],
        compiler_params=_params(("arbitrary",)),
        name="route",
    )(logits)


DISPATCH_CHUNK = 128


def _dispatch_kernel(dest_ref, xa_ref, xb_ref, xs_in_hbm, xs_hbm, sem):
    del xs_in_hbm
    n_all = dest_ref.shape[0] // 2

    def scatter(src_ref, t_off):
        def row_copy(t, k):
            return pltpu.make_async_copy(src_ref.at[pl.ds(t, 1)],
                                         xs_hbm.at[pl.ds(dest_ref[k * n_all + t_off + t], 1)], sem)

        def issue(c):
            def body(r, carry):
                row_copy(c * DISPATCH_CHUNK + r, 0).start(priority=0)
                row_copy(c * DISPATCH_CHUNK + r, 1).start(priority=1)
                return carry
            lax.fori_loop(0, DISPATCH_CHUNK, body, 0, unroll=8)

        def drain(c):
            def body(r, carry):
                row_copy(c * DISPATCH_CHUNK + r, 0).wait()
                row_copy(c * DISPATCH_CHUNK + r, 1).wait()
                return carry
            lax.fori_loop(0, DISPATCH_CHUNK, body, 0, unroll=8)

        n_chunks = src_ref.shape[0] // DISPATCH_CHUNK

        def chunk(c, carry):
            issue(c)

            @pl.when(c > 0)
            def _():
                drain(c - 1)
            return carry

        lax.fori_loop(0, n_chunks, chunk, 0)
        drain(n_chunks - 1)

    scatter(xa_ref, 0)
    scatter(xb_ref, xa_ref.shape[0])


def _dispatch(dest_flat, xa, xb, n_slots):
    w = xa.shape[1]
    xs0 = jnp.zeros((n_slots, w), xa.dtype)
    vmem = pl.BlockSpec(memory_space=pltpu.VMEM)
    grid_spec = pltpu.PrefetchScalarGridSpec(
        num_scalar_prefetch=1,
        grid=(1,),
        in_specs=[vmem, vmem, pl.BlockSpec(memory_space=pl.ANY)],
        out_specs=pl.BlockSpec(memory_space=pl.ANY),
        scratch_shapes=[pltpu.SemaphoreType.DMA(())],
    )
    return pl.pallas_call(
        _dispatch_kernel,
        grid_spec=grid_spec,
        out_shape=jax.ShapeDtypeStruct((n_slots, w), xa.dtype),
        input_output_aliases={3: 0},
        compiler_params=pltpu.CompilerParams(dimension_semantics=("arbitrary",),
                                             vmem_limit_bytes=VMEM_LIMIT, has_side_effects=True),
        name="dispatch",
    )(dest_flat, xa, xb, xs0)


CAST_CHUNK_ELEMS = 64 * 1024
WEIGHT_DMA_SPLIT = 4
WEIGHT_DMA_PRIORITY = 1


def _expert_kernel(be_ref, nu_ref, nx_ref, xs_ref, w1_hbm, w3_hbm, w2_hbm, y_ref,
                   w1f_ref, w3f_ref, w2f_ref, w1b_ref, w3b_ref, w2b_ref, sem):
    b = pl.program_id(0)
    active = b < nu_ref[0]
    new_expert = jnp.logical_or(b == 0, be_ref[b] != be_ref[jnp.maximum(b - 1, 0)])

    def fetch(e):
        copies = []
        for k, (src, dst) in enumerate(((w1_hbm, w1f_ref), (w3_hbm, w3f_ref), (w2_hbm, w2f_ref))):
            slab = dst.shape[0] // WEIGHT_DMA_SPLIT
            for c in range(WEIGHT_DMA_SPLIT):
                rows = pl.ds(c * slab, slab)
                copies.append(pltpu.make_async_copy(src.at[e, rows], dst.at[rows], sem.at[k]))
        return copies

    @pl.when(b == 0)
    def _():
        for copy in fetch(be_ref[0]):
            copy.start(priority=WEIGHT_DMA_PRIORITY)

    @pl.when(jnp.logical_and(active, new_expert))
    def _():
        for copy in fetch(be_ref[b]):
            copy.wait()
        for src, dst in ((w1f_ref, w1b_ref), (w3f_ref, w3b_ref), (w2f_ref, w2b_ref)):
            rows, cols = src.shape
            chunk = CAST_CHUNK_ELEMS // cols

            def cast_rows(c, carry, src=src, dst=dst, chunk=chunk):
                r0 = pl.multiple_of(c * chunk, chunk)
                dst[pl.ds(r0, chunk), :] = src[pl.ds(r0, chunk), :].astype(BF16)
                return carry

            lax.fori_loop(0, rows // chunk, cast_rows, 0)

        @pl.when(nx_ref[b] >= 0)
        def _():
            for copy in fetch(nx_ref[b]):
                copy.start(priority=WEIGHT_DMA_PRIORITY)

    @pl.when(active)
    def _():
        lo, hi = _unpack_bf16_pair(xs_ref[...])
        half = D_MODEL // 2
        h1 = _dot(lo, w1b_ref[:half, :]) + _dot(hi, w1b_ref[half:, :])
        h3 = _dot(lo, w3b_ref[:half, :]) + _dot(hi, w3b_ref[half:, :])
        h = (h1 * jax.nn.sigmoid(h1) * h3).astype(BF16)
        y_ref[...] = _dot(h, w2b_ref[...])

    @pl.when(jnp.logical_not(active))
    def _():
        y_ref[...] = jnp.zeros_like(y_ref)


def _experts(block_e, n_used, next_e, xs, w1, w3, w2):
    n_slots = xs.shape[0]
    nb = n_slots // MOE_BLOCK
    blk = lambda b, be, nu, nx: (jnp.minimum(b, nu[0] - 1), 0)
    hbm = pl.BlockSpec(memory_space=pl.ANY)
    grid_spec = pltpu.PrefetchScalarGridSpec(
        num_scalar_prefetch=3,
        grid=(nb,),
        in_specs=[pl.BlockSpec((MOE_BLOCK, D_MODEL // 2), blk), hbm, hbm, hbm],
        out_specs=pl.BlockSpec((MOE_BLOCK, D_MODEL), lambda b, be, nu, nx: (b, 0)),
        scratch_shapes=[pltpu.VMEM((D_MODEL, D_EXPERT), F32), pltpu.VMEM((D_MODEL, D_EXPERT), F32),
                        pltpu.VMEM((D_EXPERT, D_MODEL), F32),
                        pltpu.VMEM((D_MODEL, D_EXPERT), BF16), pltpu.VMEM((D_MODEL, D_EXPERT), BF16),
                        pltpu.VMEM((D_EXPERT, D_MODEL), BF16),
                        pltpu.SemaphoreType.DMA((3,))],
    )
    return pl.pallas_call(
        _expert_kernel,
        grid_spec=grid_spec,
        out_shape=jax.ShapeDtypeStruct((n_slots, D_MODEL), F32),
        compiler_params=_params(("arbitrary",)),
        name="experts",
    )(block_e, n_used, next_e, xs, w1, w3, w2)


PLE_SPLIT = 4


def _combine_ple_kernel(dest_ref, h_ref, info_ref, p_ref, g_ref, *rest, tm, t_off):
    wpg_refs = rest[:PLE_SPLIT]
    wpp_ref, yb_hbm, o_ref, buf_ref, sem = rest[PLE_SPLIT:]
    i = pl.program_id(0)
    last = pl.num_programs(0) - 1
    n_all = dest_ref.shape[0] // 2
    slot = i % 2

    def row_copy(step, into, r, k):
        t = t_off + step * tm + r
        return pltpu.make_async_copy(yb_hbm.at[pl.ds(dest_ref[k * n_all + t], 1)],
                                     buf_ref.at[into, k, pl.ds(r, 1)], sem.at[into])

    def drain(into):
        def body(r, carry):
            for k in range(2):
                pltpu.make_async_copy(yb_hbm.at[pl.ds(0, 1)], buf_ref.at[into, k, pl.ds(r, 1)],
                                      sem.at[into]).wait()
            return carry
        lax.fori_loop(0, tm, body, 0, unroll=8)

    @pl.when(i == 0)
    def _():
        def body(r, carry):
            row_copy(0, 0, r, 0).start()
            row_copy(0, 0, r, 1).start()
            return carry
        lax.fori_loop(0, tm, body, 0, unroll=8)

    drain(slot)
    h2 = h_ref[...] + (info_ref[:, 4:5] * buf_ref[slot, 0] + info_ref[:, 5:6] * buf_ref[slot, 1])
    hn = (h2 * lax.rsqrt(jnp.mean(h2 * h2, axis=-1, keepdims=True) + EPS) * g_ref[...]).astype(BF16)
    pb = p_ref[...].astype(BF16)

    nxt = jnp.minimum(i + 1, last)
    ch = D_MODEL // PLE_SPLIT
    per = tm // PLE_SPLIT
    for k in range(PLE_SPLIT):
        for r in range(k * per, (k + 1) * per):
            row_copy(nxt, 1 - slot, r, 0).start(priority=0)
            row_copy(nxt, 1 - slot, r, 1).start(priority=1)
        cs = slice(k * ch, (k + 1) * ch)
        gate = jax.nn.sigmoid(_dot(hn, wpg_refs[k][...]))
        o_ref[:, cs] = h2[:, cs] + gate * _dot(pb, wpp_ref[:, cs])

    @pl.when(i == last)
    def _():
        drain(1 - slot)


def _combine_ple(dest_flat, h, info, yb, p, g_ple, w_pg, w_pp, t_off, tm):
    n = h.shape[0]
    kernel = functools.partial(_combine_ple_kernel, tm=tm, t_off=t_off)
    ob = t_off // tm
    row = lambda w: pl.BlockSpec((tm, w), lambda i, d: (i, 0))
    grid_spec = pltpu.PrefetchScalarGridSpec(
        num_scalar_prefetch=1,
        grid=(n // tm,),
        in_specs=[row(D_MODEL), pl.BlockSpec((tm, LANES), lambda i, d: (i + ob, 0)), row(PLE_DIM),
                  _resident((1, D_MODEL))]
        + [_resident((D_MODEL, D_MODEL // PLE_SPLIT), (0, k)) for k in range(PLE_SPLIT)]
        + [_resident((PLE_DIM, D_MODEL)), pl.BlockSpec(memory_space=pl.ANY)],
        out_specs=row(D_MODEL),
        scratch_shapes=[pltpu.VMEM((2, 2, tm, D_MODEL), F32), pltpu.SemaphoreType.DMA((2,))],
    )
    return pl.pallas_call(
        kernel,
        grid_spec=grid_spec,
        out_shape=jax.ShapeDtypeStruct((n, D_MODEL), F32),
        compiler_params=_params(("arbitrary",)),
        name="combine_ple",
    )(dest_flat, h, info, p, g_ple, *([w_pg] * PLE_SPLIT), w_pp, yb)


def _mixer_tokens(x2d, wts, tm):
    return _in_proj(x2d, wts["g_mix"], wts["w_parts"], wts["w_f"], wts["b_f"],
                    wts["q_gain"], wts["k_gain"], tm)


def kernel(x_prompt, x_sample, cache_k, cache_v, cache_logf, cache_conv, p_prompt, p_sample,
           g_mix, w_in, b_f, q_gain, k_gain, w_dw, b_dw, ln_g, ln_b, gc, ga, w_out,
           g_ffn, w_router_g, b_router_g, w_router_e, b_router_e, w1, w3, w2,
           g_ple, w_pg, w_pp):
    batch, seq, _ = x_prompt.shape
    dec_batch, t_new, _ = x_sample.shape
    past = cache_k.shape[2]
    n_p = batch * seq
    n_s = dec_batch * t_new
    n_all = n_p + n_s
    tm = 256
    li = 0

    w_in_l = w_in[li]
    pad_lanes = lambda a: jnp.pad(a, ((0, 0), (0, LANES - a.shape[1])))
    row2d = lambda a: a.reshape(1, -1)
    wts = {
        "g_mix": row2d(g_mix[li]),
        "w_parts": [w_in_l[:, c:c + CONV_WIDTH].astype(BF16) for c in range(0, MAIN_COLS, CONV_WIDTH)],
        "w_f": pad_lanes(w_in_l[:, MAIN_COLS:]).astype(BF16),
        "b_f": pad_lanes(row2d(b_f[li])),
        "q_gain": row2d(q_gain[li]),
        "k_gain": row2d(k_gain[li]),
    }
    w_dw_p = jnp.pad(w_dw[li], ((0, CONV_HALO - CONV_KERNEL), (0, 0)))
    w_out_b = w_out[li].astype(BF16)
    w_r = pad_lanes(jnp.concatenate([w_router_e[li], w_router_g[li]], axis=1))
    wr_hi = w_r.astype(BF16)
    wr_lo = (w_r - wr_hi.astype(F32)).astype(BF16)
    b_r = pad_lanes(row2d(jnp.concatenate([b_router_e[li], b_router_g[li]])))
    w_pg_b, w_pp_b = w_pg[li].astype(BF16), w_pp[li].astype(BF16)

    xp = x_prompt.reshape(n_p, D_MODEL)
    u_p, q_p, k_p, kb_p, v_p, _, vbt_p, lf_p = _mixer_tokens(xp, wts, tm)
    lf_p_row = lf_p.reshape(batch, seq, N_HEADS).transpose(0, 2, 1)
    f_p_row = _cumsum_lanes(lf_p_row.reshape(batch * N_HEADS, seq)).reshape(batch, N_HEADS, seq)
    f_p_col = f_p_row.transpose(0, 2, 1).reshape(n_p, N_HEADS)
    ctx_p = jnp.zeros((batch, CONV_HALO, CONV_WIDTH), F32)
    conv_args = (w_dw_p, row2d(b_dw[li]), row2d(ln_g[li]), row2d(ln_b[li]), row2d(gc[li]))
    yc_p = _conv_module(u_p, ctx_p, *conv_args, batch, seq, tm)
    qa_p, ka_p = _attn_aug(f_p_col, tm)
    ya_p = _attn_prompt(q_p, qa_p, kb_p, ka_p, vbt_p, batch, seq, 256, 256)
    merge_args = (row2d(ga[li]), w_out_b, row2d(g_ffn[li]), wr_hi, wr_lo, b_r)
    h_p, xpk_p, lg_p = _merge(yc_p, ya_p, xp, *merge_args, tm)

    xs_ = x_sample.reshape(n_s, D_MODEL)
    u_s, q_s, k_s, kb_s, v_s, vb_s, _, lf_s = _mixer_tokens(xs_, wts, tm)
    clf_row = cache_logf[li].transpose(0, 2, 1).reshape(dec_batch * N_HEADS, past)
    fc_row = _cumsum_lanes(clf_row).reshape(dec_batch, N_HEADS, past)
    lf_s_row = lf_s.reshape(dec_batch, t_new, N_HEADS).transpose(0, 2, 1).reshape(dec_batch * N_HEADS, t_new)
    fn_row = _cumsum_lanes(jnp.pad(lf_s_row, ((0, 0), (0, LANES - t_new)))).reshape(dec_batch, N_HEADS, LANES)
    fn_col = fn_row[:, :, :t_new].transpose(0, 2, 1).reshape(n_s, N_HEADS)
    ctx_s = jnp.pad(cache_conv[li], ((0, 0), (CONV_HALO - CONV_STATE, 0), (0, 0)))
    yc_s = _conv_module(u_s, ctx_s, *conv_args, dec_batch, t_new, t_new)
    ya_s = _attn_sample(q_s, kb_s, vb_s,
                        cache_k[li], cache_v[li],
                        fc_row, fn_row, fn_col, dec_batch, t_new, past)
    h_s, xpk_s, lg_s = _merge(yc_s, ya_s, xs_, *merge_args, tm)

    info, info_t, counts = _route(jnp.concatenate([lg_p, lg_s], axis=0), tm)
    counts = counts[0, :N_EXPERTS].astype(jnp.int32)
    bcounts = (counts + MOE_BLOCK - 1) // MOE_BLOCK
    bends = jnp.cumsum(bcounts)
    pstarts = (bends - bcounts) * MOE_BLOCK
    n_rows = n_all * 2
    nb = -(-(n_rows + N_EXPERTS * (MOE_BLOCK - 1)) // MOE_BLOCK)
    e_idx = info_t[0:2].astype(jnp.int32)
    expert_ids = jnp.arange(N_EXPERTS, dtype=jnp.int32)[:, None, None]
    seg_start = jnp.sum(jnp.where(e_idx[None] == expert_ids, pstarts[:, None, None], 0), axis=0)
    dest = (seg_start + info_t[2:4].astype(jnp.int32)).reshape(n_rows)
    block_e = jnp.minimum(jnp.sum(bends[None, :] <= jnp.arange(nb, dtype=jnp.int32)[:, None], axis=1),
                          N_EXPERTS - 1).astype(jnp.int32)
    n_used = bends[N_EXPERTS - 1:].astype(jnp.int32)
    xs_sorted = _dispatch(dest, xpk_p, xpk_s, nb * MOE_BLOCK)
    seg_end = jnp.sum(jnp.where(block_e[None, :] == expert_ids[:, :, 0], bends[:, None], 0), axis=0)
    next_e = jnp.where(seg_end < n_used[0], block_e[jnp.minimum(seg_end, nb - 1)], -1).astype(jnp.int32)
    yb = _experts(block_e, n_used, next_e, xs_sorted, w1[li], w3[li], w2[li])

    ple_args = (row2d(g_ple[li]), w_pg_b, w_pp_b)
    y_p = _combine_ple(dest, h_p, info, yb, p_prompt[li].reshape(n_p, PLE_DIM), *ple_args, 0, tm)
    y_s = _combine_ple(dest, h_s, info, yb, p_sample[li].reshape(n_s, PLE_DIM), *ple_args, n_p, tm)

    heads = lambda a, b, t: a.reshape(1, b, t, N_HEADS, HEAD_DIM)
    return (
        y_p.reshape(batch, seq, D_MODEL),
        y_s.reshape(dec_batch, t_new, D_MODEL),
        heads(k_p, batch, seq), heads(v_p, batch, seq),
        lf_p.reshape(1, batch, seq, N_HEADS),
        u_p.reshape(batch, seq, CONV_WIDTH)[None, :, seq - CONV_STATE:, :],
        heads(k_s, dec_batch, t_new), heads(v_s, dec_batch, t_new),
        lf_s.reshape(1, dec_batch, t_new, N_HEADS),
        u_s.reshape(dec_batch, t_new, CONV_WIDTH)[None, :, t_new - CONV_STATE:, :],
    )
```

```python
import functools

import jax
import jax.numpy as jnp
from jax import lax
from jax.experimental import pallas as pl
from jax.experimental.pallas import tpu as pltpu

D_MODEL = 2048
CONV_WIDTH = 1024
ATTN_WIDTH = 1024
HEAD_DIM = 128
N_HEADS = 8
CONV_KERNEL = 31
CONV_STATE = CONV_KERNEL - 1
N_GROUPS = 4
EXPERTS_PER_GROUP = 8
N_EXPERTS = 32
D_EXPERT = 512
PLE_DIM = 256
MOE_BLOCK = 256
EPS = 1e-6
MAIN_COLS = 2 * CONV_WIDTH + 3 * ATTN_WIDTH

LANES = 128
CONV_HALO = 32
VMEM_LIMIT = 56 * 1024 * 1024

F32 = jnp.float32
BF16 = jnp.bfloat16
NEG_BIG = -1e30
LOG2E = 1.4426950408889634


def _dot(a, b):
    return jnp.dot(a, b, preferred_element_type=F32)


def _params(sem):
    return pltpu.CompilerParams(dimension_semantics=sem, vmem_limit_bytes=VMEM_LIMIT)


def _resident(shape, index=None):
    index = (0,) * len(shape) if index is None else index
    return pl.BlockSpec(shape, lambda *_: index, pipeline_mode=pl.Buffered(1))


def _inproj_kernel(x_ref, g_ref, wval_ref, wgate_ref, wq_ref, wk_ref, wv_ref, wf_ref, bf_ref, qg_ref, kg_ref,
                   u_ref, q_ref, k_ref, kb_ref, v_ref, vb_ref, vbt_ref, lf_ref):
    x = x_ref[...]
    ms = jnp.mean(x * x, axis=-1, keepdims=True)
    a = (x * lax.rsqrt(ms + EPS) * g_ref[...]).astype(BF16)

    ch = 256
    for c in range(0, CONV_WIDTH, ch):
        val = _dot(a, wval_ref[:, c:c + ch])
        gate = _dot(a, wgate_ref[:, c:c + ch])
        u_ref[:, c:c + ch] = val * jax.nn.sigmoid(gate)

    def head_norm(z, gain):
        return z * lax.rsqrt(jnp.mean(z * z, axis=-1, keepdims=True) + EPS) * gain

    scale = LOG2E * HEAD_DIM ** -0.5
    for c in range(0, ATTN_WIDTH, ch):
        zq = _dot(a, wq_ref[:, c:c + ch])
        zk = _dot(a, wk_ref[:, c:c + ch])
        zv = _dot(a, wv_ref[:, c:c + ch])
        for s in range(0, ch, HEAD_DIM):
            qn = head_norm(zq[:, s:s + HEAD_DIM], qg_ref[...])
            kn = head_norm(zk[:, s:s + HEAD_DIM], kg_ref[...])
            q_ref[:, c + s:c + s + HEAD_DIM] = (qn * scale).astype(BF16)
            k_ref[:, c + s:c + s + HEAD_DIM] = kn
            kb_ref[:, c + s:c + s + HEAD_DIM] = kn.astype(BF16)
        v_ref[:, c:c + ch] = zv
        vb_ref[:, c:c + ch] = zv.astype(BF16)
        vbt_ref[c:c + ch, :] = zv.T.astype(BF16)

    f = _dot(a, wf_ref[...]) + bf_ref[...]
    lf = jnp.minimum(f, 0.0) - jnp.log1p(jnp.exp(-jnp.abs(f)))
    lf_ref[...] = lf[:, :N_HEADS]


def _in_proj(x, g_mix, w_parts, w_f, b_f, q_gain, k_gain, tm):
    n = x.shape[0]
    row = lambda w: pl.BlockSpec((tm, w), lambda i: (i, 0))
    out_shape = (
        jax.ShapeDtypeStruct((n, CONV_WIDTH), F32),
        jax.ShapeDtypeStruct((n, ATTN_WIDTH), BF16),
        jax.ShapeDtypeStruct((n, ATTN_WIDTH), F32),
        jax.ShapeDtypeStruct((n, ATTN_WIDTH), BF16),
        jax.ShapeDtypeStruct((n, ATTN_WIDTH), F32),
        jax.ShapeDtypeStruct((n, ATTN_WIDTH), BF16),
        jax.ShapeDtypeStruct((ATTN_WIDTH, n), BF16),
        jax.ShapeDtypeStruct((n, N_HEADS), F32),
    )
    return pl.pallas_call(
        _inproj_kernel,
        grid=(n // tm,),
        in_specs=[row(D_MODEL), _resident((1, D_MODEL))] + [_resident((D_MODEL, CONV_WIDTH))] * 5 + [
                  _resident((D_MODEL, LANES)), _resident((1, LANES)),
                  _resident((1, HEAD_DIM)), _resident((1, HEAD_DIM))],
        out_specs=(row(CONV_WIDTH), row(ATTN_WIDTH), row(ATTN_WIDTH), row(ATTN_WIDTH),
                   row(ATTN_WIDTH), row(ATTN_WIDTH),
                   pl.BlockSpec((ATTN_WIDTH, tm), lambda i: (0, i)), row(N_HEADS)),
        out_shape=out_shape,
        compiler_params=_params(("parallel",)),
        name="in_proj",
    )(x, g_mix, *w_parts, w_f, b_f, q_gain, k_gain)


def _cumsum_kernel(x_ref, o_ref):
    x = x_ref[...]
    width = x.shape[1]
    lane = lax.broadcasted_iota(jnp.int32, x.shape, 1)
    s = 1
    while s < width:
        x = x + jnp.where(lane >= s, pltpu.roll(x, s, axis=1), 0.0)
        s *= 2
    o_ref[...] = x


def _cumsum_lanes(x):
    return pl.pallas_call(
        _cumsum_kernel,
        out_shape=jax.ShapeDtypeStruct(x.shape, F32),
        name="cumsum",
    )(x)


def _conv_kernel(u_ref, halo_ref, ctx_ref, w_ref, bdw_ref, lng_ref, lnb_ref, gc_ref,
                 o_ref, ext_ref, y_ref, *, tm):
    i = pl.program_id(1)

    @pl.when(i == 0)
    def _():
        ext_ref[0:CONV_HALO, :] = ctx_ref[0]

    @pl.when(i > 0)
    def _():
        ext_ref[0:CONV_HALO, :] = halo_ref[...]

    ext_ref[CONV_HALO:CONV_HALO + tm, :] = u_ref[...]

    rows = min(64, tm)
    ch = LANES
    sub = 8
    wlen = rows + CONV_HALO
    first = CONV_HALO - CONV_STATE

    def conv_rows(r, carry):
        r0 = pl.multiple_of(r * rows, rows)
        for c in range(0, CONV_WIDTH, ch):
            acc = jnp.zeros((rows, ch), F32)
            win = ext_ref[pl.ds(r0, wlen), c:c + ch]
            for rho in range(sub):
                sh = win if rho == 0 else pltpu.roll(win, wlen - rho, axis=0)
                for a in range(wlen // sub):
                    j = sub * a + rho - first
                    if 0 <= j < CONV_KERNEL:
                        acc = acc + sh[sub * a:sub * a + rows] * w_ref[j:j + 1, c:c + ch]
            y_ref[pl.ds(r0, rows), c:c + ch] = acc
        return carry

    lax.fori_loop(0, tm // rows, conv_rows, 0)

    nrows = min(128, tm)

    def norm_rows(r, carry):
        r0 = pl.multiple_of(r * nrows, nrows)
        y = y_ref[pl.ds(r0, nrows), :] + bdw_ref[...]
        mu = jnp.mean(y, axis=-1, keepdims=True)
        yc = y - mu
        var = jnp.mean(yc * yc, axis=-1, keepdims=True)
        z = yc * lax.rsqrt(var + EPS) * lng_ref[...] + lnb_ref[...]
        s = z * jax.nn.sigmoid(z)
        ms = jnp.mean(s * s, axis=-1, keepdims=True)
        o_ref[pl.ds(r0, nrows), :] = (s * lax.rsqrt(ms + EPS) * gc_ref[...]).astype(BF16)
        return carry

    lax.fori_loop(0, tm // nrows, norm_rows, 0)


def _conv_module(u, ctx, w_dw, b_dw, ln_g, ln_b, gc, batch, seq, tm):
    nt = seq // tm
    hb = tm // CONV_HALO
    kernel = functools.partial(_conv_kernel, tm=tm)
    return pl.pallas_call(
        kernel,
        grid=(batch, nt),
        in_specs=[
            pl.BlockSpec((tm, CONV_WIDTH), lambda b, i: (b * nt + i, 0)),
            pl.BlockSpec((CONV_HALO, CONV_WIDTH),
                         lambda b, i: (jnp.maximum((b * nt + i) * hb - 1, 0), 0)),
            pl.BlockSpec((1, CONV_HALO, CONV_WIDTH), lambda b, i: (b, 0, 0)),
            _resident((CONV_HALO, CONV_WIDTH)),
            _resident((1, CONV_WIDTH)), _resident((1, CONV_WIDTH)),
            _resident((1, CONV_WIDTH)), _resident((1, CONV_WIDTH)),
        ],
        out_specs=pl.BlockSpec((tm, CONV_WIDTH), lambda b, i: (b * nt + i, 0)),
        out_shape=jax.ShapeDtypeStruct((batch * seq, CONV_WIDTH), BF16),
        scratch_shapes=[pltpu.VMEM((CONV_HALO + tm, CONV_WIDTH), F32),
                        pltpu.VMEM((tm, CONV_WIDTH), F32)],
        compiler_params=_params(("parallel", "arbitrary")),
        name="conv_module",
    )(u, u, ctx, w_dw, b_dw, ln_g, ln_b, gc)


def _qk(q, k):
    return lax.dot_general(q, k, (((1,), (1,)), ((), ())), preferred_element_type=F32)


AUG_TERMS = 3
QK_AHEAD = 4


def _aug_kernel(f_ref, qa_ref, ka_ref):
    f = f_ref[...] * LOG2E
    tm = f.shape[0]
    lane = lax.broadcasted_iota(jnp.int32, (tm, HEAD_DIM), 1)
    for h in range(N_HEADS):
        hs = slice(h * HEAD_DIM, (h + 1) * HEAD_DIM)
        rest = f[:, h:h + 1]
        qa = jnp.where((lane >= AUG_TERMS) & (lane < 2 * AUG_TERMS), 1.0, 0.0)
        ka = jnp.where(lane < AUG_TERMS, 1.0, 0.0)
        for t in range(AUG_TERMS):
            piece = rest.astype(BF16).astype(F32)
            rest = rest - piece
            qa = jnp.where(lane == t, piece, qa)
            ka = jnp.where(lane == AUG_TERMS + t, -piece, ka)
        qa_ref[:, hs] = qa.astype(BF16)
        ka_ref[:, hs] = ka.astype(BF16)


def _attn_aug(f_col, tm):
    n = f_col.shape[0]
    return pl.pallas_call(
        _aug_kernel,
        grid=(n // tm,),
        in_specs=[pl.BlockSpec((tm, N_HEADS), lambda i: (i, 0))],
        out_specs=(pl.BlockSpec((tm, ATTN_WIDTH), lambda i: (i, 0)),
                   pl.BlockSpec((tm, ATTN_WIDTH), lambda i: (i, 0))),
        out_shape=(jax.ShapeDtypeStruct((n, ATTN_WIDTH), BF16),
                   jax.ShapeDtypeStruct((n, ATTN_WIDTH), BF16)),
        compiler_params=_params(("parallel",)),
        name="attn_aug",
    )(f_col)


def _attn_prompt_kernel(q_ref, qa_ref, k_ref, ka_ref, vt_ref, o_ref, m_ref, l_ref, acc_ref, qt_ref,
                        sp_ref, *, tq, tk):
    i = pl.program_id(1)
    key = lax.broadcasted_iota(jnp.int32, (tk, tq), 0)
    qry = lax.broadcasted_iota(jnp.int32, (tk, tq), 1)

    m_ref[...] = jnp.full(m_ref.shape, NEG_BIG, F32)
    l_ref[...] = jnp.zeros(l_ref.shape, F32)
    acc_ref[...] = jnp.zeros(acc_ref.shape, F32)
    for h in range(N_HEADS):
        hs = slice(h * HEAD_DIM, (h + 1) * HEAD_DIM)
        qt_ref[h] = jnp.concatenate([q_ref[:, hs], qa_ref[:, hs]], axis=1).T

    def scores(ks, h):
        hs = slice(h * HEAD_DIM, (h + 1) * HEAD_DIM)
        kf = jnp.concatenate([k_ref[pl.ds(ks, tk), hs], ka_ref[pl.ds(ks, tk), hs]], axis=1)
        return _dot(kf, qt_ref[h])

    def tile_step(ks, ks_next, masked):
        pending = [sp_ref[a] for a in range(QK_AHEAD)]
        for h in range(N_HEADS):
            hs = slice(h * HEAD_DIM, (h + 1) * HEAD_DIM)
            s = pending.pop(0)
            if h + QK_AHEAD < N_HEADS:
                pending.append(scores(ks, h + QK_AHEAD))
            elif ks_next is not None:
                pending.append(scores(ks_next, h + QK_AHEAD - N_HEADS))
            if masked:
                s = jnp.where(key <= qry, s, -jnp.inf)
            m = m_ref[h]
            m_new = jnp.maximum(m, jnp.max(s, axis=0, keepdims=True))
            alpha = jnp.exp2(m - m_new)
            p = jnp.exp2(s - m_new)
            m_ref[h] = m_new
            l_ref[h] = alpha * l_ref[h] + jnp.sum(p, axis=0, keepdims=True)
            pv = _dot(vt_ref[hs, pl.ds(ks, tk)], p.astype(BF16))
            acc_ref[h] = alpha * acc_ref[h] + pv
        for a, s in enumerate(pending):
            sp_ref[a] = s

    def body(j, carry):
        tile_step(pl.multiple_of(j * tk, tk), pl.multiple_of((j + 1) * tk, tk), False)
        return carry

    for a in range(QK_AHEAD):
        sp_ref[a] = scores(0, a)
    lax.fori_loop(0, i, body, 0)
    tile_step(pl.multiple_of(i * tk, tk), None, True)
    for h in range(N_HEADS):
        o_ref[:, h * HEAD_DIM:(h + 1) * HEAD_DIM] = (acc_ref[h] / l_ref[h]).T


def _attn_prompt(q, qa, kb, ka, vbt, batch, seq, tq, tk):
    nq = seq // tq
    kernel = functools.partial(_attn_prompt_kernel, tq=tq, tk=tk)
    qblk = pl.BlockSpec((tq, ATTN_WIDTH), lambda b, i: (b * nq + i, 0))
    kblk = pl.BlockSpec((seq, ATTN_WIDTH), lambda b, i: (b, 0), pipeline_mode=pl.Buffered(1))
    vblk = pl.BlockSpec((ATTN_WIDTH, seq), lambda b, i: (0, b), pipeline_mode=pl.Buffered(1))
    return pl.pallas_call(
        kernel,
        grid=(batch, nq),
        in_specs=[qblk, qblk, kblk, kblk, vblk],
        out_specs=pl.BlockSpec((tq, ATTN_WIDTH), lambda b, i: (b * nq + i, 0)),
        out_shape=jax.ShapeDtypeStruct((batch * seq, ATTN_WIDTH), F32),
        scratch_shapes=[pltpu.VMEM((N_HEADS, 1, tq), F32), pltpu.VMEM((N_HEADS, 1, tq), F32),
                        pltpu.VMEM((N_HEADS, HEAD_DIM, tq), F32),
                        pltpu.VMEM((N_HEADS, 2 * HEAD_DIM, tq), BF16),
                        pltpu.VMEM((QK_AHEAD, tk, tq), F32)],
        compiler_params=_params(("parallel", "arbitrary")),
        name="attn_prompt",
    )(q, qa, kb, ka, vbt)


def _attn_sample_kernel(q_ref, kn_ref, vn_ref, ck_hbm, cv_hbm, fc_ref, fnrow_ref, fncol_ref,
                        o_ref, kbuf_ref, vbuf_ref, sem, *, t_new, past):
    b = pl.program_id(0)
    slot = b % 2

    def fetch(batch_idx, into):
        copies = []
        for h in range(N_HEADS):
            copies.append(pltpu.make_async_copy(ck_hbm.at[batch_idx, :, h, :], kbuf_ref.at[into, h],
                                                sem.at[into, 0]))
            copies.append(pltpu.make_async_copy(cv_hbm.at[batch_idx, :, h, :], vbuf_ref.at[into, h],
                                                sem.at[into, 1]))
        return copies

    @pl.when(b == 0)
    def _():
        for copy in fetch(0, 0):
            copy.start()

    @pl.when(b + 1 < pl.num_programs(0))
    def _():
        for copy in fetch(b + 1, 1 - slot):
            copy.start()

    for copy in fetch(b, slot):
        copy.wait()

    row = lax.broadcasted_iota(jnp.int32, (t_new, t_new), 0)
    col = lax.broadcasted_iota(jnp.int32, (t_new, t_new), 1)
    causal = col <= row
    for h in range(N_HEADS):
        hs = slice(h * HEAD_DIM, (h + 1) * HEAD_DIM)
        q = q_ref[:, hs]
        fn_q = fncol_ref[:, h:h + 1]
        fc = fc_ref[0, h:h + 1, :]
        fc_last = fc[:, past - 1:past]
        s_c = _qk(q, kbuf_ref[slot, h].astype(BF16)) + ((fc_last + fn_q) - fc) * LOG2E
        fn_k = fnrow_ref[0, h:h + 1, 0:t_new]
        s_n = _qk(q, kn_ref[:, hs]) + (fn_q - fn_k) * LOG2E
        s_n = jnp.where(causal, s_n, -jnp.inf)
        m = jnp.maximum(jnp.max(s_c, axis=-1, keepdims=True), jnp.max(s_n, axis=-1, keepdims=True))
        p_c = jnp.exp2(s_c - m)
        p_n = jnp.exp2(s_n - m)
        l = jnp.sum(p_c, axis=-1, keepdims=True) + jnp.sum(p_n, axis=-1, keepdims=True)
        acc = _dot(p_c.astype(BF16), vbuf_ref[slot, h].astype(BF16)) + _dot(p_n.astype(BF16), vn_ref[:, hs])
        o_ref[:, hs] = acc / l


def _attn_sample(q, kb, vb, cache_k, cache_v, fc_row, fn_row, fn_col, batch, t_new, past):
    kernel = functools.partial(_attn_sample_kernel, t_new=t_new, past=past)
    tok = lambda w: pl.BlockSpec((t_new, w), lambda b: (b, 0))
    return pl.pallas_call(
        kernel,
        grid=(batch,),
        in_specs=[
            tok(ATTN_WIDTH), tok(ATTN_WIDTH), tok(ATTN_WIDTH),
            pl.BlockSpec(memory_space=pl.ANY),
            pl.BlockSpec(memory_space=pl.ANY),
            pl.BlockSpec((1, N_HEADS, past), lambda b: (b, 0, 0)),
            pl.BlockSpec((1, N_HEADS, LANES), lambda b: (b, 0, 0)),
            tok(N_HEADS),
        ],
        out_specs=tok(ATTN_WIDTH),
        out_shape=jax.ShapeDtypeStruct((batch * t_new, ATTN_WIDTH), F32),
        scratch_shapes=[pltpu.VMEM((2, N_HEADS, past, HEAD_DIM), F32),
                        pltpu.VMEM((2, N_HEADS, past, HEAD_DIM), F32),
                        pltpu.SemaphoreType.DMA((2, 2))],
        compiler_params=_params(("arbitrary",)),
        name="attn_sample",
    )(q, kb, vb, cache_k, cache_v, fc_row, fn_row, fn_col)


def _pack_bf16_pair(lo, hi):
    lo_bits = lax.bitcast_convert_type(lo, jnp.uint32) >> 16
    hi_bits = lax.bitcast_convert_type(hi, jnp.uint32) & jnp.uint32(0xFFFF0000)
    return lo_bits | hi_bits


def _unpack_bf16_pair(w):
    lo = lax.bitcast_convert_type(w << 16, F32).astype(BF16)
    hi = lax.bitcast_convert_type(w & jnp.uint32(0xFFFF0000), F32).astype(BF16)
    return lo, hi


def _merge_kernel(yc_ref, ya_ref, x_ref, ga_ref, woc_ref, woa_ref, gf_ref, wr_ref, br_ref,
                  h_ref, xp_ref, lg_ref):
    ya = ya_ref[...]
    ya_n = (ya * lax.rsqrt(jnp.mean(ya * ya, axis=-1, keepdims=True) + EPS) * ga_ref[...]).astype(BF16)
    y = _dot(yc_ref[...], woc_ref[...]) + _dot(ya_n, woa_ref[...])
    h = x_ref[...] + y
    h_ref[...] = h
    xn = h * lax.rsqrt(jnp.mean(h * h, axis=-1, keepdims=True) + EPS) * gf_ref[...]
    xn_hi = xn.astype(BF16)
    xn_hi32 = xn_hi.astype(F32)
    xn_lo = (xn - xn_hi32).astype(BF16)
    hh_hl = _dot(xn_hi, wr_ref[...])
    lg_ref[...] = hh_hl[:, :LANES] + hh_hl[:, LANES:] + _dot(xn_lo, wr_ref[:, :LANES]) + br_ref[...]
    half = D_MODEL // 2
    xp_ref[...] = _pack_bf16_pair(xn_hi32[:, :half], xn_hi32[:, half:])


def _merge(yc_n, ya, x, ga, w_out, g_ffn, wr_cat, b_r, tm):
    n = x.shape[0]
    row = lambda w: pl.BlockSpec((tm, w), lambda i: (i, 0))
    return pl.pallas_call(
        _merge_kernel,
        grid=(n // tm,),
        in_specs=[row(CONV_WIDTH), row(ATTN_WIDTH), row(D_MODEL), _resident((1, ATTN_WIDTH)),
                  _resident((CONV_WIDTH, D_MODEL), (0, 0)), _resident((ATTN_WIDTH, D_MODEL), (1, 0)),
                  _resident((1, D_MODEL)),
                  _resident((D_MODEL, 2 * LANES)), _resident((1, LANES))],
        out_specs=(row(D_MODEL), row(D_MODEL // 2), row(LANES)),
        out_shape=(jax.ShapeDtypeStruct((n, D_MODEL), F32),
                   jax.ShapeDtypeStruct((n, D_MODEL // 2), jnp.uint32),
                   jax.ShapeDtypeStruct((n, LANES), F32)),
        compiler_params=_params(("parallel",)),
        name="merge_out",
    )(yc_n, ya, x, ga, w_out, w_out, g_ffn, wr_cat, b_r)


def _route_kernel(lg_ref, info_ref, infot_ref, cnt_ref, carry_ref, *, tm):
    step = pl.program_id(0)

    @pl.when(step == 0)
    def _():
        carry_ref[...] = jnp.zeros_like(carry_ref)

    lg = lg_ref[...]
    lane = lax.broadcasted_iota(jnp.int32, lg.shape, 1)
    lanef = lane.astype(F32)
    big = jnp.float32(1e9)
    rmax = lambda v: jnp.max(v, axis=-1, keepdims=True)
    rmin = lambda v: jnp.min(v, axis=-1, keepdims=True)
    rsum = lambda v: jnp.sum(v, axis=-1, keepdims=True)

    is_g = (lane >= N_EXPERTS) & (lane < N_EXPERTS + N_GROUPS)
    gl = jnp.where(is_g, lg, NEG_BIG)
    gmax = rmax(gl)
    gsum = rsum(jnp.where(is_g, jnp.exp(gl - gmax), 0.0))
    pg_star = 1.0 / gsum
    g_idx = rmin(jnp.where(is_g & (gl == gmax), lanef - N_EXPERTS, big))

    e_lo = g_idx * EXPERTS_PER_GROUP
    is_e = (lanef >= e_lo) & (lanef < e_lo + EXPERTS_PER_GROUP)
    el = jnp.where(is_e, lg, NEG_BIG)
    m1 = rmax(el)
    i1 = rmin(jnp.where(is_e & (el == m1), lanef, big))
    sel1 = lanef == i1
    el2 = jnp.where(sel1, NEG_BIG, el)
    m2 = rmax(el2)
    i2 = rmin(jnp.where(is_e & (el2 == m2) & jnp.logical_not(sel1), lanef, big))
    sel2 = lanef == i2
    z = rsum(jnp.where(is_e, jnp.exp(el - m1), 0.0))
    p1 = 1.0 / z
    p2 = jnp.exp(m2 - m1) / z
    gate1 = pg_star * p1 / (p1 + p2)
    gate2 = pg_star * p2 / (p1 + p2)

    onehot = jnp.where(sel1 | sel2, 1.0, 0.0)
    r = lax.broadcasted_iota(jnp.int32, (tm, tm), 0)
    c = lax.broadcasted_iota(jnp.int32, (tm, tm), 1)
    tri = jnp.where(c < r, 1.0, 0.0).astype(BF16)
    before = _dot(tri, onehot.astype(BF16)) + carry_ref[...]
    rank1 = rsum(jnp.where(sel1, before, 0.0))
    rank2 = rsum(jnp.where(sel2, before, 0.0))
    carry_ref[...] = carry_ref[...] + jnp.sum(onehot, axis=0, keepdims=True)
    cnt_ref[...] = carry_ref[...]

    info = jnp.zeros_like(lg)
    for k, val in enumerate((i1, i2, rank1, rank2, gate1, gate2)):
        info = jnp.where(lane == k, val, info)
    info_ref[...] = info
    infot_ref[...] = info.T[0:8, :]


def _route(logits, tm):
    n = logits.shape[0]
    kernel = functools.partial(_route_kernel, tm=tm)
    return pl.pallas_call(
        kernel,
        grid=(n // tm,),
        in_specs=[pl.BlockSpec((tm, LANES), lambda i: (i, 0))],
        out_specs=(pl.BlockSpec((tm, LANES), lambda i: (i, 0)),
                   pl.BlockSpec((8, tm), lambda i: (0, i)),
                   pl.BlockSpec((1, LANES), lambda i: (0, 0))),
        out_shape=(jax.ShapeDtypeStruct((n, LANES), F32), jax.ShapeDtypeStruct((8, n), F32),
                   jax.ShapeDtypeStruct((1, LANES), F32)),
        scratch_shapes=[pltpu.VMEM((1, LANES), F32)],
        compiler_params=_params(("arbitrary",)),
        name="route",
    )(logits)


DISPATCH_CHUNK = 128


def _dispatch_kernel(dest_ref, xa_ref, xb_ref, xs_in_hbm, xs_hbm, sem):
    del xs_in_hbm
    n_all = dest_ref.shape[0] // 2

    def scatter(src_ref, t_off):
        def row_copy(t, k):
            return pltpu.make_async_copy(src_ref.at[pl.ds(t, 1)],
                                         xs_hbm.at[pl.ds(dest_ref[k * n_all + t_off + t], 1)], sem)

        def issue(c):
            def body(r, carry):
                row_copy(c * DISPATCH_CHUNK + r, 0).start(priority=0)
                row_copy(c * DISPATCH_CHUNK + r, 1).start(priority=1)
                return carry
            lax.fori_loop(0, DISPATCH_CHUNK, body, 0, unroll=8)

        def drain(c):
            def body(r, carry):
                row_copy(c * DISPATCH_CHUNK + r, 0).wait()
                row_copy(c * DISPATCH_CHUNK + r, 1).wait()
                return carry
            lax.fori_loop(0, DISPATCH_CHUNK, body, 0, unroll=8)

        n_chunks = src_ref.shape[0] // DISPATCH_CHUNK

        def chunk(c, carry):
            issue(c)

            @pl.when(c > 0)
            def _():
                drain(c - 1)
            return carry

        lax.fori_loop(0, n_chunks, chunk, 0)
        drain(n_chunks - 1)

    scatter(xa_ref, 0)
    scatter(xb_ref, xa_ref.shape[0])


def _dispatch(dest_flat, xa, xb, n_slots):
    w = xa.shape[1]
    xs0 = jnp.zeros((n_slots, w), xa.dtype)
    vmem = pl.BlockSpec(memory_space=pltpu.VMEM)
    grid_spec = pltpu.PrefetchScalarGridSpec(
        num_scalar_prefetch=1,
        grid=(1,),
        in_specs=[vmem, vmem, pl.BlockSpec(memory_space=pl.ANY)],
        out_specs=pl.BlockSpec(memory_space=pl.ANY),
        scratch_shapes=[pltpu.SemaphoreType.DMA(())],
    )
    return pl.pallas_call(
        _dispatch_kernel,
        grid_spec=grid_spec,
        out_shape=jax.ShapeDtypeStruct((n_slots, w), xa.dtype),
        input_output_aliases={3: 0},
        compiler_params=pltpu.CompilerParams(dimension_semantics=("arbitrary",),
                                             vmem_limit_bytes=VMEM_LIMIT, has_side_effects=True),
        name="dispatch",
    )(dest_flat, xa, xb, xs0)


CAST_CHUNK_ELEMS = 64 * 1024
WEIGHT_DMA_SPLIT = 4
WEIGHT_DMA_PRIORITY = 1


def _expert_kernel(be_ref, nu_ref, nx_ref, xs_ref, w1_hbm, w3_hbm, w2_hbm, y_ref,
                   w1f_ref, w3f_ref, w2f_ref, w1b_ref, w3b_ref, w2b_ref, sem):
    b = pl.program_id(0)
    active = b < nu_ref[0]
    new_expert = jnp.logical_or(b == 0, be_ref[b] != be_ref[jnp.maximum(b - 1, 0)])

    def fetch(e):
        copies = []
        for k, (src, dst) in enumerate(((w1_hbm, w1f_ref), (w3_hbm, w3f_ref), (w2_hbm, w2f_ref))):
            slab = dst.shape[0] // WEIGHT_DMA_SPLIT
            for c in range(WEIGHT_DMA_SPLIT):
                rows = pl.ds(c * slab, slab)
                copies.append(pltpu.make_async_copy(src.at[e, rows], dst.at[rows], sem.at[k]))
        return copies

    @pl.when(b == 0)
    def _():
        for copy in fetch(be_ref[0]):
            copy.start(priority=WEIGHT_DMA_PRIORITY)

    @pl.when(jnp.logical_and(active, new_expert))
    def _():
        for copy in fetch(be_ref[b]):
            copy.wait()
        for src, dst in ((w1f_ref, w1b_ref), (w3f_ref, w3b_ref), (w2f_ref, w2b_ref)):
            rows, cols = src.shape
            chunk = CAST_CHUNK_ELEMS // cols

            def cast_rows(c, carry, src=src, dst=dst, chunk=chunk):
                r0 = pl.multiple_of(c * chunk, chunk)
                dst[pl.ds(r0, chunk), :] = src[pl.ds(r0, chunk), :].astype(BF16)
                return carry

            lax.fori_loop(0, rows // chunk, cast_rows, 0)

        @pl.when(nx_ref[b] >= 0)
        def _():
            for copy in fetch(nx_ref[b]):
                copy.start(priority=WEIGHT_DMA_PRIORITY)

    @pl.when(active)
    def _():
        lo, hi = _unpack_bf16_pair(xs_ref[...])
        half = D_MODEL // 2
        h1 = _dot(lo, w1b_ref[:half, :]) + _dot(hi, w1b_ref[half:, :])
        h3 = _dot(lo, w3b_ref[:half, :]) + _dot(hi, w3b_ref[half:, :])
        h = (h1 * jax.nn.sigmoid(h1) * h3).astype(BF16)
        y_ref[...] = _dot(h, w2b_ref[...])

    @pl.when(jnp.logical_not(active))
    def _():
        y_ref[...] = jnp.zeros_like(y_ref)


def _experts(block_e, n_used, next_e, xs, w1, w3, w2):
    n_slots = xs.shape[0]
    nb = n_slots // MOE_BLOCK
    blk = lambda b, be, nu, nx: (jnp.minimum(b, nu[0] - 1), 0)
    hbm = pl.BlockSpec(memory_space=pl.ANY)
    grid_spec = pltpu.PrefetchScalarGridSpec(
        num_scalar_prefetch=3,
        grid=(nb,),
        in_specs=[pl.BlockSpec((MOE_BLOCK, D_MODEL // 2), blk), hbm, hbm, hbm],
        out_specs=pl.BlockSpec((MOE_BLOCK, D_MODEL), lambda b, be, nu, nx: (b, 0)),
        scratch_shapes=[pltpu.VMEM((D_MODEL, D_EXPERT), F32), pltpu.VMEM((D_MODEL, D_EXPERT), F32),
                        pltpu.VMEM((D_EXPERT, D_MODEL), F32),
                        pltpu.VMEM((D_MODEL, D_EXPERT), BF16), pltpu.VMEM((D_MODEL, D_EXPERT), BF16),
                        pltpu.VMEM((D_EXPERT, D_MODEL), BF16),
                        pltpu.SemaphoreType.DMA((3,))],
    )
    return pl.pallas_call(
        _expert_kernel,
        grid_spec=grid_spec,
        out_shape=jax.ShapeDtypeStruct((n_slots, D_MODEL), F32),
        compiler_params=_params(("arbitrary",)),
        name="experts",
    )(block_e, n_used, next_e, xs, w1, w3, w2)


PLE_SPLIT = 4
GATHER_ISSUE_SLABS = 2


def _combine_ple_kernel(dest_ref, h_ref, info_ref, p_ref, g_ref, *rest, tm, t_off):
    wpg_refs = rest[:PLE_SPLIT]
    wpp_ref, yb_hbm, o_ref, buf_ref, sem = rest[PLE_SPLIT:]
    i = pl.program_id(0)
    last = pl.num_programs(0) - 1
    n_all = dest_ref.shape[0] // 2
    slot = i % 2

    def row_copy(step, into, r, k):
        t = t_off + step * tm + r
        return pltpu.make_async_copy(yb_hbm.at[pl.ds(dest_ref[k * n_all + t], 1)],
                                     buf_ref.at[into, k, pl.ds(r, 1)], sem.at[into])

    def drain(into):
        def body(r, carry):
            for k in range(2):
                pltpu.make_async_copy(yb_hbm.at[pl.ds(0, 1)], buf_ref.at[into, k, pl.ds(r, 1)],
                                      sem.at[into]).wait()
            return carry
        lax.fori_loop(0, tm, body, 0, unroll=8)

    @pl.when(i == 0)
    def _():
        def body(r, carry):
            row_copy(0, 0, r, 0).start()
            row_copy(0, 0, r, 1).start()
            return carry
        lax.fori_loop(0, tm, body, 0, unroll=8)

    drain(slot)
    h2 = h_ref[...] + (info_ref[:, 4:5] * buf_ref[slot, 0] + info_ref[:, 5:6] * buf_ref[slot, 1])
    hn = (h2 * lax.rsqrt(jnp.mean(h2 * h2, axis=-1, keepdims=True) + EPS) * g_ref[...]).astype(BF16)
    pb = p_ref[...].astype(BF16)

    nxt = jnp.minimum(i + 1, last)
    ch = D_MODEL // PLE_SPLIT
    per = tm // GATHER_ISSUE_SLABS
    for k in range(PLE_SPLIT):
        for r in range(k * per, (k + 1) * per if k < GATHER_ISSUE_SLABS else k * per):
            row_copy(nxt, 1 - slot, r, 0).start(priority=0)
            row_copy(nxt, 1 - slot, r, 1).start(priority=1)
        cs = slice(k * ch, (k + 1) * ch)
        gate = jax.nn.sigmoid(_dot(hn, wpg_refs[k][...]))
        o_ref[:, cs] = h2[:, cs] + gate * _dot(pb, wpp_ref[:, cs])

    @pl.when(i == last)
    def _():
        drain(1 - slot)


def _combine_ple(dest_flat, h, info, yb, p, g_ple, w_pg, w_pp, t_off, tm):
    n = h.shape[0]
    kernel = functools.partial(_combine_ple_kernel, tm=tm, t_off=t_off)
    ob = t_off // tm
    row = lambda w: pl.BlockSpec((tm, w), lambda i, d: (i, 0))
    grid_spec = pltpu.PrefetchScalarGridSpec(
        num_scalar_prefetch=1,
        grid=(n // tm,),
        in_specs=[row(D_MODEL), pl.BlockSpec((tm, LANES), lambda i, d: (i + ob, 0)), row(PLE_DIM),
                  _resident((1, D_MODEL))]
        + [_resident((D_MODEL, D_MODEL // PLE_SPLIT), (0, k)) for k in range(PLE_SPLIT)]
        + [_resident((PLE_DIM, D_MODEL)), pl.BlockSpec(memory_space=pl.ANY)],
        out_specs=row(D_MODEL),
        scratch_shapes=[pltpu.VMEM((2, 2, tm, D_MODEL), F32), pltpu.SemaphoreType.DMA((2,))],
    )
    return pl.pallas_call(
        kernel,
        grid_spec=grid_spec,
        out_shape=jax.ShapeDtypeStruct((n, D_MODEL), F32),
        compiler_params=_params(("arbitrary",)),
        name="combine_ple",
    )(dest_flat, h, info, p, g_ple, *([w_pg] * PLE_SPLIT), w_pp, yb)


def _mixer_tokens(x2d, wts, tm):
    return _in_proj(x2d, wts["g_mix"], wts["w_parts"], wts["w_f"], wts["b_f"],
                    wts["q_gain"], wts["k_gain"], tm)


def kernel(x_prompt, x_sample, cache_k, cache_v, cache_logf, cache_conv, p_prompt, p_sample,
           g_mix, w_in, b_f, q_gain, k_gain, w_dw, b_dw, ln_g, ln_b, gc, ga, w_out,
           g_ffn, w_router_g, b_router_g, w_router_e, b_router_e, w1, w3, w2,
           g_ple, w_pg, w_pp):
    batch, seq, _ = x_prompt.shape
    dec_batch, t_new, _ = x_sample.shape
    past = cache_k.shape[2]
    n_p = batch * seq
    n_s = dec_batch * t_new
    n_all = n_p + n_s
    tm = 256
    li = 0

    w_in_l = w_in[li]
    pad_lanes = lambda a: jnp.pad(a, ((0, 0), (0, LANES - a.shape[1])))
    row2d = lambda a: a.reshape(1, -1)
    wts = {
        "g_mix": row2d(g_mix[li]),
        "w_parts": [w_in_l[:, c:c + CONV_WIDTH].astype(BF16) for c in range(0, MAIN_COLS, CONV_WIDTH)],
        "w_f": pad_lanes(w_in_l[:, MAIN_COLS:]).astype(BF16),
        "b_f": pad_lanes(row2d(b_f[li])),
        "q_gain": row2d(q_gain[li]),
        "k_gain": row2d(k_gain[li]),
    }
    w_dw_p = jnp.pad(w_dw[li], ((0, CONV_HALO - CONV_KERNEL), (0, 0)))
    w_out_b = w_out[li].astype(BF16)
    w_r = pad_lanes(jnp.concatenate([w_router_e[li], w_router_g[li]], axis=1))
    wr_hi = w_r.astype(BF16)
    wr_lo = (w_r - wr_hi.astype(F32)).astype(BF16)
    b_r = pad_lanes(row2d(jnp.concatenate([b_router_e[li], b_router_g[li]])))
    w_pg_b, w_pp_b = w_pg[li].astype(BF16), w_pp[li].astype(BF16)

    xp = x_prompt.reshape(n_p, D_MODEL)
    u_p, q_p, k_p, kb_p, v_p, _, vbt_p, lf_p = _mixer_tokens(xp, wts, tm)
    lf_p_row = lf_p.reshape(batch, seq, N_HEADS).transpose(0, 2, 1)
    f_p_row = _cumsum_lanes(lf_p_row.reshape(batch * N_HEADS, seq)).reshape(batch, N_HEADS, seq)
    f_p_col = f_p_row.transpose(0, 2, 1).reshape(n_p, N_HEADS)
    ctx_p = jnp.zeros((batch, CONV_HALO, CONV_WIDTH), F32)
    conv_args = (w_dw_p, row2d(b_dw[li]), row2d(ln_g[li]), row2d(ln_b[li]), row2d(gc[li]))
    yc_p = _conv_module(u_p, ctx_p, *conv_args, batch, seq, tm)
    qa_p, ka_p = _attn_aug(f_p_col, tm)
    ya_p = _attn_prompt(q_p, qa_p, kb_p, ka_p, vbt_p, batch, seq, 256, 256)
    merge_args = (row2d(ga[li]), w_out_b, row2d(g_ffn[li]), jnp.concatenate([wr_hi, wr_lo], axis=1), b_r)
    h_p, xpk_p, lg_p = _merge(yc_p, ya_p, xp, *merge_args, tm)

    xs_ = x_sample.reshape(n_s, D_MODEL)
    u_s, q_s, k_s, kb_s, v_s, vb_s, _, lf_s = _mixer_tokens(xs_, wts, tm)
    clf_row = cache_logf[li].transpose(0, 2, 1).reshape(dec_batch * N_HEADS, past)
    fc_row = _cumsum_lanes(clf_row).reshape(dec_batch, N_HEADS, past)
    lf_s_row = lf_s.reshape(dec_batch, t_new, N_HEADS).transpose(0, 2, 1).reshape(dec_batch * N_HEADS, t_new)
    fn_row = _cumsum_lanes(jnp.pad(lf_s_row, ((0, 0), (0, LANES - t_new)))).reshape(dec_batch, N_HEADS, LANES)
    fn_col = fn_row[:, :, :t_new].transpose(0, 2, 1).reshape(n_s, N_HEADS)
    ctx_s = jnp.pad(cache_conv[li], ((0, 0), (CONV_HALO - CONV_STATE, 0), (0, 0)))
    yc_s = _conv_module(u_s, ctx_s, *conv_args, dec_batch, t_new, t_new)
    ya_s = _attn_sample(q_s, kb_s, vb_s,
                        cache_k[li], cache_v[li],
                        fc_row, fn_row, fn_col, dec_batch, t_new, past)
    h_s, xpk_s, lg_s = _merge(yc_s, ya_s, xs_, *merge_args, tm)

    info, info_t, counts = _route(jnp.concatenate([lg_p, lg_s], axis=0), tm)
    counts = counts[0, :N_EXPERTS].astype(jnp.int32)
    bcounts = (counts + MOE_BLOCK - 1) // MOE_BLOCK
    bends = jnp.cumsum(bcounts)
    pstarts = (bends - bcounts) * MOE_BLOCK
    n_rows = n_all * 2
    nb = -(-(n_rows + N_EXPERTS * (MOE_BLOCK - 1)) // MOE_BLOCK)
    e_idx = info_t[0:2].astype(jnp.int32)
    expert_ids = jnp.arange(N_EXPERTS, dtype=jnp.int32)[:, None, None]
    seg_start = jnp.sum(jnp.where(e_idx[None] == expert_ids, pstarts[:, None, None], 0), axis=0)
    dest = (seg_start + info_t[2:4].astype(jnp.int32)).reshape(n_rows)
    block_e = jnp.minimum(jnp.sum(bends[None, :] <= jnp.arange(nb, dtype=jnp.int32)[:, None], axis=1),
                          N_EXPERTS - 1).astype(jnp.int32)
    n_used = bends[N_EXPERTS - 1:].astype(jnp.int32)
    xs_sorted = _dispatch(dest, xpk_p, xpk_s, nb * MOE_BLOCK)
    seg_end = jnp.sum(jnp.where(block_e[None, :] == expert_ids[:, :, 0], bends[:, None], 0), axis=0)
    next_e = jnp.where(seg_end < n_used[0], block_e[jnp.minimum(seg_end, nb - 1)], -1).astype(jnp.int32)
    yb = _experts(block_e, n_used, next_e, xs_sorted, w1[li], w3[li], w2[li])

    ple_args = (row2d(g_ple[li]), w_pg_b, w_pp_b)
    y_p = _combine_ple(dest, h_p, info, yb, p_prompt[li].reshape(n_p, PLE_DIM), *ple_args, 0, tm)
    y_s = _combine_ple(dest, h_s, info, yb, p_sample[li].reshape(n_s, PLE_DIM), *ple_args, n_p, tm)

    heads = lambda a, b, t: a.reshape(1, b, t, N_HEADS, HEAD_DIM)
    return (
        y_p.reshape(batch, seq, D_MODEL),
        y_s.reshape(dec_batch, t_new, D_MODEL),
        heads(k_p, batch, seq), heads(v_p, batch, seq),
        lf_p.reshape(1, batch, seq, N_HEADS),
        u_p.reshape(batch, seq, CONV_WIDTH)[None, :, seq - CONV_STATE:, :],
        heads(k_s, dec_batch, t_new), heads(v_s, dec_batch, t_new),
        lf_s.reshape(1, dec_batch, t_new, N_HEADS),
        u_s.reshape(dec_batch, t_new, CONV_WIDTH)[None, :, t_new - CONV_STATE:, :],
    )
```

```python
import functools

import jax
import jax.numpy as jnp
from jax import lax
from jax.experimental import pallas as pl
from jax.experimental.pallas import tpu as pltpu

D_MODEL = 2048
CONV_WIDTH = 1024
ATTN_WIDTH = 1024
HEAD_DIM = 128
N_HEADS = 8
CONV_KERNEL = 31
CONV_STATE = CONV_KERNEL - 1
N_GROUPS = 4
EXPERTS_PER_GROUP = 8
N_EXPERTS = 32
D_EXPERT = 512
PLE_DIM = 256
MOE_BLOCK = 256
EPS = 1e-6
MAIN_COLS = 2 * CONV_WIDTH + 3 * ATTN_WIDTH

LANES = 128
CONV_HALO = 32
VMEM_LIMIT = 56 * 1024 * 1024

F32 = jnp.float32
BF16 = jnp.bfloat16
NEG_BIG = -1e30
LOG2E = 1.4426950408889634


def _dot(a, b):
    return jnp.dot(a, b, preferred_element_type=F32)


def _params(sem):
    return pltpu.CompilerParams(dimension_semantics=sem, vmem_limit_bytes=VMEM_LIMIT)


def _resident(shape, index=None):
    index = (0,) * len(shape) if index is None else index
    return pl.BlockSpec(shape, lambda *_: index, pipeline_mode=pl.Buffered(1))


def _inproj_kernel(x_ref, g_ref, wval_ref, wgate_ref, wq_ref, wk_ref, wv_ref, wf_ref, bf_ref, qg_ref, kg_ref,
                   u_ref, q_ref, k_ref, kb_ref, v_ref, vb_ref, vbt_ref, lf_ref):
    x = x_ref[...]
    ms = jnp.mean(x * x, axis=-1, keepdims=True)
    a = (x * lax.rsqrt(ms + EPS) * g_ref[...]).astype(BF16)

    ch = 256
    for c in range(0, CONV_WIDTH, ch):
        val = _dot(a, wval_ref[:, c:c + ch])
        gate = _dot(a, wgate_ref[:, c:c + ch])
        u_ref[:, c:c + ch] = val * jax.nn.sigmoid(gate)

    def head_norm(z, gain):
        return z * lax.rsqrt(jnp.mean(z * z, axis=-1, keepdims=True) + EPS) * gain

    scale = LOG2E * HEAD_DIM ** -0.5
    for c in range(0, ATTN_WIDTH, ch):
        zq = _dot(a, wq_ref[:, c:c + ch])
        zk = _dot(a, wk_ref[:, c:c + ch])
        zv = _dot(a, wv_ref[:, c:c + ch])
        for s in range(0, ch, HEAD_DIM):
            qn = head_norm(zq[:, s:s + HEAD_DIM], qg_ref[...])
            kn = head_norm(zk[:, s:s + HEAD_DIM], kg_ref[...])
            q_ref[:, c + s:c + s + HEAD_DIM] = (qn * scale).astype(BF16)
            k_ref[:, c + s:c + s + HEAD_DIM] = kn
            kb_ref[:, c + s:c + s + HEAD_DIM] = kn.astype(BF16)
        v_ref[:, c:c + ch] = zv
        vb_ref[:, c:c + ch] = zv.astype(BF16)
        vbt_ref[c:c + ch, :] = zv.T.astype(BF16)

    f = _dot(a, wf_ref[...]) + bf_ref[...]
    lf = jnp.minimum(f, 0.0) - jnp.log1p(jnp.exp(-jnp.abs(f)))
    lf_ref[...] = lf[:, :N_HEADS]


def _in_proj(x, g_mix, w_parts, w_f, b_f, q_gain, k_gain, tm):
    n = x.shape[0]
    row = lambda w: pl.BlockSpec((tm, w), lambda i: (i, 0))
    out_shape = (
        jax.ShapeDtypeStruct((n, CONV_WIDTH), F32),
        jax.ShapeDtypeStruct((n, ATTN_WIDTH), BF16),
        jax.ShapeDtypeStruct((n, ATTN_WIDTH), F32),
        jax.ShapeDtypeStruct((n, ATTN_WIDTH), BF16),
        jax.ShapeDtypeStruct((n, ATTN_WIDTH), F32),
        jax.ShapeDtypeStruct((n, ATTN_WIDTH), BF16),
        jax.ShapeDtypeStruct((ATTN_WIDTH, n), BF16),
        jax.ShapeDtypeStruct((n, N_HEADS), F32),
    )
    return pl.pallas_call(
        _inproj_kernel,
        grid=(n // tm,),
        in_specs=[row(D_MODEL), _resident((1, D_MODEL))] + [_resident((D_MODEL, CONV_WIDTH))] * 5 + [
                  _resident((D_MODEL, LANES)), _resident((1, LANES)),
                  _resident((1, HEAD_DIM)), _resident((1, HEAD_DIM))],
        out_specs=(row(CONV_WIDTH), row(ATTN_WIDTH), row(ATTN_WIDTH), row(ATTN_WIDTH),
                   row(ATTN_WIDTH), row(ATTN_WIDTH),
                   pl.BlockSpec((ATTN_WIDTH, tm), lambda i: (0, i)), row(N_HEADS)),
        out_shape=out_shape,
        compiler_params=_params(("parallel",)),
        name="in_proj",
    )(x, g_mix, *w_parts, w_f, b_f, q_gain, k_gain)


def _cumsum_kernel(x_ref, o_ref):
    x = x_ref[...]
    width = x.shape[1]
    lane = lax.broadcasted_iota(jnp.int32, x.shape, 1)
    s = 1
    while s < width:
        x = x + jnp.where(lane >= s, pltpu.roll(x, s, axis=1), 0.0)
        s *= 2
    o_ref[...] = x


def _cumsum_lanes(x):
    return pl.pallas_call(
        _cumsum_kernel,
        out_shape=jax.ShapeDtypeStruct(x.shape, F32),
        name="cumsum",
    )(x)


def _conv_kernel(u_ref, halo_ref, ctx_ref, w_ref, bdw_ref, lng_ref, lnb_ref, gc_ref,
                 o_ref, ext_ref, y_ref, *, tm):
    i = pl.program_id(1)

    @pl.when(i == 0)
    def _():
        ext_ref[0:CONV_HALO, :] = ctx_ref[0]

    @pl.when(i > 0)
    def _():
        ext_ref[0:CONV_HALO, :] = halo_ref[...]

    ext_ref[CONV_HALO:CONV_HALO + tm, :] = u_ref[...]

    rows = min(64, tm)
    ch = LANES
    sub = 8
    wlen = rows + CONV_HALO
    first = CONV_HALO - CONV_STATE

    def conv_rows(r, carry):
        r0 = pl.multiple_of(r * rows, rows)
        for c in range(0, CONV_WIDTH, ch):
            acc = jnp.zeros((rows, ch), F32)
            win = ext_ref[pl.ds(r0, wlen), c:c + ch]
            for rho in range(sub):
                sh = win if rho == 0 else pltpu.roll(win, wlen - rho, axis=0)
                for a in range(wlen // sub):
                    j = sub * a + rho - first
                    if 0 <= j < CONV_KERNEL:
                        acc = acc + sh[sub * a:sub * a + rows] * w_ref[j:j + 1, c:c + ch]
            y_ref[pl.ds(r0, rows), c:c + ch] = acc
        return carry

    lax.fori_loop(0, tm // rows, conv_rows, 0)

    nrows = min(128, tm)

    def norm_rows(r, carry):
        r0 = pl.multiple_of(r * nrows, nrows)
        y = y_ref[pl.ds(r0, nrows), :] + bdw_ref[...]
        mu = jnp.mean(y, axis=-1, keepdims=True)
        yc = y - mu
        var = jnp.mean(yc * yc, axis=-1, keepdims=True)
        z = yc * lax.rsqrt(var + EPS) * lng_ref[...] + lnb_ref[...]
        s = z * jax.nn.sigmoid(z)
        ms = jnp.mean(s * s, axis=-1, keepdims=True)
        o_ref[pl.ds(r0, nrows), :] = (s * lax.rsqrt(ms + EPS) * gc_ref[...]).astype(BF16)
        return carry

    lax.fori_loop(0, tm // nrows, norm_rows, 0)


def _conv_module(u, ctx, w_dw, b_dw, ln_g, ln_b, gc, batch, seq, tm):
    nt = seq // tm
    hb = tm // CONV_HALO
    kernel = functools.partial(_conv_kernel, tm=tm)
    return pl.pallas_call(
        kernel,
        grid=(batch, nt),
        in_specs=[
            pl.BlockSpec((tm, CONV_WIDTH), lambda b, i: (b * nt + i, 0)),
            pl.BlockSpec((CONV_HALO, CONV_WIDTH),
                         lambda b, i: (jnp.maximum((b * nt + i) * hb - 1, 0), 0)),
            pl.BlockSpec((1, CONV_HALO, CONV_WIDTH), lambda b, i: (b, 0, 0)),
            _resident((CONV_HALO, CONV_WIDTH)),
            _resident((1, CONV_WIDTH)), _resident((1, CONV_WIDTH)),
            _resident((1, CONV_WIDTH)), _resident((1, CONV_WIDTH)),
        ],
        out_specs=pl.BlockSpec((tm, CONV_WIDTH), lambda b, i: (b * nt + i, 0)),
        out_shape=jax.ShapeDtypeStruct((batch * seq, CONV_WIDTH), BF16),
        scratch_shapes=[pltpu.VMEM((CONV_HALO + tm, CONV_WIDTH), F32),
                        pltpu.VMEM((tm, CONV_WIDTH), F32)],
        compiler_params=_params(("parallel", "arbitrary")),
        name="conv_module",
    )(u, u, ctx, w_dw, b_dw, ln_g, ln_b, gc)


def _qk(q, k):
    return lax.dot_general(q, k, (((1,), (1,)), ((), ())), preferred_element_type=F32)


AUG_TERMS = 3
AUG_STRIDE = LANES // N_HEADS
QK_AHEAD = 4


def _aug_kernel(f_ref, qa_ref, ka_ref):
    f = f_ref[...] * LOG2E
    tm = f.shape[1]
    row = lax.broadcasted_iota(jnp.int32, (LANES, tm), 0)
    sub = row % AUG_STRIDE
    qa = jnp.where((sub >= AUG_TERMS) & (sub < 2 * AUG_TERMS), 1.0, 0.0)
    ka = jnp.where(sub < AUG_TERMS, 1.0, 0.0)
    for h in range(N_HEADS):
        rest = f[h:h + 1, :]
        for t in range(AUG_TERMS):
            piece = rest.astype(BF16).astype(F32)
            rest = rest - piece
            qa = jnp.where(row == h * AUG_STRIDE + t, piece, qa)
            ka = jnp.where(row == h * AUG_STRIDE + AUG_TERMS + t, -piece, ka)
    qa_ref[...] = qa.T.astype(BF16)
    ka_ref[...] = ka.T.astype(BF16)


def _attn_aug(f_row, batch, seq, tm):
    nt = seq // tm
    out = pl.BlockSpec((tm, LANES), lambda b, i: (b * nt + i, 0))
    return pl.pallas_call(
        _aug_kernel,
        grid=(batch, nt),
        in_specs=[pl.BlockSpec((N_HEADS, tm), lambda b, i: (b, i))],
        out_specs=(out, out),
        out_shape=(jax.ShapeDtypeStruct((batch * seq, LANES), BF16),
                   jax.ShapeDtypeStruct((batch * seq, LANES), BF16)),
        compiler_params=_params(("parallel", "parallel")),
        name="attn_aug",
    )(f_row)


def _attn_prompt_kernel(q_ref, qa_ref, k_ref, ka_ref, vt_ref, o_ref, m_ref, l_ref, acc_ref, qt_ref,
                        sp_ref, *, tq, tk):
    i = pl.program_id(1)
    key = lax.broadcasted_iota(jnp.int32, (tk, tq), 0)
    qry = lax.broadcasted_iota(jnp.int32, (tk, tq), 1)

    m_ref[...] = jnp.full(m_ref.shape, NEG_BIG, F32)
    l_ref[...] = jnp.zeros(l_ref.shape, F32)
    acc_ref[...] = jnp.zeros(acc_ref.shape, F32)
    aug_lane = lax.broadcasted_iota(jnp.int32, (tq, LANES), 1)
    for h in range(N_HEADS):
        hs = slice(h * HEAD_DIM, (h + 1) * HEAD_DIM)
        own = (aug_lane >= h * AUG_STRIDE) & (aug_lane < (h + 1) * AUG_STRIDE)
        qa = jnp.where(own, qa_ref[...], jnp.zeros_like(qa_ref))
        qt_ref[h] = jnp.concatenate([q_ref[:, hs], qa], axis=1).T

    def scores(ks, h):
        hs = slice(h * HEAD_DIM, (h + 1) * HEAD_DIM)
        kf = jnp.concatenate([k_ref[pl.ds(ks, tk), hs], ka_ref[pl.ds(ks, tk), :]], axis=1)
        return _dot(kf, qt_ref[h])

    def tile_step(ks, ks_next, masked):
        pending = [sp_ref[a] for a in range(QK_AHEAD)]
        for h in range(N_HEADS):
            hs = slice(h * HEAD_DIM, (h + 1) * HEAD_DIM)
            s = pending.pop(0)
            if h + QK_AHEAD < N_HEADS:
                pending.append(scores(ks, h + QK_AHEAD))
            elif ks_next is not None:
                pending.append(scores(ks_next, h + QK_AHEAD - N_HEADS))
            if masked:
                s = jnp.where(key <= qry, s, -jnp.inf)
            m = m_ref[h]
            m_new = jnp.maximum(m, jnp.max(s, axis=0, keepdims=True))
            alpha = jnp.exp2(m - m_new)
            p = jnp.exp2(s - m_new)
            m_ref[h] = m_new
            l_ref[h] = alpha * l_ref[h] + jnp.sum(p, axis=0, keepdims=True)
            pv = _dot(vt_ref[hs, pl.ds(ks, tk)], p.astype(BF16))
            acc_ref[h] = alpha * acc_ref[h] + pv
        for a, s in enumerate(pending):
            sp_ref[a] = s

    def body(j, carry):
        tile_step(pl.multiple_of(j * tk, tk), pl.multiple_of((j + 1) * tk, tk), False)
        return carry

    for a in range(QK_AHEAD):
        sp_ref[a] = scores(0, a)
    lax.fori_loop(0, i, body, 0)
    tile_step(pl.multiple_of(i * tk, tk), None, True)
    for h in range(N_HEADS):
        o_ref[:, h * HEAD_DIM:(h + 1) * HEAD_DIM] = (acc_ref[h] / l_ref[h]).T


def _attn_prompt(q, qa, kb, ka, vbt, batch, seq, tq, tk):
    nq = seq // tq
    kernel = functools.partial(_attn_prompt_kernel, tq=tq, tk=tk)
    qblk = pl.BlockSpec((tq, ATTN_WIDTH), lambda b, i: (b * nq + i, 0))
    kblk = pl.BlockSpec((seq, ATTN_WIDTH), lambda b, i: (b, 0), pipeline_mode=pl.Buffered(1))
    vblk = pl.BlockSpec((ATTN_WIDTH, seq), lambda b, i: (0, b), pipeline_mode=pl.Buffered(1))
    return pl.pallas_call(
        kernel,
        grid=(batch, nq),
        in_specs=[qblk, pl.BlockSpec((tq, LANES), lambda b, i: (b * nq + i, 0)), kblk,
                  pl.BlockSpec((seq, LANES), lambda b, i: (b, 0), pipeline_mode=pl.Buffered(1)), vblk],
        out_specs=pl.BlockSpec((tq, ATTN_WIDTH), lambda b, i: (b * nq + i, 0)),
        out_shape=jax.ShapeDtypeStruct((batch * seq, ATTN_WIDTH), F32),
        scratch_shapes=[pltpu.VMEM((N_HEADS, 1, tq), F32), pltpu.VMEM((N_HEADS, 1, tq), F32),
                        pltpu.VMEM((N_HEADS, HEAD_DIM, tq), F32),
                        pltpu.VMEM((N_HEADS, 2 * HEAD_DIM, tq), BF16),
                        pltpu.VMEM((QK_AHEAD, tk, tq), F32)],
        compiler_params=_params(("parallel", "arbitrary")),
        name="attn_prompt",
    )(q, qa, kb, ka, vbt)


def _attn_sample_kernel(q_ref, kn_ref, vn_ref, ck_hbm, cv_hbm, fc_ref, fnrow_ref, fncol_ref,
                        o_ref, kbuf_ref, vbuf_ref, sem, *, t_new, past):
    b = pl.program_id(0)
    slot = b % 2

    def fetch(batch_idx, into):
        copies = []
        for h in range(N_HEADS):
            copies.append(pltpu.make_async_copy(ck_hbm.at[batch_idx, :, h, :], kbuf_ref.at[into, h],
                                                sem.at[into, 0]))
            copies.append(pltpu.make_async_copy(cv_hbm.at[batch_idx, :, h, :], vbuf_ref.at[into, h],
                                                sem.at[into, 1]))
        return copies

    @pl.when(b == 0)
    def _():
        for copy in fetch(0, 0):
            copy.start()

    @pl.when(b + 1 < pl.num_programs(0))
    def _():
        for copy in fetch(b + 1, 1 - slot):
            copy.start()

    for copy in fetch(b, slot):
        copy.wait()

    row = lax.broadcasted_iota(jnp.int32, (t_new, t_new), 0)
    col = lax.broadcasted_iota(jnp.int32, (t_new, t_new), 1)
    causal = col <= row
    for h in range(N_HEADS):
        hs = slice(h * HEAD_DIM, (h + 1) * HEAD_DIM)
        q = q_ref[:, hs]
        fn_q = fncol_ref[:, h:h + 1]
        fc = fc_ref[0, h:h + 1, :]
        fc_last = fc[:, past - 1:past]
        s_c = _qk(q, kbuf_ref[slot, h].astype(BF16)) + ((fc_last + fn_q) - fc) * LOG2E
        fn_k = fnrow_ref[0, h:h + 1, 0:t_new]
        s_n = _qk(q, kn_ref[:, hs]) + (fn_q - fn_k) * LOG2E
        s_n = jnp.where(causal, s_n, -jnp.inf)
        m = jnp.maximum(jnp.max(s_c, axis=-1, keepdims=True), jnp.max(s_n, axis=-1, keepdims=True))
        p_c = jnp.exp2(s_c - m)
        p_n = jnp.exp2(s_n - m)
        l = jnp.sum(p_c, axis=-1, keepdims=True) + jnp.sum(p_n, axis=-1, keepdims=True)
        acc = _dot(p_c.astype(BF16), vbuf_ref[slot, h].astype(BF16)) + _dot(p_n.astype(BF16), vn_ref[:, hs])
        o_ref[:, hs] = acc / l


def _attn_sample(q, kb, vb, cache_k, cache_v, fc_row, fn_row, fn_col, batch, t_new, past):
    kernel = functools.partial(_attn_sample_kernel, t_new=t_new, past=past)
    tok = lambda w: pl.BlockSpec((t_new, w), lambda b: (b, 0))
    return pl.pallas_call(
        kernel,
        grid=(batch,),
        in_specs=[
            tok(ATTN_WIDTH), tok(ATTN_WIDTH), tok(ATTN_WIDTH),
            pl.BlockSpec(memory_space=pl.ANY),
            pl.BlockSpec(memory_space=pl.ANY),
            pl.BlockSpec((1, N_HEADS, past), lambda b: (b, 0, 0)),
            pl.BlockSpec((1, N_HEADS, LANES), lambda b: (b, 0, 0)),
            tok(N_HEADS),
        ],
        out_specs=tok(ATTN_WIDTH),
        out_shape=jax.ShapeDtypeStruct((batch * t_new, ATTN_WIDTH), F32),
        scratch_shapes=[pltpu.VMEM((2, N_HEADS, past, HEAD_DIM), F32),
                        pltpu.VMEM((2, N_HEADS, past, HEAD_DIM), F32),
                        pltpu.SemaphoreType.DMA((2, 2))],
        compiler_params=_params(("arbitrary",)),
        name="attn_sample",
    )(q, kb, vb, cache_k, cache_v, fc_row, fn_row, fn_col)


def _pack_bf16_pair(lo, hi):
    lo_bits = lax.bitcast_convert_type(lo, jnp.uint32) >> 16
    hi_bits = lax.bitcast_convert_type(hi, jnp.uint32) & jnp.uint32(0xFFFF0000)
    return lo_bits | hi_bits


def _unpack_bf16_pair(w):
    lo = lax.bitcast_convert_type(w << 16, F32).astype(BF16)
    hi = lax.bitcast_convert_type(w & jnp.uint32(0xFFFF0000), F32).astype(BF16)
    return lo, hi


def _merge_kernel(yc_ref, ya_ref, x_ref, ga_ref, woc_ref, woa_ref, gf_ref, wr_ref, br_ref,
                  h_ref, xp_ref, lg_ref):
    ya = ya_ref[...]
    ya_n = (ya * lax.rsqrt(jnp.mean(ya * ya, axis=-1, keepdims=True) + EPS) * ga_ref[...]).astype(BF16)
    y = _dot(yc_ref[...], woc_ref[...]) + _dot(ya_n, woa_ref[...])
    h = x_ref[...] + y
    h_ref[...] = h
    xn = h * lax.rsqrt(jnp.mean(h * h, axis=-1, keepdims=True) + EPS) * gf_ref[...]
    xn_hi = xn.astype(BF16)
    xn_hi32 = xn_hi.astype(F32)
    xn_lo = (xn - xn_hi32).astype(BF16)
    hh_hl = _dot(xn_hi, wr_ref[...])
    lg_ref[...] = hh_hl[:, :LANES] + hh_hl[:, LANES:] + _dot(xn_lo, wr_ref[:, :LANES]) + br_ref[...]
    half = D_MODEL // 2
    xp_ref[...] = _pack_bf16_pair(xn_hi32[:, :half], xn_hi32[:, half:])


def _merge(yc_n, ya, x, ga, w_out, g_ffn, wr_cat, b_r, tm):
    n = x.shape[0]
    row = lambda w: pl.BlockSpec((tm, w), lambda i: (i, 0))
    return pl.pallas_call(
        _merge_kernel,
        grid=(n // tm,),
        in_specs=[row(CONV_WIDTH), row(ATTN_WIDTH), row(D_MODEL), _resident((1, ATTN_WIDTH)),
                  _resident((CONV_WIDTH, D_MODEL), (0, 0)), _resident((ATTN_WIDTH, D_MODEL), (1, 0)),
                  _resident((1, D_MODEL)),
                  _resident((D_MODEL, 2 * LANES)), _resident((1, LANES))],
        out_specs=(row(D_MODEL), row(D_MODEL // 2), row(LANES)),
        out_shape=(jax.ShapeDtypeStruct((n, D_MODEL), F32),
                   jax.ShapeDtypeStruct((n, D_MODEL // 2), jnp.uint32),
                   jax.ShapeDtypeStruct((n, LANES), F32)),
        compiler_params=_params(("parallel",)),
        name="merge_out",
    )(yc_n, ya, x, ga, w_out, w_out, g_ffn, wr_cat, b_r)


def _route_kernel(lg_ref, info_ref, infot_ref, cnt_ref, carry_ref, *, tm):
    step = pl.program_id(0)

    @pl.when(step == 0)
    def _():
        carry_ref[...] = jnp.zeros_like(carry_ref)

    lg = lg_ref[...]
    lane = lax.broadcasted_iota(jnp.int32, lg.shape, 1)
    lanef = lane.astype(F32)
    big = jnp.float32(1e9)
    rmax = lambda v: jnp.max(v, axis=-1, keepdims=True)
    rmin = lambda v: jnp.min(v, axis=-1, keepdims=True)
    rsum = lambda v: jnp.sum(v, axis=-1, keepdims=True)

    is_g = (lane >= N_EXPERTS) & (lane < N_EXPERTS + N_GROUPS)
    gl = jnp.where(is_g, lg, NEG_BIG)
    gmax = rmax(gl)
    gsum = rsum(jnp.where(is_g, jnp.exp(gl - gmax), 0.0))
    pg_star = 1.0 / gsum
    g_idx = rmin(jnp.where(is_g & (gl == gmax), lanef - N_EXPERTS, big))

    e_lo = g_idx * EXPERTS_PER_GROUP
    is_e = (lanef >= e_lo) & (lanef < e_lo + EXPERTS_PER_GROUP)
    el = jnp.where(is_e, lg, NEG_BIG)
    m1 = rmax(el)
    i1 = rmin(jnp.where(is_e & (el == m1), lanef, big))
    sel1 = lanef == i1
    el2 = jnp.where(sel1, NEG_BIG, el)
    m2 = rmax(el2)
    i2 = rmin(jnp.where(is_e & (el2 == m2) & jnp.logical_not(sel1), lanef, big))
    sel2 = lanef == i2
    z = rsum(jnp.where(is_e, jnp.exp(el - m1), 0.0))
    p1 = 1.0 / z
    p2 = jnp.exp(m2 - m1) / z
    gate1 = pg_star * p1 / (p1 + p2)
    gate2 = pg_star * p2 / (p1 + p2)

    onehot = jnp.where(sel1 | sel2, 1.0, 0.0)
    r = lax.broadcasted_iota(jnp.int32, (tm, tm), 0)
    c = lax.broadcasted_iota(jnp.int32, (tm, tm), 1)
    tri = jnp.where(c < r, 1.0, 0.0).astype(BF16)
    before = _dot(tri, onehot.astype(BF16)) + carry_ref[...]
    rank1 = rsum(jnp.where(sel1, before, 0.0))
    rank2 = rsum(jnp.where(sel2, before, 0.0))
    carry_ref[...] = carry_ref[...] + jnp.sum(onehot, axis=0, keepdims=True)
    cnt_ref[...] = carry_ref[...]

    info = jnp.zeros_like(lg)
    for k, val in enumerate((i1, i2, rank1, rank2, gate1, gate2)):
        info = jnp.where(lane == k, val, info)
    info_ref[...] = info
    infot_ref[...] = info.T[0:8, :]


def _route(logits, tm):
    n = logits.shape[0]
    kernel = functools.partial(_route_kernel, tm=tm)
    return pl.pallas_call(
        kernel,
        grid=(n // tm,),
        in_specs=[pl.BlockSpec((tm, LANES), lambda i: (i, 0))],
        out_specs=(pl.BlockSpec((tm, LANES), lambda i: (i, 0)),
                   pl.BlockSpec((8, tm), lambda i: (0, i)),
                   pl.BlockSpec((1, LANES), lambda i: (0, 0))),
        out_shape=(jax.ShapeDtypeStruct((n, LANES), F32), jax.ShapeDtypeStruct((8, n), F32),
                   jax.ShapeDtypeStruct((1, LANES), F32)),
        scratch_shapes=[pltpu.VMEM((1, LANES), F32)],
        compiler_params=_params(("arbitrary",)),
        name="route",
    )(logits)


DISPATCH_CHUNK = 128


def _dispatch_kernel(dest_ref, zflag_ref, xa_ref, xb_ref, xs_hbm, zero_ref, sem, zsem):
    n_all = dest_ref.shape[0] // 2

    zero_ref[...] = jnp.zeros_like(zero_ref)

    def zero_copy(blk):
        return pltpu.make_async_copy(zero_ref, xs_hbm.at[pl.ds(blk * MOE_BLOCK, MOE_BLOCK)], zsem)

    def start_zero(blk, carry):
        @pl.when(zflag_ref[blk] != 0)
        def _():
            zero_copy(blk).start()
        return carry

    def wait_zero(blk, carry):
        @pl.when(zflag_ref[blk] != 0)
        def _():
            zero_copy(blk).wait()
        return carry

    lax.fori_loop(0, zflag_ref.shape[0], start_zero, 0)
    lax.fori_loop(0, zflag_ref.shape[0], wait_zero, 0)

    def scatter(src_ref, t_off):
        def row_copy(t, k):
            return pltpu.make_async_copy(src_ref.at[pl.ds(t, 1)],
                                         xs_hbm.at[pl.ds(dest_ref[k * n_all + t_off + t], 1)], sem)

        def issue(c):
            def body(r, carry):
                row_copy(c * DISPATCH_CHUNK + r, 0).start(priority=0)
                row_copy(c * DISPATCH_CHUNK + r, 1).start(priority=1)
                return carry
            lax.fori_loop(0, DISPATCH_CHUNK, body, 0, unroll=8)

        def drain(c):
            def body(r, carry):
                row_copy(c * DISPATCH_CHUNK + r, 0).wait()
                row_copy(c * DISPATCH_CHUNK + r, 1).wait()
                return carry
            lax.fori_loop(0, DISPATCH_CHUNK, body, 0, unroll=8)

        n_chunks = src_ref.shape[0] // DISPATCH_CHUNK

        def chunk(c, carry):
            issue(c)

            @pl.when(c > 0)
            def _():
                drain(c - 1)
            return carry

        lax.fori_loop(0, n_chunks, chunk, 0)
        drain(n_chunks - 1)

    scatter(xa_ref, 0)
    scatter(xb_ref, xa_ref.shape[0])


def _dispatch(dest_flat, zero_flag, xa, xb):
    w = xa.shape[1]
    n_slots = zero_flag.shape[0] * MOE_BLOCK
    vmem = pl.BlockSpec(memory_space=pltpu.VMEM)
    grid_spec = pltpu.PrefetchScalarGridSpec(
        num_scalar_prefetch=2,
        grid=(1,),
        in_specs=[vmem, vmem],
        out_specs=pl.BlockSpec(memory_space=pl.ANY),
        scratch_shapes=[pltpu.VMEM((MOE_BLOCK, w), xa.dtype), pltpu.SemaphoreType.DMA(()),
                        pltpu.SemaphoreType.DMA(())],
    )
    return pl.pallas_call(
        _dispatch_kernel,
        grid_spec=grid_spec,
        out_shape=jax.ShapeDtypeStruct((n_slots, w), xa.dtype),
        compiler_params=pltpu.CompilerParams(dimension_semantics=("arbitrary",),
                                             vmem_limit_bytes=VMEM_LIMIT, has_side_effects=True),
        name="dispatch",
    )(dest_flat, zero_flag, xa, xb)


CAST_CHUNK_ELEMS = 64 * 1024
WEIGHT_DMA_SPLIT = 4
WEIGHT_DMA_PRIORITY = 1


def _expert_kernel(be_ref, nu_ref, nx_ref, xs_ref, w1_hbm, w3_hbm, w2_hbm, y_ref,
                   w1f_ref, w3f_ref, w2f_ref, w1b_ref, w3b_ref, w2b_ref, sem):
    b = pl.program_id(0)
    active = b < nu_ref[0]
    new_expert = jnp.logical_or(b == 0, be_ref[b] != be_ref[jnp.maximum(b - 1, 0)])

    def fetch(e):
        copies = []
        for k, (src, dst) in enumerate(((w1_hbm, w1f_ref), (w3_hbm, w3f_ref), (w2_hbm, w2f_ref))):
            slab = dst.shape[0] // WEIGHT_DMA_SPLIT
            for c in range(WEIGHT_DMA_SPLIT):
                rows = pl.ds(c * slab, slab)
                copies.append(pltpu.make_async_copy(src.at[e, rows], dst.at[rows], sem.at[k]))
        return copies

    @pl.when(b == 0)
    def _():
        for copy in fetch(be_ref[0]):
            copy.start(priority=WEIGHT_DMA_PRIORITY)

    @pl.when(jnp.logical_and(active, new_expert))
    def _():
        for copy in fetch(be_ref[b]):
            copy.wait()
        for src, dst in ((w1f_ref, w1b_ref), (w3f_ref, w3b_ref), (w2f_ref, w2b_ref)):
            rows, cols = src.shape
            chunk = CAST_CHUNK_ELEMS // cols

            def cast_rows(c, carry, src=src, dst=dst, chunk=chunk):
                r0 = pl.multiple_of(c * chunk, chunk)
                dst[pl.ds(r0, chunk), :] = src[pl.ds(r0, chunk), :].astype(BF16)
                return carry

            lax.fori_loop(0, rows // chunk, cast_rows, 0)

        @pl.when(nx_ref[b] >= 0)
        def _():
            for copy in fetch(nx_ref[b]):
                copy.start(priority=WEIGHT_DMA_PRIORITY)

    @pl.when(active)
    def _():
        lo, hi = _unpack_bf16_pair(xs_ref[...])
        half = D_MODEL // 2
        h1 = _dot(lo, w1b_ref[:half, :]) + _dot(hi, w1b_ref[half:, :])
        h3 = _dot(lo, w3b_ref[:half, :]) + _dot(hi, w3b_ref[half:, :])
        h = (h1 * jax.nn.sigmoid(h1) * h3).astype(BF16)
        y_ref[...] = _dot(h, w2b_ref[...])

    @pl.when(jnp.logical_not(active))
    def _():
        y_ref[...] = jnp.zeros_like(y_ref)


def _experts(block_e, n_used, next_e, xs, w1, w3, w2):
    n_slots = xs.shape[0]
    nb = n_slots // MOE_BLOCK
    blk = lambda b, be, nu, nx: (jnp.minimum(b, nu[0] - 1), 0)
    hbm = pl.BlockSpec(memory_space=pl.ANY)
    grid_spec = pltpu.PrefetchScalarGridSpec(
        num_scalar_prefetch=3,
        grid=(nb,),
        in_specs=[pl.BlockSpec((MOE_BLOCK, D_MODEL // 2), blk), hbm, hbm, hbm],
        out_specs=pl.BlockSpec((MOE_BLOCK, D_MODEL), lambda b, be, nu, nx: (b, 0)),
        scratch_shapes=[pltpu.VMEM((D_MODEL, D_EXPERT), F32), pltpu.VMEM((D_MODEL, D_EXPERT), F32),
                        pltpu.VMEM((D_EXPERT, D_MODEL), F32),
                        pltpu.VMEM((D_MODEL, D_EXPERT), BF16), pltpu.VMEM((D_MODEL, D_EXPERT), BF16),
                        pltpu.VMEM((D_EXPERT, D_MODEL), BF16),
                        pltpu.SemaphoreType.DMA((3,))],
    )
    return pl.pallas_call(
        _expert_kernel,
        grid_spec=grid_spec,
        out_shape=jax.ShapeDtypeStruct((n_slots, D_MODEL), F32),
        compiler_params=_params(("arbitrary",)),
        name="experts",
    )(block_e, n_used, next_e, xs, w1, w3, w2)


PLE_SPLIT = 4
GATHER_ISSUE_SLABS = 2


def _combine_ple_kernel(dest_ref, h_ref, info_ref, p_ref, g_ref, *rest, tm, t_off):
    wpg_refs = rest[:PLE_SPLIT]
    wpp_ref, yb_hbm, o_ref, buf_ref, sem = rest[PLE_SPLIT:]
    i = pl.program_id(0)
    last = pl.num_programs(0) - 1
    n_all = dest_ref.shape[0] // 2
    slot = i % 2

    def row_copy(step, into, r, k):
        t = t_off + step * tm + r
        return pltpu.make_async_copy(yb_hbm.at[pl.ds(dest_ref[k * n_all + t], 1)],
                                     buf_ref.at[into, k, pl.ds(r, 1)], sem.at[into])

    def drain(into):
        def body(r, carry):
            for k in range(2):
                pltpu.make_async_copy(yb_hbm.at[pl.ds(0, 1)], buf_ref.at[into, k, pl.ds(r, 1)],
                                      sem.at[into]).wait()
            return carry
        lax.fori_loop(0, tm, body, 0, unroll=8)

    @pl.when(i == 0)
    def _():
        def body(r, carry):
            row_copy(0, 0, r, 0).start()
            row_copy(0, 0, r, 1).start()
            return carry
        lax.fori_loop(0, tm, body, 0, unroll=8)

    drain(slot)
    h2 = h_ref[...] + (info_ref[:, 4:5] * buf_ref[slot, 0] + info_ref[:, 5:6] * buf_ref[slot, 1])
    hn = (h2 * lax.rsqrt(jnp.mean(h2 * h2, axis=-1, keepdims=True) + EPS) * g_ref[...]).astype(BF16)
    pb = p_ref[...].astype(BF16)

    nxt = jnp.minimum(i + 1, last)
    ch = D_MODEL // PLE_SPLIT
    per = tm // GATHER_ISSUE_SLABS
    for k in range(PLE_SPLIT):
        for r in range(k * per, (k + 1) * per if k < GATHER_ISSUE_SLABS else k * per):
            row_copy(nxt, 1 - slot, r, 0).start(priority=0)
            row_copy(nxt, 1 - slot, r, 1).start(priority=1)
        cs = slice(k * ch, (k + 1) * ch)
        gate = jax.nn.sigmoid(_dot(hn, wpg_refs[k][...]))
        o_ref[:, cs] = h2[:, cs] + gate * _dot(pb, wpp_ref[:, cs])

    @pl.when(i == last)
    def _():
        drain(1 - slot)


def _combine_ple(dest_flat, h, info, yb, p, g_ple, w_pg, w_pp, t_off, tm):
    n = h.shape[0]
    kernel = functools.partial(_combine_ple_kernel, tm=tm, t_off=t_off)
    ob = t_off // tm
    row = lambda w: pl.BlockSpec((tm, w), lambda i, d: (i, 0))
    grid_spec = pltpu.PrefetchScalarGridSpec(
        num_scalar_prefetch=1,
        grid=(n // tm,),
        in_specs=[row(D_MODEL), pl.BlockSpec((tm, LANES), lambda i, d: (i + ob, 0)), row(PLE_DIM),
                  _resident((1, D_MODEL))]
        + [_resident((D_MODEL, D_MODEL // PLE_SPLIT), (0, k)) for k in range(PLE_SPLIT)]
        + [_resident((PLE_DIM, D_MODEL)), pl.BlockSpec(memory_space=pl.ANY)],
        out_specs=row(D_MODEL),
        scratch_shapes=[pltpu.VMEM((2, 2, tm, D_MODEL), F32), pltpu.SemaphoreType.DMA((2,))],
    )
    return pl.pallas_call(
        kernel,
        grid_spec=grid_spec,
        out_shape=jax.ShapeDtypeStruct((n, D_MODEL), F32),
        compiler_params=_params(("arbitrary",)),
        name="combine_ple",
    )(dest_flat, h, info, p, g_ple, *([w_pg] * PLE_SPLIT), w_pp, yb)


def _mixer_tokens(x2d, wts, tm):
    return _in_proj(x2d, wts["g_mix"], wts["w_parts"], wts["w_f"], wts["b_f"],
                    wts["q_gain"], wts["k_gain"], tm)


def kernel(x_prompt, x_sample, cache_k, cache_v, cache_logf, cache_conv, p_prompt, p_sample,
           g_mix, w_in, b_f, q_gain, k_gain, w_dw, b_dw, ln_g, ln_b, gc, ga, w_out,
           g_ffn, w_router_g, b_router_g, w_router_e, b_router_e, w1, w3, w2,
           g_ple, w_pg, w_pp):
    batch, seq, _ = x_prompt.shape
    dec_batch, t_new, _ = x_sample.shape
    past = cache_k.shape[2]
    n_p = batch * seq
    n_s = dec_batch * t_new
    n_all = n_p + n_s
    tm = 256
    li = 0

    w_in_l = w_in[li]
    pad_lanes = lambda a: jnp.pad(a, ((0, 0), (0, LANES - a.shape[1])))
    row2d = lambda a: a.reshape(1, -1)
    wts = {
        "g_mix": row2d(g_mix[li]),
        "w_parts": [w_in_l[:, c:c + CONV_WIDTH].astype(BF16) for c in range(0, MAIN_COLS, CONV_WIDTH)],
        "w_f": pad_lanes(w_in_l[:, MAIN_COLS:]).astype(BF16),
        "b_f": pad_lanes(row2d(b_f[li])),
        "q_gain": row2d(q_gain[li]),
        "k_gain": row2d(k_gain[li]),
    }
    w_dw_p = jnp.pad(w_dw[li], ((0, CONV_HALO - CONV_KERNEL), (0, 0)))
    w_out_b = w_out[li].astype(BF16)
    w_r = pad_lanes(jnp.concatenate([w_router_e[li], w_router_g[li]], axis=1))
    wr_hi = w_r.astype(BF16)
    wr_lo = (w_r - wr_hi.astype(F32)).astype(BF16)
    b_r = pad_lanes(row2d(jnp.concatenate([b_router_e[li], b_router_g[li]])))
    w_pg_b, w_pp_b = w_pg[li].astype(BF16), w_pp[li].astype(BF16)

    xp = x_prompt.reshape(n_p, D_MODEL)
    u_p, q_p, k_p, kb_p, v_p, _, vbt_p, lf_p = _mixer_tokens(xp, wts, tm)
    lf_p_row = lf_p.reshape(batch, seq, N_HEADS).transpose(0, 2, 1)
    f_p_row = _cumsum_lanes(lf_p_row.reshape(batch * N_HEADS, seq))
    ctx_p = jnp.zeros((batch, CONV_HALO, CONV_WIDTH), F32)
    conv_args = (w_dw_p, row2d(b_dw[li]), row2d(ln_g[li]), row2d(ln_b[li]), row2d(gc[li]))
    yc_p = _conv_module(u_p, ctx_p, *conv_args, batch, seq, tm)
    qa_p, ka_p = _attn_aug(f_p_row, batch, seq, tm)
    ya_p = _attn_prompt(q_p, qa_p, kb_p, ka_p, vbt_p, batch, seq, 256, 256)
    merge_args = (row2d(ga[li]), w_out_b, row2d(g_ffn[li]), jnp.concatenate([wr_hi, wr_lo], axis=1), b_r)
    h_p, xpk_p, lg_p = _merge(yc_p, ya_p, xp, *merge_args, tm)

    xs_ = x_sample.reshape(n_s, D_MODEL)
    u_s, q_s, k_s, kb_s, v_s, vb_s, _, lf_s = _mixer_tokens(xs_, wts, tm)
    clf_row = cache_logf[li].transpose(0, 2, 1).reshape(dec_batch * N_HEADS, past)
    fc_row = _cumsum_lanes(clf_row).reshape(dec_batch, N_HEADS, past)
    lf_s_row = lf_s.reshape(dec_batch, t_new, N_HEADS).transpose(0, 2, 1).reshape(dec_batch * N_HEADS, t_new)
    fn_row = _cumsum_lanes(jnp.pad(lf_s_row, ((0, 0), (0, LANES - t_new)))).reshape(dec_batch, N_HEADS, LANES)
    fn_col = fn_row[:, :, :t_new].transpose(0, 2, 1).reshape(n_s, N_HEADS)
    ctx_s = jnp.pad(cache_conv[li], ((0, 0), (CONV_HALO - CONV_STATE, 0), (0, 0)))
    yc_s = _conv_module(u_s, ctx_s, *conv_args, dec_batch, t_new, t_new)
    ya_s = _attn_sample(q_s, kb_s, vb_s,
                        cache_k[li], cache_v[li],
                        fc_row, fn_row, fn_col, dec_batch, t_new, past)
    h_s, xpk_s, lg_s = _merge(yc_s, ya_s, xs_, *merge_args, tm)

    info, info_t, counts = _route(jnp.concatenate([lg_p, lg_s], axis=0), tm)
    counts = counts[0, :N_EXPERTS].astype(jnp.int32)
    bcounts = (counts + MOE_BLOCK - 1) // MOE_BLOCK
    bends = jnp.cumsum(bcounts)
    pstarts = (bends - bcounts) * MOE_BLOCK
    n_rows = n_all * 2
    nb = -(-(n_rows + N_EXPERTS * (MOE_BLOCK - 1)) // MOE_BLOCK)
    e_idx = info_t[0:2].astype(jnp.int32)
    expert_ids = jnp.arange(N_EXPERTS, dtype=jnp.int32)[:, None, None]
    seg_start = jnp.sum(jnp.where(e_idx[None] == expert_ids, pstarts[:, None, None], 0), axis=0)
    dest = (seg_start + info_t[2:4].astype(jnp.int32)).reshape(n_rows)
    block_e = jnp.minimum(jnp.sum(bends[None, :] <= jnp.arange(nb, dtype=jnp.int32)[:, None], axis=1),
                          N_EXPERTS - 1).astype(jnp.int32)
    n_used = bends[N_EXPERTS - 1:].astype(jnp.int32)
    seg_end = jnp.sum(jnp.where(block_e[None, :] == expert_ids[:, :, 0], bends[:, None], 0), axis=0)
    block_ids = jnp.arange(nb, dtype=jnp.int32)
    zero_flag = ((block_ids + 1 == seg_end) | (block_ids >= n_used[0])).astype(jnp.int32)
    xs_sorted = _dispatch(dest, zero_flag, xpk_p, xpk_s)
    next_e = jnp.where(seg_end < n_used[0], block_e[jnp.minimum(seg_end, nb - 1)], -1).astype(jnp.int32)
    yb = _experts(block_e, n_used, next_e, xs_sorted, w1[li], w3[li], w2[li])

    ple_args = (row2d(g_ple[li]), w_pg_b, w_pp_b)
    y_p = _combine_ple(dest, h_p, info, yb, p_prompt[li].reshape(n_p, PLE_DIM), *ple_args, 0, tm)
    y_s = _combine_ple(dest, h_s, info, yb, p_sample[li].reshape(n_s, PLE_DIM), *ple_args, n_p, tm)

    heads = lambda a, b, t: a.reshape(1, b, t, N_HEADS, HEAD_DIM)
    return (
        y_p.reshape(batch, seq, D_MODEL),
        y_s.reshape(dec_batch, t_new, D_MODEL),
        heads(k_p, batch, seq), heads(v_p, batch, seq),
        lf_p.reshape(1, batch, seq, N_HEADS),
        u_p.reshape(batch, seq, CONV_WIDTH)[None, :, seq - CONV_STATE:, :],
        heads(k_s, dec_batch, t_new), heads(v_s, dec_batch, t_new),
        lf_s.reshape(1, dec_batch, t_new, N_HEADS),
        u_s.reshape(dec_batch, t_new, CONV_WIDTH)[None, :, t_new - CONV_STATE:, :],
    )
```

```python
import functools

import jax
import jax.numpy as jnp
from jax import lax
from jax.experimental import pallas as pl
from jax.experimental.pallas import tpu as pltpu

D_MODEL = 2048
CONV_WIDTH = 1024
ATTN_WIDTH = 1024
HEAD_DIM = 128
N_HEADS = 8
CONV_KERNEL = 31
CONV_STATE = CONV_KERNEL - 1
N_GROUPS = 4
EXPERTS_PER_GROUP = 8
N_EXPERTS = 32
D_EXPERT = 512
PLE_DIM = 256
MOE_BLOCK = 256
EPS = 1e-6
MAIN_COLS = 2 * CONV_WIDTH + 3 * ATTN_WIDTH

LANES = 128
CONV_HALO = 32
VMEM_LIMIT = 56 * 1024 * 1024

F32 = jnp.float32
BF16 = jnp.bfloat16
NEG_BIG = -1e30
LOG2E = 1.4426950408889634


def _dot(a, b):
    return jnp.dot(a, b, preferred_element_type=F32)


def _params(sem):
    return pltpu.CompilerParams(dimension_semantics=sem, vmem_limit_bytes=VMEM_LIMIT)


def _resident(shape, index=None):
    index = (0,) * len(shape) if index is None else index
    return pl.BlockSpec(shape, lambda *_: index, pipeline_mode=pl.Buffered(1))


def _inproj_kernel(x_ref, g_ref, wval_ref, wgate_ref, wq_ref, wk_ref, wv_ref, wf_ref, bf_ref, qg_ref, kg_ref,
                   u_ref, q_ref, k_ref, kb_ref, v_ref, vb_ref, vbt_ref, lf_ref):
    x = x_ref[...]
    ms = jnp.mean(x * x, axis=-1, keepdims=True)
    a = (x * lax.rsqrt(ms + EPS) * g_ref[...]).astype(BF16)

    ch = 256
    for c in range(0, CONV_WIDTH, ch):
        val = _dot(a, wval_ref[:, c:c + ch])
        gate = _dot(a, wgate_ref[:, c:c + ch])
        u_ref[:, c:c + ch] = val * jax.nn.sigmoid(gate)

    def head_norm(z, gain):
        return z * lax.rsqrt(jnp.mean(z * z, axis=-1, keepdims=True) + EPS) * gain

    scale = LOG2E * HEAD_DIM ** -0.5
    for c in range(0, ATTN_WIDTH, ch):
        zq = _dot(a, wq_ref[:, c:c + ch])
        zk = _dot(a, wk_ref[:, c:c + ch])
        zv = _dot(a, wv_ref[:, c:c + ch])
        for s in range(0, ch, HEAD_DIM):
            qn = head_norm(zq[:, s:s + HEAD_DIM], qg_ref[...])
            kn = head_norm(zk[:, s:s + HEAD_DIM], kg_ref[...])
            q_ref[:, c + s:c + s + HEAD_DIM] = (qn * scale).astype(BF16)
            k_ref[:, c + s:c + s + HEAD_DIM] = kn
            kb_ref[:, c + s:c + s + HEAD_DIM] = kn.astype(BF16)
        v_ref[:, c:c + ch] = zv
        vb_ref[:, c:c + ch] = zv.astype(BF16)
        vbt_ref[c:c + ch, :] = zv.T.astype(BF16)

    f = _dot(a, wf_ref[...]) + bf_ref[...]
    lf = jnp.minimum(f, 0.0) - jnp.log1p(jnp.exp(-jnp.abs(f)))
    lf_ref[...] = lf[:, :N_HEADS]


def _in_proj(x, g_mix, w_parts, w_f, b_f, q_gain, k_gain, tm):
    n = x.shape[0]
    row = lambda w: pl.BlockSpec((tm, w), lambda i: (i, 0))
    out_shape = (
        jax.ShapeDtypeStruct((n, CONV_WIDTH), F32),
        jax.ShapeDtypeStruct((n, ATTN_WIDTH), BF16),
        jax.ShapeDtypeStruct((n, ATTN_WIDTH), F32),
        jax.ShapeDtypeStruct((n, ATTN_WIDTH), BF16),
        jax.ShapeDtypeStruct((n, ATTN_WIDTH), F32),
        jax.ShapeDtypeStruct((n, ATTN_WIDTH), BF16),
        jax.ShapeDtypeStruct((ATTN_WIDTH, n), BF16),
        jax.ShapeDtypeStruct((n, N_HEADS), F32),
    )
    return pl.pallas_call(
        _inproj_kernel,
        grid=(n // tm,),
        in_specs=[row(D_MODEL), _resident((1, D_MODEL))] + [_resident((D_MODEL, CONV_WIDTH))] * 5 + [
                  _resident((D_MODEL, LANES)), _resident((1, LANES)),
                  _resident((1, HEAD_DIM)), _resident((1, HEAD_DIM))],
        out_specs=(row(CONV_WIDTH), row(ATTN_WIDTH), row(ATTN_WIDTH), row(ATTN_WIDTH),
                   row(ATTN_WIDTH), row(ATTN_WIDTH),
                   pl.BlockSpec((ATTN_WIDTH, tm), lambda i: (0, i)), row(N_HEADS)),
        out_shape=out_shape,
        compiler_params=_params(("parallel",)),
        name="in_proj",
    )(x, g_mix, *w_parts, w_f, b_f, q_gain, k_gain)


def _cumsum_kernel(x_ref, o_ref):
    x = x_ref[...]
    width = x.shape[1]
    lane = lax.broadcasted_iota(jnp.int32, x.shape, 1)
    s = 1
    while s < width:
        x = x + jnp.where(lane >= s, pltpu.roll(x, s, axis=1), 0.0)
        s *= 2
    o_ref[...] = x


def _cumsum_lanes(x):
    return pl.pallas_call(
        _cumsum_kernel,
        out_shape=jax.ShapeDtypeStruct(x.shape, F32),
        name="cumsum",
    )(x)


def _conv_kernel(u_ref, halo_ref, ctx_ref, w_ref, bdw_ref, lng_ref, lnb_ref, gc_ref,
                 o_ref, ext_ref, y_ref, *, tm):
    i = pl.program_id(1)

    @pl.when(i == 0)
    def _():
        ext_ref[0:CONV_HALO, :] = ctx_ref[0]

    @pl.when(i > 0)
    def _():
        ext_ref[0:CONV_HALO, :] = halo_ref[...]

    ext_ref[CONV_HALO:CONV_HALO + tm, :] = u_ref[...]

    rows = min(64, tm)
    ch = LANES
    sub = 8
    wlen = rows + CONV_HALO
    first = CONV_HALO - CONV_STATE

    def conv_rows(r, carry):
        r0 = pl.multiple_of(r * rows, rows)
        for c in range(0, CONV_WIDTH, ch):
            acc = jnp.zeros((rows, ch), F32)
            win = ext_ref[pl.ds(r0, wlen), c:c + ch]
            for rho in range(sub):
                sh = win if rho == 0 else pltpu.roll(win, wlen - rho, axis=0)
                for a in range(wlen // sub):
                    j = sub * a + rho - first
                    if 0 <= j < CONV_KERNEL:
                        acc = acc + sh[sub * a:sub * a + rows] * w_ref[j:j + 1, c:c + ch]
            y_ref[pl.ds(r0, rows), c:c + ch] = acc
        return carry

    lax.fori_loop(0, tm // rows, conv_rows, 0)

    nrows = min(128, tm)

    def norm_rows(r, carry):
        r0 = pl.multiple_of(r * nrows, nrows)
        y = y_ref[pl.ds(r0, nrows), :] + bdw_ref[...]
        mu = jnp.mean(y, axis=-1, keepdims=True)
        yc = y - mu
        var = jnp.mean(yc * yc, axis=-1, keepdims=True)
        z = yc * lax.rsqrt(var + EPS) * lng_ref[...] + lnb_ref[...]
        s = z * jax.nn.sigmoid(z)
        ms = jnp.mean(s * s, axis=-1, keepdims=True)
        o_ref[pl.ds(r0, nrows), :] = (s * lax.rsqrt(ms + EPS) * gc_ref[...]).astype(BF16)
        return carry

    lax.fori_loop(0, tm // nrows, norm_rows, 0)


def _conv_module(u, ctx, w_dw, b_dw, ln_g, ln_b, gc, batch, seq, tm):
    nt = seq // tm
    hb = tm // CONV_HALO
    kernel = functools.partial(_conv_kernel, tm=tm)
    return pl.pallas_call(
        kernel,
        grid=(batch, nt),
        in_specs=[
            pl.BlockSpec((tm, CONV_WIDTH), lambda b, i: (b * nt + i, 0)),
            pl.BlockSpec((CONV_HALO, CONV_WIDTH),
                         lambda b, i: (jnp.maximum((b * nt + i) * hb - 1, 0), 0)),
            pl.BlockSpec((1, CONV_HALO, CONV_WIDTH), lambda b, i: (b, 0, 0)),
            _resident((CONV_HALO, CONV_WIDTH)),
            _resident((1, CONV_WIDTH)), _resident((1, CONV_WIDTH)),
            _resident((1, CONV_WIDTH)), _resident((1, CONV_WIDTH)),
        ],
        out_specs=pl.BlockSpec((tm, CONV_WIDTH), lambda b, i: (b * nt + i, 0)),
        out_shape=jax.ShapeDtypeStruct((batch * seq, CONV_WIDTH), BF16),
        scratch_shapes=[pltpu.VMEM((CONV_HALO + tm, CONV_WIDTH), F32),
                        pltpu.VMEM((tm, CONV_WIDTH), F32)],
        compiler_params=_params(("parallel", "arbitrary")),
        name="conv_module",
    )(u, u, ctx, w_dw, b_dw, ln_g, ln_b, gc)


def _qk(q, k):
    return lax.dot_general(q, k, (((1,), (1,)), ((), ())), preferred_element_type=F32)


AUG_TERMS = 3
AUG_STRIDE = LANES // N_HEADS
QK_AHEAD = 4


def _aug_kernel(f_ref, qa_ref, ka_ref):
    f = f_ref[...] * LOG2E
    tm = f.shape[1]
    row = lax.broadcasted_iota(jnp.int32, (LANES, tm), 0)
    sub = row % AUG_STRIDE
    qa = jnp.where((sub >= AUG_TERMS) & (sub < 2 * AUG_TERMS), 1.0, 0.0)
    ka = jnp.where(sub < AUG_TERMS, 1.0, 0.0)
    for h in range(N_HEADS):
        rest = f[h:h + 1, :]
        for t in range(AUG_TERMS):
            piece = rest.astype(BF16).astype(F32)
            rest = rest - piece
            qa = jnp.where(row == h * AUG_STRIDE + t, piece, qa)
            ka = jnp.where(row == h * AUG_STRIDE + AUG_TERMS + t, -piece, ka)
    qa_ref[...] = qa.T.astype(BF16)
    ka_ref[...] = ka.T.astype(BF16)


def _attn_aug(f_row, batch, seq, tm):
    nt = seq // tm
    out = pl.BlockSpec((tm, LANES), lambda b, i: (b * nt + i, 0))
    return pl.pallas_call(
        _aug_kernel,
        grid=(batch, nt),
        in_specs=[pl.BlockSpec((N_HEADS, tm), lambda b, i: (b, i))],
        out_specs=(out, out),
        out_shape=(jax.ShapeDtypeStruct((batch * seq, LANES), BF16),
                   jax.ShapeDtypeStruct((batch * seq, LANES), BF16)),
        compiler_params=_params(("parallel", "parallel")),
        name="attn_aug",
    )(f_row)


def _attn_prompt_kernel(q_ref, qa_ref, k_ref, ka_ref, vt_ref, o_ref, m_ref, l_ref, acc_ref, qt_ref,
                        sp_ref, *, tq, tk):
    i = pl.program_id(1)
    key = lax.broadcasted_iota(jnp.int32, (tk, tq), 0)
    qry = lax.broadcasted_iota(jnp.int32, (tk, tq), 1)

    m_ref[...] = jnp.full(m_ref.shape, NEG_BIG, F32)
    l_ref[...] = jnp.zeros(l_ref.shape, F32)
    acc_ref[...] = jnp.zeros(acc_ref.shape, F32)
    aug_lane = lax.broadcasted_iota(jnp.int32, (tq, LANES), 1)
    for h in range(N_HEADS):
        hs = slice(h * HEAD_DIM, (h + 1) * HEAD_DIM)
        own = (aug_lane >= h * AUG_STRIDE) & (aug_lane < (h + 1) * AUG_STRIDE)
        qa = jnp.where(own, qa_ref[...], jnp.zeros_like(qa_ref))
        qt_ref[h] = jnp.concatenate([q_ref[:, hs], qa], axis=1).T

    def scores(ks, h):
        hs = slice(h * HEAD_DIM, (h + 1) * HEAD_DIM)
        kf = jnp.concatenate([k_ref[pl.ds(ks, tk), hs], ka_ref[pl.ds(ks, tk), :]], axis=1)
        return _dot(kf, qt_ref[h])

    def tile_step(ks, ks_next, masked):
        pending = [sp_ref[a] for a in range(QK_AHEAD)]
        for h in range(N_HEADS):
            hs = slice(h * HEAD_DIM, (h + 1) * HEAD_DIM)
            s = pending.pop(0)
            if h + QK_AHEAD < N_HEADS:
                pending.append(scores(ks, h + QK_AHEAD))
            elif ks_next is not None:
                pending.append(scores(ks_next, h + QK_AHEAD - N_HEADS))
            if masked:
                s = jnp.where(key <= qry, s, -jnp.inf)
            m = m_ref[h]
            m_new = jnp.maximum(m, jnp.max(s, axis=0, keepdims=True))
            alpha = jnp.exp2(m - m_new)
            p = jnp.exp2(s - m_new)
            m_ref[h] = m_new
            l_ref[h] = alpha * l_ref[h] + jnp.sum(p, axis=0, keepdims=True)
            pv = _dot(vt_ref[hs, pl.ds(ks, tk)], p.astype(BF16))
            acc_ref[h] = alpha * acc_ref[h] + pv
        for a, s in enumerate(pending):
            sp_ref[a] = s

    def body(j, carry):
        tile_step(pl.multiple_of(j * tk, tk), pl.multiple_of((j + 1) * tk, tk), False)
        return carry

    for a in range(QK_AHEAD):
        sp_ref[a] = scores(0, a)
    lax.fori_loop(0, i, body, 0)
    tile_step(pl.multiple_of(i * tk, tk), None, True)
    for h in range(N_HEADS):
        o_ref[:, h * HEAD_DIM:(h + 1) * HEAD_DIM] = (acc_ref[h] / l_ref[h]).T


def _attn_prompt(q, qa, kb, ka, vbt, batch, seq, tq, tk):
    nq = seq // tq
    kernel = functools.partial(_attn_prompt_kernel, tq=tq, tk=tk)
    qblk = pl.BlockSpec((tq, ATTN_WIDTH), lambda b, i: (b * nq + i, 0))
    kblk = pl.BlockSpec((seq, ATTN_WIDTH), lambda b, i: (b, 0), pipeline_mode=pl.Buffered(1))
    vblk = pl.BlockSpec((ATTN_WIDTH, seq), lambda b, i: (0, b), pipeline_mode=pl.Buffered(1))
    return pl.pallas_call(
        kernel,
        grid=(batch, nq),
        in_specs=[qblk, pl.BlockSpec((tq, LANES), lambda b, i: (b * nq + i, 0)), kblk,
                  pl.BlockSpec((seq, LANES), lambda b, i: (b, 0), pipeline_mode=pl.Buffered(1)), vblk],
        out_specs=pl.BlockSpec((tq, ATTN_WIDTH), lambda b, i: (b * nq + i, 0)),
        out_shape=jax.ShapeDtypeStruct((batch * seq, ATTN_WIDTH), F32),
        scratch_shapes=[pltpu.VMEM((N_HEADS, 1, tq), F32), pltpu.VMEM((N_HEADS, 1, tq), F32),
                        pltpu.VMEM((N_HEADS, HEAD_DIM, tq), F32),
                        pltpu.VMEM((N_HEADS, 2 * HEAD_DIM, tq), BF16),
                        pltpu.VMEM((QK_AHEAD, tk, tq), F32)],
        compiler_params=_params(("parallel", "arbitrary")),
        name="attn_prompt",
    )(q, qa, kb, ka, vbt)


def _attn_sample_kernel(q_ref, kn_ref, vn_ref, ck_hbm, cv_hbm, fc_ref, fnrow_ref, fncol_ref,
                        o_ref, kbuf_ref, vbuf_ref, sem, *, t_new, past):
    b = pl.program_id(0)
    slot = b % 2

    def fetch(batch_idx, into):
        copies = []
        for h in range(N_HEADS):
            copies.append(pltpu.make_async_copy(ck_hbm.at[batch_idx, :, h, :], kbuf_ref.at[into, h],
                                                sem.at[into, 0]))
            copies.append(pltpu.make_async_copy(cv_hbm.at[batch_idx, :, h, :], vbuf_ref.at[into, h],
                                                sem.at[into, 1]))
        return copies

    @pl.when(b == 0)
    def _():
        for copy in fetch(0, 0):
            copy.start()

    @pl.when(b + 1 < pl.num_programs(0))
    def _():
        for copy in fetch(b + 1, 1 - slot):
            copy.start()

    for copy in fetch(b, slot):
        copy.wait()

    row = lax.broadcasted_iota(jnp.int32, (t_new, t_new), 0)
    col = lax.broadcasted_iota(jnp.int32, (t_new, t_new), 1)
    causal = col <= row
    for h in range(N_HEADS):
        hs = slice(h * HEAD_DIM, (h + 1) * HEAD_DIM)
        q = q_ref[:, hs]
        fn_q = fncol_ref[:, h:h + 1]
        fc = fc_ref[0, h:h + 1, :]
        fc_last = fc[:, past - 1:past]
        s_c = _qk(q, kbuf_ref[slot, h].astype(BF16)) + ((fc_last + fn_q) - fc) * LOG2E
        fn_k = fnrow_ref[0, h:h + 1, 0:t_new]
        s_n = _qk(q, kn_ref[:, hs]) + (fn_q - fn_k) * LOG2E
        s_n = jnp.where(causal, s_n, -jnp.inf)
        m = jnp.maximum(jnp.max(s_c, axis=-1, keepdims=True), jnp.max(s_n, axis=-1, keepdims=True))
        p_c = jnp.exp2(s_c - m)
        p_n = jnp.exp2(s_n - m)
        l = jnp.sum(p_c, axis=-1, keepdims=True) + jnp.sum(p_n, axis=-1, keepdims=True)
        acc = _dot(p_c.astype(BF16), vbuf_ref[slot, h].astype(BF16)) + _dot(p_n.astype(BF16), vn_ref[:, hs])
        o_ref[:, hs] = acc / l


def _attn_sample(q, kb, vb, cache_k, cache_v, fc_row, fn_row, fn_col, batch, t_new, past):
    kernel = functools.partial(_attn_sample_kernel, t_new=t_new, past=past)
    tok = lambda w: pl.BlockSpec((t_new, w), lambda b: (b, 0))
    return pl.pallas_call(
        kernel,
        grid=(batch,),
        in_specs=[
            tok(ATTN_WIDTH), tok(ATTN_WIDTH), tok(ATTN_WIDTH),
            pl.BlockSpec(memory_space=pl.ANY),
            pl.BlockSpec(memory_space=pl.ANY),
            pl.BlockSpec((1, N_HEADS, past), lambda b: (b, 0, 0)),
            pl.BlockSpec((1, N_HEADS, LANES), lambda b: (b, 0, 0)),
            tok(N_HEADS),
        ],
        out_specs=tok(ATTN_WIDTH),
        out_shape=jax.ShapeDtypeStruct((batch * t_new, ATTN_WIDTH), F32),
        scratch_shapes=[pltpu.VMEM((2, N_HEADS, past, HEAD_DIM), F32),
                        pltpu.VMEM((2, N_HEADS, past, HEAD_DIM), F32),
                        pltpu.SemaphoreType.DMA((2, 2))],
        compiler_params=_params(("arbitrary",)),
        name="attn_sample",
    )(q, kb, vb, cache_k, cache_v, fc_row, fn_row, fn_col)


def _pack_bf16_pair(lo, hi):
    lo_bits = lax.bitcast_convert_type(lo, jnp.uint32) >> 16
    hi_bits = lax.bitcast_convert_type(hi, jnp.uint32) & jnp.uint32(0xFFFF0000)
    return lo_bits | hi_bits


def _unpack_f32_pair(w):
    lo = lax.bitcast_convert_type(w << 16, F32)
    hi = lax.bitcast_convert_type(w & jnp.uint32(0xFFFF0000), F32)
    return lo, hi


def _unpack_bf16_pair(w):
    lo, hi = _unpack_f32_pair(w)
    return lo.astype(BF16), hi.astype(BF16)


def _merge_kernel(yc_ref, ya_ref, x_ref, ga_ref, woc_ref, woa_ref, gf_ref, wr_ref, br_ref,
                  h_ref, xp_ref, lg_ref):
    ya = ya_ref[...]
    ya_n = (ya * lax.rsqrt(jnp.mean(ya * ya, axis=-1, keepdims=True) + EPS) * ga_ref[...]).astype(BF16)
    y = _dot(yc_ref[...], woc_ref[...]) + _dot(ya_n, woa_ref[...])
    h = x_ref[...] + y
    h_ref[...] = h
    xn = h * lax.rsqrt(jnp.mean(h * h, axis=-1, keepdims=True) + EPS) * gf_ref[...]
    xn_hi = xn.astype(BF16)
    xn_hi32 = xn_hi.astype(F32)
    xn_lo = (xn - xn_hi32).astype(BF16)
    hh_hl = _dot(xn_hi, wr_ref[...])
    lg_ref[...] = hh_hl[:, :LANES] + hh_hl[:, LANES:] + _dot(xn_lo, wr_ref[:, :LANES]) + br_ref[...]
    half = D_MODEL // 2
    xp_ref[...] = _pack_bf16_pair(xn_hi32[:, :half], xn_hi32[:, half:])


def _merge(yc_n, ya, x, ga, w_out, g_ffn, wr_cat, b_r, tm):
    n = x.shape[0]
    row = lambda w: pl.BlockSpec((tm, w), lambda i: (i, 0))
    return pl.pallas_call(
        _merge_kernel,
        grid=(n // tm,),
        in_specs=[row(CONV_WIDTH), row(ATTN_WIDTH), row(D_MODEL), _resident((1, ATTN_WIDTH)),
                  _resident((CONV_WIDTH, D_MODEL), (0, 0)), _resident((ATTN_WIDTH, D_MODEL), (1, 0)),
                  _resident((1, D_MODEL)),
                  _resident((D_MODEL, 2 * LANES)), _resident((1, LANES))],
        out_specs=(row(D_MODEL), row(D_MODEL // 2), row(LANES)),
        out_shape=(jax.ShapeDtypeStruct((n, D_MODEL), F32),
                   jax.ShapeDtypeStruct((n, D_MODEL // 2), jnp.uint32),
                   jax.ShapeDtypeStruct((n, LANES), F32)),
        compiler_params=_params(("parallel",)),
        name="merge_out",
    )(yc_n, ya, x, ga, w_out, w_out, g_ffn, wr_cat, b_r)


def _route_kernel(lg_ref, info_ref, infot_ref, cnt_ref, carry_ref, *, tm):
    step = pl.program_id(0)

    @pl.when(step == 0)
    def _():
        carry_ref[...] = jnp.zeros_like(carry_ref)

    lg = lg_ref[...]
    lane = lax.broadcasted_iota(jnp.int32, lg.shape, 1)
    lanef = lane.astype(F32)
    big = jnp.float32(1e9)
    rmax = lambda v: jnp.max(v, axis=-1, keepdims=True)
    rmin = lambda v: jnp.min(v, axis=-1, keepdims=True)
    rsum = lambda v: jnp.sum(v, axis=-1, keepdims=True)

    is_g = (lane >= N_EXPERTS) & (lane < N_EXPERTS + N_GROUPS)
    gl = jnp.where(is_g, lg, NEG_BIG)
    gmax = rmax(gl)
    gsum = rsum(jnp.where(is_g, jnp.exp(gl - gmax), 0.0))
    pg_star = 1.0 / gsum
    g_idx = rmin(jnp.where(is_g & (gl == gmax), lanef - N_EXPERTS, big))

    e_lo = g_idx * EXPERTS_PER_GROUP
    is_e = (lanef >= e_lo) & (lanef < e_lo + EXPERTS_PER_GROUP)
    el = jnp.where(is_e, lg, NEG_BIG)
    m1 = rmax(el)
    i1 = rmin(jnp.where(is_e & (el == m1), lanef, big))
    sel1 = lanef == i1
    el2 = jnp.where(sel1, NEG_BIG, el)
    m2 = rmax(el2)
    i2 = rmin(jnp.where(is_e & (el2 == m2) & jnp.logical_not(sel1), lanef, big))
    sel2 = lanef == i2
    z = rsum(jnp.where(is_e, jnp.exp(el - m1), 0.0))
    p1 = 1.0 / z
    p2 = jnp.exp(m2 - m1) / z
    gate1 = pg_star * p1 / (p1 + p2)
    gate2 = pg_star * p2 / (p1 + p2)

    onehot = jnp.where(sel1 | sel2, 1.0, 0.0)
    r = lax.broadcasted_iota(jnp.int32, (tm, tm), 0)
    c = lax.broadcasted_iota(jnp.int32, (tm, tm), 1)
    tri = jnp.where(c < r, 1.0, 0.0).astype(BF16)
    before = _dot(tri, onehot.astype(BF16)) + carry_ref[...]
    rank1 = rsum(jnp.where(sel1, before, 0.0))
    rank2 = rsum(jnp.where(sel2, before, 0.0))
    carry_ref[...] = carry_ref[...] + jnp.sum(onehot, axis=0, keepdims=True)
    cnt_ref[...] = carry_ref[...]

    info = jnp.zeros_like(lg)
    for k, val in enumerate((i1, i2, rank1, rank2, gate1, gate2)):
        info = jnp.where(lane == k, val, info)
    info_ref[...] = info
    infot_ref[...] = info.T[0:8, :]


def _route(logits, tm):
    n = logits.shape[0]
    kernel = functools.partial(_route_kernel, tm=tm)
    return pl.pallas_call(
        kernel,
        grid=(n // tm,),
        in_specs=[pl.BlockSpec((tm, LANES), lambda i: (i, 0))],
        out_specs=(pl.BlockSpec((tm, LANES), lambda i: (i, 0)),
                   pl.BlockSpec((8, tm), lambda i: (0, i)),
                   pl.BlockSpec((1, LANES), lambda i: (0, 0))),
        out_shape=(jax.ShapeDtypeStruct((n, LANES), F32), jax.ShapeDtypeStruct((8, n), F32),
                   jax.ShapeDtypeStruct((1, LANES), F32)),
        scratch_shapes=[pltpu.VMEM((1, LANES), F32)],
        compiler_params=_params(("arbitrary",)),
        name="route",
    )(logits)


DISPATCH_CHUNK = 128


def _dispatch_kernel(dest_ref, zflag_ref, xa_ref, xb_ref, xs_hbm, zero_ref, sem, zsem):
    n_all = dest_ref.shape[0] // 2

    zero_ref[...] = jnp.zeros_like(zero_ref)

    def zero_copy(blk):
        return pltpu.make_async_copy(zero_ref, xs_hbm.at[pl.ds(blk * MOE_BLOCK, MOE_BLOCK)], zsem)

    def start_zero(blk, carry):
        @pl.when(zflag_ref[blk] != 0)
        def _():
            zero_copy(blk).start()
        return carry

    def wait_zero(blk, carry):
        @pl.when(zflag_ref[blk] != 0)
        def _():
            zero_copy(blk).wait()
        return carry

    lax.fori_loop(0, zflag_ref.shape[0], start_zero, 0)
    lax.fori_loop(0, zflag_ref.shape[0], wait_zero, 0)

    def scatter(src_ref, t_off):
        def row_copy(t, k):
            return pltpu.make_async_copy(src_ref.at[pl.ds(t, 1)],
                                         xs_hbm.at[pl.ds(dest_ref[k * n_all + t_off + t], 1)], sem)

        def issue(c):
            def body(r, carry):
                row_copy(c * DISPATCH_CHUNK + r, 0).start(priority=0)
                row_copy(c * DISPATCH_CHUNK + r, 1).start(priority=1)
                return carry
            lax.fori_loop(0, DISPATCH_CHUNK, body, 0, unroll=8)

        def drain(c):
            def body(r, carry):
                row_copy(c * DISPATCH_CHUNK + r, 0).wait()
                row_copy(c * DISPATCH_CHUNK + r, 1).wait()
                return carry
            lax.fori_loop(0, DISPATCH_CHUNK, body, 0, unroll=8)

        n_chunks = src_ref.shape[0] // DISPATCH_CHUNK

        def chunk(c, carry):
            issue(c)

            @pl.when(c > 0)
            def _():
                drain(c - 1)
            return carry

        lax.fori_loop(0, n_chunks, chunk, 0)
        drain(n_chunks - 1)

    scatter(xa_ref, 0)
    scatter(xb_ref, xa_ref.shape[0])


def _dispatch(dest_flat, zero_flag, xa, xb):
    w = xa.shape[1]
    n_slots = zero_flag.shape[0] * MOE_BLOCK
    vmem = pl.BlockSpec(memory_space=pltpu.VMEM)
    grid_spec = pltpu.PrefetchScalarGridSpec(
        num_scalar_prefetch=2,
        grid=(1,),
        in_specs=[vmem, vmem],
        out_specs=pl.BlockSpec(memory_space=pl.ANY),
        scratch_shapes=[pltpu.VMEM((MOE_BLOCK, w), xa.dtype), pltpu.SemaphoreType.DMA(()),
                        pltpu.SemaphoreType.DMA(())],
    )
    return pl.pallas_call(
        _dispatch_kernel,
        grid_spec=grid_spec,
        out_shape=jax.ShapeDtypeStruct((n_slots, w), xa.dtype),
        compiler_params=pltpu.CompilerParams(dimension_semantics=("arbitrary",),
                                             vmem_limit_bytes=VMEM_LIMIT, has_side_effects=True),
        name="dispatch",
    )(dest_flat, zero_flag, xa, xb)


CAST_CHUNK_ELEMS = 64 * 1024
WEIGHT_DMA_SPLIT = 4
WEIGHT_DMA_PRIORITY = 1


def _expert_kernel(be_ref, nu_ref, nx_ref, xs_ref, w1_hbm, w3_hbm, w2_hbm, y_ref,
                   w1f_ref, w3f_ref, w2f_ref, w1b_ref, w3b_ref, w2b_ref, sem):
    b = pl.program_id(0)
    active = b < nu_ref[0]
    new_expert = jnp.logical_or(b == 0, be_ref[b] != be_ref[jnp.maximum(b - 1, 0)])

    def fetch(e):
        copies = []
        for k, (src, dst) in enumerate(((w1_hbm, w1f_ref), (w3_hbm, w3f_ref), (w2_hbm, w2f_ref))):
            slab = dst.shape[0] // WEIGHT_DMA_SPLIT
            for c in range(WEIGHT_DMA_SPLIT):
                rows = pl.ds(c * slab, slab)
                copies.append(pltpu.make_async_copy(src.at[e, rows], dst.at[rows], sem.at[k]))
        return copies

    @pl.when(b == 0)
    def _():
        for copy in fetch(be_ref[0]):
            copy.start(priority=WEIGHT_DMA_PRIORITY)

    @pl.when(jnp.logical_and(active, new_expert))
    def _():
        for copy in fetch(be_ref[b]):
            copy.wait()
        for src, dst in ((w1f_ref, w1b_ref), (w3f_ref, w3b_ref), (w2f_ref, w2b_ref)):
            rows, cols = src.shape
            chunk = CAST_CHUNK_ELEMS // cols

            def cast_rows(c, carry, src=src, dst=dst, chunk=chunk):
                r0 = pl.multiple_of(c * chunk, chunk)
                dst[pl.ds(r0, chunk), :] = src[pl.ds(r0, chunk), :].astype(BF16)
                return carry

            lax.fori_loop(0, rows // chunk, cast_rows, 0)

        @pl.when(nx_ref[b] >= 0)
        def _():
            for copy in fetch(nx_ref[b]):
                copy.start(priority=WEIGHT_DMA_PRIORITY)

    @pl.when(active)
    def _():
        lo, hi = _unpack_bf16_pair(xs_ref[...])
        half = D_MODEL // 2
        h1 = _dot(lo, w1b_ref[:half, :]) + _dot(hi, w1b_ref[half:, :])
        h3 = _dot(lo, w3b_ref[:half, :]) + _dot(hi, w3b_ref[half:, :])
        h = (h1 * jax.nn.sigmoid(h1) * h3).astype(BF16)
        y = _dot(h, w2b_ref[...]).astype(BF16).astype(F32)
        y_ref[...] = _pack_bf16_pair(y[:, :half], y[:, half:])

    @pl.when(jnp.logical_not(active))
    def _():
        y_ref[...] = jnp.zeros_like(y_ref)


def _experts(block_e, n_used, next_e, xs, w1, w3, w2):
    n_slots = xs.shape[0]
    nb = n_slots // MOE_BLOCK
    blk = lambda b, be, nu, nx: (jnp.minimum(b, nu[0] - 1), 0)
    hbm = pl.BlockSpec(memory_space=pl.ANY)
    grid_spec = pltpu.PrefetchScalarGridSpec(
        num_scalar_prefetch=3,
        grid=(nb,),
        in_specs=[pl.BlockSpec((MOE_BLOCK, D_MODEL // 2), blk), hbm, hbm, hbm],
        out_specs=pl.BlockSpec((MOE_BLOCK, D_MODEL // 2), lambda b, be, nu, nx: (b, 0)),
        scratch_shapes=[pltpu.VMEM((D_MODEL, D_EXPERT), F32), pltpu.VMEM((D_MODEL, D_EXPERT), F32),
                        pltpu.VMEM((D_EXPERT, D_MODEL), F32),
                        pltpu.VMEM((D_MODEL, D_EXPERT), BF16), pltpu.VMEM((D_MODEL, D_EXPERT), BF16),
                        pltpu.VMEM((D_EXPERT, D_MODEL), BF16),
                        pltpu.SemaphoreType.DMA((3,))],
    )
    return pl.pallas_call(
        _expert_kernel,
        grid_spec=grid_spec,
        out_shape=jax.ShapeDtypeStruct((n_slots, D_MODEL // 2), jnp.uint32),
        compiler_params=_params(("arbitrary",)),
        name="experts",
    )(block_e, n_used, next_e, xs, w1, w3, w2)


PLE_SPLIT = 4
GATHER_ISSUE_SLABS = 2


def _combine_ple_kernel(dest_ref, h_ref, info_ref, p_ref, g_ref, *rest, tm, t_off):
    wpg_refs = rest[:PLE_SPLIT]
    wpp_ref, yb_hbm, o_ref, buf_ref, sem = rest[PLE_SPLIT:]
    i = pl.program_id(0)
    last = pl.num_programs(0) - 1
    n_all = dest_ref.shape[0] // 2
    slot = i % 2

    def row_copy(step, into, r, k):
        t = t_off + step * tm + r
        return pltpu.make_async_copy(yb_hbm.at[pl.ds(dest_ref[k * n_all + t], 1)],
                                     buf_ref.at[into, k, pl.ds(r, 1)], sem.at[into])

    def drain(into):
        def body(r, carry):
            for k in range(2):
                pltpu.make_async_copy(yb_hbm.at[pl.ds(0, 1)], buf_ref.at[into, k, pl.ds(r, 1)],
                                      sem.at[into]).wait()
            return carry
        lax.fori_loop(0, tm, body, 0, unroll=8)

    @pl.when(i == 0)
    def _():
        def body(r, carry):
            row_copy(0, 0, r, 0).start()
            row_copy(0, 0, r, 1).start()
            return carry
        lax.fori_loop(0, tm, body, 0, unroll=8)

    drain(slot)
    lo0, hi0 = _unpack_f32_pair(buf_ref[slot, 0])
    lo1, hi1 = _unpack_f32_pair(buf_ref[slot, 1])
    g0, g1 = info_ref[:, 4:5], info_ref[:, 5:6]
    h2 = h_ref[...] + jnp.concatenate([g0 * lo0 + g1 * lo1, g0 * hi0 + g1 * hi1], axis=1)
    hn = (h2 * lax.rsqrt(jnp.mean(h2 * h2, axis=-1, keepdims=True) + EPS) * g_ref[...]).astype(BF16)
    pb = p_ref[...].astype(BF16)

    nxt = jnp.minimum(i + 1, last)
    ch = D_MODEL // PLE_SPLIT
    per = tm // GATHER_ISSUE_SLABS
    for k in range(PLE_SPLIT):
        for r in range(k * per, (k + 1) * per if k < GATHER_ISSUE_SLABS else k * per):
            row_copy(nxt, 1 - slot, r, 0).start(priority=0)
            row_copy(nxt, 1 - slot, r, 1).start(priority=1)
        cs = slice(k * ch, (k + 1) * ch)
        gate = jax.nn.sigmoid(_dot(hn, wpg_refs[k][...]))
        o_ref[:, cs] = h2[:, cs] + gate * _dot(pb, wpp_ref[:, cs])

    @pl.when(i == last)
    def _():
        drain(1 - slot)


def _combine_ple(dest_flat, h, info, yb, p, g_ple, w_pg, w_pp, t_off, tm):
    n = h.shape[0]
    kernel = functools.partial(_combine_ple_kernel, tm=tm, t_off=t_off)
    ob = t_off // tm
    row = lambda w: pl.BlockSpec((tm, w), lambda i, d: (i, 0))
    grid_spec = pltpu.PrefetchScalarGridSpec(
        num_scalar_prefetch=1,
        grid=(n // tm,),
        in_specs=[row(D_MODEL), pl.BlockSpec((tm, LANES), lambda i, d: (i + ob, 0)), row(PLE_DIM),
                  _resident((1, D_MODEL))]
        + [_resident((D_MODEL, D_MODEL // PLE_SPLIT), (0, k)) for k in range(PLE_SPLIT)]
        + [_resident((PLE_DIM, D_MODEL)), pl.BlockSpec(memory_space=pl.ANY)],
        out_specs=row(D_MODEL),
        scratch_shapes=[pltpu.VMEM((2, 2, tm, D_MODEL // 2), jnp.uint32), pltpu.SemaphoreType.DMA((2,))],
    )
    return pl.pallas_call(
        kernel,
        grid_spec=grid_spec,
        out_shape=jax.ShapeDtypeStruct((n, D_MODEL), F32),
        compiler_params=_params(("arbitrary",)),
        name="combine_ple",
    )(dest_flat, h, info, p, g_ple, *([w_pg] * PLE_SPLIT), w_pp, yb)


def _mixer_tokens(x2d, wts, tm):
    return _in_proj(x2d, wts["g_mix"], wts["w_parts"], wts["w_f"], wts["b_f"],
                    wts["q_gain"], wts["k_gain"], tm)


def kernel(x_prompt, x_sample, cache_k, cache_v, cache_logf, cache_conv, p_prompt, p_sample,
           g_mix, w_in, b_f, q_gain, k_gain, w_dw, b_dw, ln_g, ln_b, gc, ga, w_out,
           g_ffn, w_router_g, b_router_g, w_router_e, b_router_e, w1, w3, w2,
           g_ple, w_pg, w_pp):
    batch, seq, _ = x_prompt.shape
    dec_batch, t_new, _ = x_sample.shape
    past = cache_k.shape[2]
    n_p = batch * seq
    n_s = dec_batch * t_new
    n_all = n_p + n_s
    tm = 256
    li = 0

    w_in_l = w_in[li]
    pad_lanes = lambda a: jnp.pad(a, ((0, 0), (0, LANES - a.shape[1])))
    row2d = lambda a: a.reshape(1, -1)
    wts = {
        "g_mix": row2d(g_mix[li]),
        "w_parts": [w_in_l[:, c:c + CONV_WIDTH].astype(BF16) for c in range(0, MAIN_COLS, CONV_WIDTH)],
        "w_f": pad_lanes(w_in_l[:, MAIN_COLS:]).astype(BF16),
        "b_f": pad_lanes(row2d(b_f[li])),
        "q_gain": row2d(q_gain[li]),
        "k_gain": row2d(k_gain[li]),
    }
    w_dw_p = jnp.pad(w_dw[li], ((0, CONV_HALO - CONV_KERNEL), (0, 0)))
    w_out_b = w_out[li].astype(BF16)
    w_r = pad_lanes(jnp.concatenate([w_router_e[li], w_router_g[li]], axis=1))
    wr_hi = w_r.astype(BF16)
    wr_lo = (w_r - wr_hi.astype(F32)).astype(BF16)
    b_r = pad_lanes(row2d(jnp.concatenate([b_router_e[li], b_router_g[li]])))
    w_pg_b, w_pp_b = w_pg[li].astype(BF16), w_pp[li].astype(BF16)

    xp = x_prompt.reshape(n_p, D_MODEL)
    u_p, q_p, k_p, kb_p, v_p, _, vbt_p, lf_p = _mixer_tokens(xp, wts, tm)
    lf_p_row = lf_p.reshape(batch, seq, N_HEADS).transpose(0, 2, 1)
    f_p_row = _cumsum_lanes(lf_p_row.reshape(batch * N_HEADS, seq))
    ctx_p = jnp.zeros((batch, CONV_HALO, CONV_WIDTH), F32)
    conv_args = (w_dw_p, row2d(b_dw[li]), row2d(ln_g[li]), row2d(ln_b[li]), row2d(gc[li]))
    yc_p = _conv_module(u_p, ctx_p, *conv_args, batch, seq, tm)
    qa_p, ka_p = _attn_aug(f_p_row, batch, seq, 2 * tm)
    ya_p = _attn_prompt(q_p, qa_p, kb_p, ka_p, vbt_p, batch, seq, 256, 256)
    merge_args = (row2d(ga[li]), w_out_b, row2d(g_ffn[li]), jnp.concatenate([wr_hi, wr_lo], axis=1), b_r)
    h_p, xpk_p, lg_p = _merge(yc_p, ya_p, xp, *merge_args, tm)

    xs_ = x_sample.reshape(n_s, D_MODEL)
    u_s, q_s, k_s, kb_s, v_s, vb_s, _, lf_s = _mixer_tokens(xs_, wts, tm)
    clf_row = cache_logf[li].transpose(0, 2, 1).reshape(dec_batch * N_HEADS, past)
    fc_row = _cumsum_lanes(clf_row).reshape(dec_batch, N_HEADS, past)
    lf_s_row = lf_s.reshape(dec_batch, t_new, N_HEADS).transpose(0, 2, 1).reshape(dec_batch * N_HEADS, t_new)
    fn_row = _cumsum_lanes(jnp.pad(lf_s_row, ((0, 0), (0, LANES - t_new)))).reshape(dec_batch, N_HEADS, LANES)
    fn_col = fn_row[:, :, :t_new].transpose(0, 2, 1).reshape(n_s, N_HEADS)
    ctx_s = jnp.pad(cache_conv[li], ((0, 0), (CONV_HALO - CONV_STATE, 0), (0, 0)))
    yc_s = _conv_module(u_s, ctx_s, *conv_args, dec_batch, t_new, t_new)
    ya_s = _attn_sample(q_s, kb_s, vb_s,
                        cache_k[li], cache_v[li],
                        fc_row, fn_row, fn_col, dec_batch, t_new, past)
    h_s, xpk_s, lg_s = _merge(yc_s, ya_s, xs_, *merge_args, tm)

    info, info_t, counts = _route(jnp.concatenate([lg_p, lg_s], axis=0), tm)
    counts = counts[0, :N_EXPERTS].astype(jnp.int32)
    bcounts = (counts + MOE_BLOCK - 1) // MOE_BLOCK
    bends = jnp.cumsum(bcounts)
    pstarts = (bends - bcounts) * MOE_BLOCK
    n_rows = n_all * 2
    nb = -(-(n_rows + N_EXPERTS * (MOE_BLOCK - 1)) // MOE_BLOCK)
    e_idx = info_t[0:2].astype(jnp.int32)
    expert_ids = jnp.arange(N_EXPERTS, dtype=jnp.int32)[:, None, None]
    seg_start = jnp.sum(jnp.where(e_idx[None] == expert_ids, pstarts[:, None, None], 0), axis=0)
    dest = (seg_start + info_t[2:4].astype(jnp.int32)).reshape(n_rows)
    block_e = jnp.minimum(jnp.sum(bends[None, :] <= jnp.arange(nb, dtype=jnp.int32)[:, None], axis=1),
                          N_EXPERTS - 1).astype(jnp.int32)
    n_used = bends[N_EXPERTS - 1:].astype(jnp.int32)
    seg_end = jnp.sum(jnp.where(block_e[None, :] == expert_ids[:, :, 0], bends[:, None], 0), axis=0)
    block_ids = jnp.arange(nb, dtype=jnp.int32)
    zero_flag = ((block_ids + 1 == seg_end) | (block_ids >= n_used[0])).astype(jnp.int32)
    xs_sorted = _dispatch(dest, zero_flag, xpk_p, xpk_s)
    next_e = jnp.where(seg_end < n_used[0], block_e[jnp.minimum(seg_end, nb - 1)], -1).astype(jnp.int32)
    yb = _experts(block_e, n_used, next_e, xs_sorted, w1[li], w3[li], w2[li])

    ple_args = (row2d(g_ple[li]), w_pg_b, w_pp_b)
    y_p = _combine_ple(dest, h_p, info, yb, p_prompt[li].reshape(n_p, PLE_DIM), *ple_args, 0, tm)
    y_s = _combine_ple(dest, h_s, info, yb, p_sample[li].reshape(n_s, PLE_DIM), *ple_args, n_p, tm)

    heads = lambda a, b, t: a.reshape(1, b, t, N_HEADS, HEAD_DIM)
    return (
        y_p.reshape(batch, seq, D_MODEL),
        y_s.reshape(dec_batch, t_new, D_MODEL),
        heads(k_p, batch, seq), heads(v_p, batch, seq),
        lf_p.reshape(1, batch, seq, N_HEADS),
        u_p.reshape(batch, seq, CONV_WIDTH)[None, :, seq - CONV_STATE:, :],
        heads(k_s, dec_batch, t_new), heads(v_s, dec_batch, t_new),
        lf_s.reshape(1, dec_batch, t_new, N_HEADS),
        u_s.reshape(dec_batch, t_new, CONV_WIDTH)[None, :, t_new - CONV_STATE:, :],
    )
```

```python
import functools

import jax
import jax.numpy as jnp
from jax import lax
from jax.experimental import pallas as pl
from jax.experimental.pallas import tpu as pltpu

D_MODEL = 2048
CONV_WIDTH = 1024
ATTN_WIDTH = 1024
HEAD_DIM = 128
N_HEADS = 8
CONV_KERNEL = 31
CONV_STATE = CONV_KERNEL - 1
N_GROUPS = 4
EXPERTS_PER_GROUP = 8
N_EXPERTS = 32
D_EXPERT = 512
PLE_DIM = 256
MOE_BLOCK = 256
EPS = 1e-6
MAIN_COLS = 2 * CONV_WIDTH + 3 * ATTN_WIDTH

LANES = 128
CONV_HALO = 32
VMEM_LIMIT = 56 * 1024 * 1024

F32 = jnp.float32
BF16 = jnp.bfloat16
NEG_BIG = -1e30
LOG2E = 1.4426950408889634


def _dot(a, b):
    return jnp.dot(a, b, preferred_element_type=F32)


def _params(sem):
    return pltpu.CompilerParams(dimension_semantics=sem, vmem_limit_bytes=VMEM_LIMIT)


def _resident(shape, index=None):
    index = (0,) * len(shape) if index is None else index
    return pl.BlockSpec(shape, lambda *_: index, pipeline_mode=pl.Buffered(1))


def _inproj_kernel(x_ref, g_ref, wval_ref, wgate_ref, wq_ref, wk_ref, wv_ref, wf_ref, bf_ref, qg_ref, kg_ref,
                   u_ref, q_ref, k_ref, kb_ref, v_ref, vb_ref, vbt_ref, lf_ref):
    x = x_ref[...]
    ms = jnp.mean(x * x, axis=-1, keepdims=True)
    a = (x * lax.rsqrt(ms + EPS) * g_ref[...]).astype(BF16)

    ch = 256
    for c in range(0, CONV_WIDTH, ch):
        val = _dot(a, wval_ref[:, c:c + ch])
        gate = _dot(a, wgate_ref[:, c:c + ch])
        u_ref[:, c:c + ch] = val * jax.nn.sigmoid(gate)

    def head_norm(z, gain):
        return z * lax.rsqrt(jnp.mean(z * z, axis=-1, keepdims=True) + EPS) * gain

    scale = LOG2E * HEAD_DIM ** -0.5
    for c in range(0, ATTN_WIDTH, ch):
        zq = _dot(a, wq_ref[:, c:c + ch])
        zk = _dot(a, wk_ref[:, c:c + ch])
        zv = _dot(a, wv_ref[:, c:c + ch])
        for s in range(0, ch, HEAD_DIM):
            qn = head_norm(zq[:, s:s + HEAD_DIM], qg_ref[...])
            kn = head_norm(zk[:, s:s + HEAD_DIM], kg_ref[...])
            q_ref[:, c + s:c + s + HEAD_DIM] = (qn * scale).astype(BF16)
            k_ref[:, c + s:c + s + HEAD_DIM] = kn
            kb_ref[:, c + s:c + s + HEAD_DIM] = kn.astype(BF16)
        v_ref[:, c:c + ch] = zv
        vb_ref[:, c:c + ch] = zv.astype(BF16)
        vbt_ref[c:c + ch, :] = zv.T.astype(BF16)

    f = _dot(a, wf_ref[...]) + bf_ref[...]
    lf = jnp.minimum(f, 0.0) - jnp.log1p(jnp.exp(-jnp.abs(f)))
    lf_ref[...] = lf[:, :N_HEADS]


def _in_proj(x, g_mix, w_parts, w_f, b_f, q_gain, k_gain, tm):
    n = x.shape[0]
    row = lambda w: pl.BlockSpec((tm, w), lambda i: (i, 0))
    out_shape = (
        jax.ShapeDtypeStruct((n, CONV_WIDTH), F32),
        jax.ShapeDtypeStruct((n, ATTN_WIDTH), BF16),
        jax.ShapeDtypeStruct((n, ATTN_WIDTH), F32),
        jax.ShapeDtypeStruct((n, ATTN_WIDTH), BF16),
        jax.ShapeDtypeStruct((n, ATTN_WIDTH), F32),
        jax.ShapeDtypeStruct((n, ATTN_WIDTH), BF16),
        jax.ShapeDtypeStruct((ATTN_WIDTH, n), BF16),
        jax.ShapeDtypeStruct((n, N_HEADS), F32),
    )
    return pl.pallas_call(
        _inproj_kernel,
        grid=(n // tm,),
        in_specs=[row(D_MODEL), _resident((1, D_MODEL))] + [_resident((D_MODEL, CONV_WIDTH))] * 5 + [
                  _resident((D_MODEL, LANES)), _resident((1, LANES)),
                  _resident((1, HEAD_DIM)), _resident((1, HEAD_DIM))],
        out_specs=(row(CONV_WIDTH), row(ATTN_WIDTH), row(ATTN_WIDTH), row(ATTN_WIDTH),
                   row(ATTN_WIDTH), row(ATTN_WIDTH),
                   pl.BlockSpec((ATTN_WIDTH, tm), lambda i: (0, i)), row(N_HEADS)),
        out_shape=out_shape,
        compiler_params=_params(("parallel",)),
        name="in_proj",
    )(x, g_mix, *w_parts, w_f, b_f, q_gain, k_gain)


def _cumsum_kernel(x_ref, o_ref):
    x = x_ref[...]
    width = x.shape[1]
    lane = lax.broadcasted_iota(jnp.int32, x.shape, 1)
    s = 1
    while s < width:
        x = x + jnp.where(lane >= s, pltpu.roll(x, s, axis=1), 0.0)
        s *= 2
    o_ref[...] = x


def _cumsum_lanes(x):
    return pl.pallas_call(
        _cumsum_kernel,
        out_shape=jax.ShapeDtypeStruct(x.shape, F32),
        name="cumsum",
    )(x)


def _conv_kernel(u_ref, halo_ref, ctx_ref, w_ref, bdw_ref, lng_ref, lnb_ref, gc_ref,
                 o_ref, ext_ref, y_ref, *, tm):
    i = pl.program_id(1)

    @pl.when(i == 0)
    def _():
        ext_ref[0:CONV_HALO, :] = ctx_ref[0]

    @pl.when(i > 0)
    def _():
        ext_ref[0:CONV_HALO, :] = halo_ref[...]

    ext_ref[CONV_HALO:CONV_HALO + tm, :] = u_ref[...]

    rows = min(64, tm)
    ch = LANES
    sub = 8
    wlen = rows + CONV_HALO
    first = CONV_HALO - CONV_STATE

    def conv_rows(r, carry):
        r0 = pl.multiple_of(r * rows, rows)
        for c in range(0, CONV_WIDTH, ch):
            acc = jnp.zeros((rows, ch), F32)
            win = ext_ref[pl.ds(r0, wlen), c:c + ch]
            for rho in range(sub):
                sh = win if rho == 0 else pltpu.roll(win, wlen - rho, axis=0)
                for a in range(wlen // sub):
                    j = sub * a + rho - first
                    if 0 <= j < CONV_KERNEL:
                        acc = acc + sh[sub * a:sub * a + rows] * w_ref[j:j + 1, c:c + ch]
            y_ref[pl.ds(r0, rows), c:c + ch] = acc
        return carry

    lax.fori_loop(0, tm // rows, conv_rows, 0)

    nrows = min(128, tm)

    def norm_rows(r, carry):
        r0 = pl.multiple_of(r * nrows, nrows)
        y = y_ref[pl.ds(r0, nrows), :] + bdw_ref[...]
        mu = jnp.mean(y, axis=-1, keepdims=True)
        yc = y - mu
        var = jnp.mean(yc * yc, axis=-1, keepdims=True)
        z = yc * lax.rsqrt(var + EPS) * lng_ref[...] + lnb_ref[...]
        s = z * jax.nn.sigmoid(z)
        ms = jnp.mean(s * s, axis=-1, keepdims=True)
        o_ref[pl.ds(r0, nrows), :] = (s * lax.rsqrt(ms + EPS) * gc_ref[...]).astype(BF16)
        return carry

    lax.fori_loop(0, tm // nrows, norm_rows, 0)


def _conv_module(u, ctx, w_dw, b_dw, ln_g, ln_b, gc, batch, seq, tm):
    nt = seq // tm
    hb = tm // CONV_HALO
    kernel = functools.partial(_conv_kernel, tm=tm)
    return pl.pallas_call(
        kernel,
        grid=(batch, nt),
        in_specs=[
            pl.BlockSpec((tm, CONV_WIDTH), lambda b, i: (b * nt + i, 0)),
            pl.BlockSpec((CONV_HALO, CONV_WIDTH),
                         lambda b, i: (jnp.maximum((b * nt + i) * hb - 1, 0), 0)),
            pl.BlockSpec((1, CONV_HALO, CONV_WIDTH), lambda b, i: (b, 0, 0)),
            _resident((CONV_HALO, CONV_WIDTH)),
            _resident((1, CONV_WIDTH)), _resident((1, CONV_WIDTH)),
            _resident((1, CONV_WIDTH)), _resident((1, CONV_WIDTH)),
        ],
        out_specs=pl.BlockSpec((tm, CONV_WIDTH), lambda b, i: (b * nt + i, 0)),
        out_shape=jax.ShapeDtypeStruct((batch * seq, CONV_WIDTH), BF16),
        scratch_shapes=[pltpu.VMEM((CONV_HALO + tm, CONV_WIDTH), F32),
                        pltpu.VMEM((tm, CONV_WIDTH), F32)],
        compiler_params=_params(("parallel", "arbitrary")),
        name="conv_module",
    )(u, u, ctx, w_dw, b_dw, ln_g, ln_b, gc)


def _qk(q, k):
    return lax.dot_general(q, k, (((1,), (1,)), ((), ())), preferred_element_type=F32)


AUG_TERMS = 3
AUG_STRIDE = LANES // N_HEADS
QK_AHEAD = 4


def _aug_kernel(f_ref, qa_ref, ka_ref):
    f = f_ref[...] * LOG2E
    tm = f.shape[1]
    row = lax.broadcasted_iota(jnp.int32, (LANES, tm), 0)
    sub = row % AUG_STRIDE
    qa = jnp.where((sub >= AUG_TERMS) & (sub < 2 * AUG_TERMS), 1.0, 0.0)
    ka = jnp.where(sub < AUG_TERMS, 1.0, 0.0)
    for h in range(N_HEADS):
        rest = f[h:h + 1, :]
        for t in range(AUG_TERMS):
            piece = rest.astype(BF16).astype(F32)
            rest = rest - piece
            qa = jnp.where(row == h * AUG_STRIDE + t, piece, qa)
            ka = jnp.where(row == h * AUG_STRIDE + AUG_TERMS + t, -piece, ka)
    qa_ref[...] = qa.T.astype(BF16)
    ka_ref[...] = ka.T.astype(BF16)


def _attn_aug(f_row, batch, seq, tm):
    nt = seq // tm
    out = pl.BlockSpec((tm, LANES), lambda b, i: (b * nt + i, 0))
    return pl.pallas_call(
        _aug_kernel,
        grid=(batch, nt),
        in_specs=[pl.BlockSpec((N_HEADS, tm), lambda b, i: (b, i))],
        out_specs=(out, out),
        out_shape=(jax.ShapeDtypeStruct((batch * seq, LANES), BF16),
                   jax.ShapeDtypeStruct((batch * seq, LANES), BF16)),
        compiler_params=_params(("parallel", "parallel")),
        name="attn_aug",
    )(f_row)


def _attn_prompt_kernel(q_ref, qa_ref, k_ref, ka_ref, vt_ref, o_ref, m_ref, l_ref, acc_ref, qt_ref,
                        sp_ref, *, tq, tk):
    i = pl.program_id(1)
    key = lax.broadcasted_iota(jnp.int32, (tk, tq), 0)
    qry = lax.broadcasted_iota(jnp.int32, (tk, tq), 1)

    m_ref[...] = jnp.full(m_ref.shape, NEG_BIG, F32)
    l_ref[...] = jnp.zeros(l_ref.shape, F32)
    acc_ref[...] = jnp.zeros(acc_ref.shape, F32)
    aug_lane = lax.broadcasted_iota(jnp.int32, (tq, LANES), 1)
    for h in range(N_HEADS):
        hs = slice(h * HEAD_DIM, (h + 1) * HEAD_DIM)
        own = (aug_lane >= h * AUG_STRIDE) & (aug_lane < (h + 1) * AUG_STRIDE)
        qa = jnp.where(own, qa_ref[...], jnp.zeros_like(qa_ref))
        qt_ref[h] = jnp.concatenate([q_ref[:, hs], qa], axis=1).T

    def scores(ks, h):
        hs = slice(h * HEAD_DIM, (h + 1) * HEAD_DIM)
        kf = jnp.concatenate([k_ref[pl.ds(ks, tk), hs], ka_ref[pl.ds(ks, tk), :]], axis=1)
        return _dot(kf, qt_ref[h])

    def tile_step(ks, ks_next, masked):
        pending = [sp_ref[a] for a in range(QK_AHEAD)]
        for h in range(N_HEADS):
            hs = slice(h * HEAD_DIM, (h + 1) * HEAD_DIM)
            s = pending.pop(0)
            if h + QK_AHEAD < N_HEADS:
                pending.append(scores(ks, h + QK_AHEAD))
            elif ks_next is not None:
                pending.append(scores(ks_next, h + QK_AHEAD - N_HEADS))
            if masked:
                s = jnp.where(key <= qry, s, -jnp.inf)
            m = m_ref[h]
            m_new = jnp.maximum(m, jnp.max(s, axis=0, keepdims=True))
            alpha = jnp.exp2(m - m_new)
            p = jnp.exp2(s - m_new)
            m_ref[h] = m_new
            l_ref[h] = alpha * l_ref[h] + jnp.sum(p, axis=0, keepdims=True)
            pv = _dot(vt_ref[hs, pl.ds(ks, tk)], p.astype(BF16))
            acc_ref[h] = alpha * acc_ref[h] + pv
        for a, s in enumerate(pending):
            sp_ref[a] = s

    def body(j, carry):
        tile_step(pl.multiple_of(j * tk, tk), pl.multiple_of((j + 1) * tk, tk), False)
        return carry

    for a in range(QK_AHEAD):
        sp_ref[a] = scores(0, a)
    lax.fori_loop(0, i, body, 0)
    tile_step(pl.multiple_of(i * tk, tk), None, True)
    for h in range(N_HEADS):
        o_ref[:, h * HEAD_DIM:(h + 1) * HEAD_DIM] = (acc_ref[h] / l_ref[h]).T


def _attn_prompt(q, qa, kb, ka, vbt, batch, seq, tq, tk):
    nq = seq // tq
    kernel = functools.partial(_attn_prompt_kernel, tq=tq, tk=tk)
    qblk = pl.BlockSpec((tq, ATTN_WIDTH), lambda b, i: (b * nq + i, 0))
    kblk = pl.BlockSpec((seq, ATTN_WIDTH), lambda b, i: (b, 0), pipeline_mode=pl.Buffered(1))
    vblk = pl.BlockSpec((ATTN_WIDTH, seq), lambda b, i: (0, b), pipeline_mode=pl.Buffered(1))
    return pl.pallas_call(
        kernel,
        grid=(batch, nq),
        in_specs=[qblk, pl.BlockSpec((tq, LANES), lambda b, i: (b * nq + i, 0)), kblk,
                  pl.BlockSpec((seq, LANES), lambda b, i: (b, 0), pipeline_mode=pl.Buffered(1)), vblk],
        out_specs=pl.BlockSpec((tq, ATTN_WIDTH), lambda b, i: (b * nq + i, 0)),
        out_shape=jax.ShapeDtypeStruct((batch * seq, ATTN_WIDTH), F32),
        scratch_shapes=[pltpu.VMEM((N_HEADS, 1, tq), F32), pltpu.VMEM((N_HEADS, 1, tq), F32),
                        pltpu.VMEM((N_HEADS, HEAD_DIM, tq), F32),
                        pltpu.VMEM((N_HEADS, 2 * HEAD_DIM, tq), BF16),
                        pltpu.VMEM((QK_AHEAD, tk, tq), F32)],
        compiler_params=_params(("parallel", "arbitrary")),
        name="attn_prompt",
    )(q, qa, kb, ka, vbt)


def _attn_sample_kernel(q_ref, kn_ref, vn_ref, ck_hbm, cv_hbm, fc_ref, fnrow_ref, fncol_ref,
                        o_ref, kbuf_ref, vbuf_ref, sem, *, t_new, past):
    b = pl.program_id(0)
    slot = b % 2

    def fetch(batch_idx, into):
        copies = []
        for h in range(N_HEADS):
            copies.append(pltpu.make_async_copy(ck_hbm.at[batch_idx, :, h, :], kbuf_ref.at[into, h],
                                                sem.at[into, 0]))
            copies.append(pltpu.make_async_copy(cv_hbm.at[batch_idx, :, h, :], vbuf_ref.at[into, h],
                                                sem.at[into, 1]))
        return copies

    @pl.when(b == 0)
    def _():
        for copy in fetch(0, 0):
            copy.start()

    @pl.when(b + 1 < pl.num_programs(0))
    def _():
        for copy in fetch(b + 1, 1 - slot):
            copy.start()

    for copy in fetch(b, slot):
        copy.wait()

    row = lax.broadcasted_iota(jnp.int32, (t_new, t_new), 0)
    col = lax.broadcasted_iota(jnp.int32, (t_new, t_new), 1)
    causal = col <= row
    for h in range(N_HEADS):
        hs = slice(h * HEAD_DIM, (h + 1) * HEAD_DIM)
        q = q_ref[:, hs]
        fn_q = fncol_ref[:, h:h + 1]
        fc = fc_ref[0, h:h + 1, :]
        fc_last = fc[:, past - 1:past]
        s_c = _qk(q, kbuf_ref[slot, h].astype(BF16)) + ((fc_last + fn_q) - fc) * LOG2E
        fn_k = fnrow_ref[0, h:h + 1, 0:t_new]
        s_n = _qk(q, kn_ref[:, hs]) + (fn_q - fn_k) * LOG2E
        s_n = jnp.where(causal, s_n, -jnp.inf)
        m = jnp.maximum(jnp.max(s_c, axis=-1, keepdims=True), jnp.max(s_n, axis=-1, keepdims=True))
        p_c = jnp.exp2(s_c - m)
        p_n = jnp.exp2(s_n - m)
        l = jnp.sum(p_c, axis=-1, keepdims=True) + jnp.sum(p_n, axis=-1, keepdims=True)
        acc = _dot(p_c.astype(BF16), vbuf_ref[slot, h].astype(BF16)) + _dot(p_n.astype(BF16), vn_ref[:, hs])
        o_ref[:, hs] = acc / l


def _attn_sample(q, kb, vb, cache_k, cache_v, fc_row, fn_row, fn_col, batch, t_new, past):
    kernel = functools.partial(_attn_sample_kernel, t_new=t_new, past=past)
    tok = lambda w: pl.BlockSpec((t_new, w), lambda b: (b, 0))
    return pl.pallas_call(
        kernel,
        grid=(batch,),
        in_specs=[
            tok(ATTN_WIDTH), tok(ATTN_WIDTH), tok(ATTN_WIDTH),
            pl.BlockSpec(memory_space=pl.ANY),
            pl.BlockSpec(memory_space=pl.ANY),
            pl.BlockSpec((1, N_HEADS, past), lambda b: (b, 0, 0)),
            pl.BlockSpec((1, N_HEADS, LANES), lambda b: (b, 0, 0)),
            tok(N_HEADS),
        ],
        out_specs=tok(ATTN_WIDTH),
        out_shape=jax.ShapeDtypeStruct((batch * t_new, ATTN_WIDTH), F32),
        scratch_shapes=[pltpu.VMEM((2, N_HEADS, past, HEAD_DIM), F32),
                        pltpu.VMEM((2, N_HEADS, past, HEAD_DIM), F32),
                        pltpu.SemaphoreType.DMA((2, 2))],
        compiler_params=_params(("arbitrary",)),
        name="attn_sample",
    )(q, kb, vb, cache_k, cache_v, fc_row, fn_row, fn_col)


def _pack_bf16_pair(lo, hi):
    lo_bits = lax.bitcast_convert_type(lo, jnp.uint32) >> 16
    hi_bits = lax.bitcast_convert_type(hi, jnp.uint32) & jnp.uint32(0xFFFF0000)
    return lo_bits | hi_bits


def _unpack_f32_pair(w):
    lo = lax.bitcast_convert_type(w << 16, F32)
    hi = lax.bitcast_convert_type(w & jnp.uint32(0xFFFF0000), F32)
    return lo, hi


def _unpack_bf16_pair(w):
    lo, hi = _unpack_f32_pair(w)
    return lo.astype(BF16), hi.astype(BF16)


def _merge_kernel(yc_ref, ya_ref, x_ref, ga_ref, woc_ref, woa_ref, gf_ref, wr_ref, br_ref,
                  h_ref, xp_ref, lg_ref):
    ya = ya_ref[...]
    ya_n = (ya * lax.rsqrt(jnp.mean(ya * ya, axis=-1, keepdims=True) + EPS) * ga_ref[...]).astype(BF16)
    y = _dot(yc_ref[...], woc_ref[...]) + _dot(ya_n, woa_ref[...])
    h = x_ref[...] + y
    h_ref[...] = h
    xn = h * lax.rsqrt(jnp.mean(h * h, axis=-1, keepdims=True) + EPS) * gf_ref[...]
    xn_hi = xn.astype(BF16)
    xn_hi32 = xn_hi.astype(F32)
    xn_lo = (xn - xn_hi32).astype(BF16)
    hh_hl = _dot(xn_hi, wr_ref[...])
    lg_ref[...] = hh_hl[:, :LANES] + hh_hl[:, LANES:] + _dot(xn_lo, wr_ref[:, :LANES]) + br_ref[...]
    half = D_MODEL // 2
    xp_ref[...] = _pack_bf16_pair(xn_hi32[:, :half], xn_hi32[:, half:])


def _merge(yc_n, ya, x, ga, w_out, g_ffn, wr_cat, b_r, tm):
    n = x.shape[0]
    row = lambda w: pl.BlockSpec((tm, w), lambda i: (i, 0))
    return pl.pallas_call(
        _merge_kernel,
        grid=(n // tm,),
        in_specs=[row(CONV_WIDTH), row(ATTN_WIDTH), row(D_MODEL), _resident((1, ATTN_WIDTH)),
                  _resident((CONV_WIDTH, D_MODEL), (0, 0)), _resident((ATTN_WIDTH, D_MODEL), (1, 0)),
                  _resident((1, D_MODEL)),
                  _resident((D_MODEL, 2 * LANES)), _resident((1, LANES))],
        out_specs=(row(D_MODEL), row(D_MODEL // 2), row(LANES)),
        out_shape=(jax.ShapeDtypeStruct((n, D_MODEL), F32),
                   jax.ShapeDtypeStruct((n, D_MODEL // 2), jnp.uint32),
                   jax.ShapeDtypeStruct((n, LANES), F32)),
        compiler_params=_params(("parallel",)),
        name="merge_out",
    )(yc_n, ya, x, ga, w_out, w_out, g_ffn, wr_cat, b_r)


def _route_kernel(lg_ref, info_ref, infot_ref, cnt_ref, carry_ref, *, tm):
    step = pl.program_id(0)

    @pl.when(step == 0)
    def _():
        carry_ref[...] = jnp.zeros_like(carry_ref)

    lg = lg_ref[...]
    lane = lax.broadcasted_iota(jnp.int32, lg.shape, 1)
    lanef = lane.astype(F32)
    big = jnp.float32(1e9)
    rmax = lambda v: jnp.max(v, axis=-1, keepdims=True)
    rmin = lambda v: jnp.min(v, axis=-1, keepdims=True)
    rsum = lambda v: jnp.sum(v, axis=-1, keepdims=True)

    is_g = (lane >= N_EXPERTS) & (lane < N_EXPERTS + N_GROUPS)
    gl = jnp.where(is_g, lg, NEG_BIG)
    gmax = rmax(gl)
    gsum = rsum(jnp.where(is_g, jnp.exp(gl - gmax), 0.0))
    pg_star = 1.0 / gsum
    g_idx = rmin(jnp.where(is_g & (gl == gmax), lanef - N_EXPERTS, big))

    e_lo = g_idx * EXPERTS_PER_GROUP
    is_e = (lanef >= e_lo) & (lanef < e_lo + EXPERTS_PER_GROUP)
    el = jnp.where(is_e, lg, NEG_BIG)
    m1 = rmax(el)
    i1 = rmin(jnp.where(is_e & (el == m1), lanef, big))
    sel1 = lanef == i1
    el2 = jnp.where(sel1, NEG_BIG, el)
    m2 = rmax(el2)
    i2 = rmin(jnp.where(is_e & (el2 == m2) & jnp.logical_not(sel1), lanef, big))
    sel2 = lanef == i2
    z = rsum(jnp.where(is_e, jnp.exp(el - m1), 0.0))
    p1 = 1.0 / z
    p2 = jnp.exp(m2 - m1) / z
    gate1 = pg_star * p1 / (p1 + p2)
    gate2 = pg_star * p2 / (p1 + p2)

    onehot = jnp.where(sel1 | sel2, 1.0, 0.0)
    r = lax.broadcasted_iota(jnp.int32, (tm, tm), 0)
    c = lax.broadcasted_iota(jnp.int32, (tm, tm), 1)
    tri = jnp.where(c < r, 1.0, 0.0).astype(BF16)
    before = _dot(tri, onehot.astype(BF16)) + carry_ref[...]
    rank1 = rsum(jnp.where(sel1, before, 0.0))
    rank2 = rsum(jnp.where(sel2, before, 0.0))
    carry_ref[...] = carry_ref[...] + jnp.sum(onehot, axis=0, keepdims=True)
    cnt_ref[...] = carry_ref[...]

    info = jnp.zeros_like(lg)
    for k, val in enumerate((i1, i2, rank1, rank2, gate1, gate2)):
        info = jnp.where(lane == k, val, info)
    info_ref[...] = info
    infot_ref[...] = info.T[0:8, :]


def _route(logits, tm):
    n = logits.shape[0]
    kernel = functools.partial(_route_kernel, tm=tm)
    return pl.pallas_call(
        kernel,
        grid=(n // tm,),
        in_specs=[pl.BlockSpec((tm, LANES), lambda i: (i, 0))],
        out_specs=(pl.BlockSpec((tm, LANES), lambda i: (i, 0)),
                   pl.BlockSpec((8, tm), lambda i: (0, i)),
                   pl.BlockSpec((1, LANES), lambda i: (0, 0))),
        out_shape=(jax.ShapeDtypeStruct((n, LANES), F32), jax.ShapeDtypeStruct((8, n), F32),
                   jax.ShapeDtypeStruct((1, LANES), F32)),
        scratch_shapes=[pltpu.VMEM((1, LANES), F32)],
        compiler_params=_params(("arbitrary",)),
        name="route",
    )(logits)


DISPATCH_CHUNK = 128


def _dispatch_kernel(dest_ref, zflag_ref, xa_ref, xb_ref, xs_hbm, zero_ref, sem, zsem):
    n_all = dest_ref.shape[0] // 2

    zero_ref[...] = jnp.zeros_like(zero_ref)

    def zero_copy(blk):
        return pltpu.make_async_copy(zero_ref, xs_hbm.at[pl.ds(blk * MOE_BLOCK, MOE_BLOCK)], zsem)

    def start_zero(blk, carry):
        @pl.when(zflag_ref[blk] != 0)
        def _():
            zero_copy(blk).start()
        return carry

    def wait_zero(blk, carry):
        @pl.when(zflag_ref[blk] != 0)
        def _():
            zero_copy(blk).wait()
        return carry

    lax.fori_loop(0, zflag_ref.shape[0], start_zero, 0)
    lax.fori_loop(0, zflag_ref.shape[0], wait_zero, 0)

    def scatter(src_ref, t_off):
        def row_copy(t, k):
            return pltpu.make_async_copy(src_ref.at[pl.ds(t, 1)],
                                         xs_hbm.at[pl.ds(dest_ref[k * n_all + t_off + t], 1)], sem)

        def issue(c):
            def body(r, carry):
                row_copy(c * DISPATCH_CHUNK + r, 0).start(priority=0)
                row_copy(c * DISPATCH_CHUNK + r, 1).start(priority=1)
                return carry
            lax.fori_loop(0, DISPATCH_CHUNK, body, 0, unroll=8)

        def drain(c):
            def body(r, carry):
                row_copy(c * DISPATCH_CHUNK + r, 0).wait()
                row_copy(c * DISPATCH_CHUNK + r, 1).wait()
                return carry
            lax.fori_loop(0, DISPATCH_CHUNK, body, 0, unroll=8)

        n_chunks = src_ref.shape[0] // DISPATCH_CHUNK

        def chunk(c, carry):
            issue(c)

            @pl.when(c > 0)
            def _():
                drain(c - 1)
            return carry

        lax.fori_loop(0, n_chunks, chunk, 0)
        drain(n_chunks - 1)

    scatter(xa_ref, 0)
    scatter(xb_ref, xa_ref.shape[0])


def _dispatch(dest_flat, zero_flag, xa, xb):
    w = xa.shape[1]
    n_slots = zero_flag.shape[0] * MOE_BLOCK
    vmem = pl.BlockSpec(memory_space=pltpu.VMEM)
    grid_spec = pltpu.PrefetchScalarGridSpec(
        num_scalar_prefetch=2,
        grid=(1,),
        in_specs=[vmem, vmem],
        out_specs=pl.BlockSpec(memory_space=pl.ANY),
        scratch_shapes=[pltpu.VMEM((MOE_BLOCK, w), xa.dtype), pltpu.SemaphoreType.DMA(()),
                        pltpu.SemaphoreType.DMA(())],
    )
    return pl.pallas_call(
        _dispatch_kernel,
        grid_spec=grid_spec,
        out_shape=jax.ShapeDtypeStruct((n_slots, w), xa.dtype),
        compiler_params=pltpu.CompilerParams(dimension_semantics=("arbitrary",),
                                             vmem_limit_bytes=VMEM_LIMIT, has_side_effects=True),
        name="dispatch",
    )(dest_flat, zero_flag, xa, xb)


CAST_CHUNK_ELEMS = 64 * 1024
WEIGHT_DMA_SPLIT = 4
WEIGHT_DMA_PRIORITY = 1


def _expert_kernel(be_ref, nu_ref, nx_ref, xs_ref, w1_hbm, w3_hbm, w2_hbm, y_ref,
                   w1f_ref, w3f_ref, w2f_ref, w1b_ref, w3b_ref, w2b_ref, sem):
    b = pl.program_id(0)
    active = b < nu_ref[0]
    new_expert = jnp.logical_or(b == 0, be_ref[b] != be_ref[jnp.maximum(b - 1, 0)])

    def fetch(e):
        copies = []
        for k, (src, dst) in enumerate(((w1_hbm, w1f_ref), (w3_hbm, w3f_ref), (w2_hbm, w2f_ref))):
            slab = dst.shape[0] // WEIGHT_DMA_SPLIT
            for c in range(WEIGHT_DMA_SPLIT):
                rows = pl.ds(c * slab, slab)
                copies.append(pltpu.make_async_copy(src.at[e, rows], dst.at[rows], sem.at[k]))
        return copies

    @pl.when(b == 0)
    def _():
        for copy in fetch(be_ref[0]):
            copy.start(priority=WEIGHT_DMA_PRIORITY)

    @pl.when(jnp.logical_and(active, new_expert))
    def _():
        for copy in fetch(be_ref[b]):
            copy.wait()
        for src, dst in ((w1f_ref, w1b_ref), (w3f_ref, w3b_ref), (w2f_ref, w2b_ref)):
            rows, cols = src.shape
            chunk = CAST_CHUNK_ELEMS // cols

            def cast_rows(c, carry, src=src, dst=dst, chunk=chunk):
                r0 = pl.multiple_of(c * chunk, chunk)
                dst[pl.ds(r0, chunk), :] = src[pl.ds(r0, chunk), :].astype(BF16)
                return carry

            lax.fori_loop(0, rows // chunk, cast_rows, 0)

        @pl.when(nx_ref[b] >= 0)
        def _():
            for copy in fetch(nx_ref[b]):
                copy.start(priority=WEIGHT_DMA_PRIORITY)

    @pl.when(active)
    def _():
        lo, hi = _unpack_bf16_pair(xs_ref[...])
        half = D_MODEL // 2
        h1 = _dot(lo, w1b_ref[:half, :]) + _dot(hi, w1b_ref[half:, :])
        h3 = _dot(lo, w3b_ref[:half, :]) + _dot(hi, w3b_ref[half:, :])
        h = (h1 * jax.nn.sigmoid(h1) * h3).astype(BF16)
        y = _dot(h, w2b_ref[...]).astype(BF16).astype(F32)
        y_ref[...] = _pack_bf16_pair(y[:, :half], y[:, half:])

    @pl.when(jnp.logical_not(active))
    def _():
        y_ref[...] = jnp.zeros_like(y_ref)


def _experts(block_e, n_used, next_e, xs, w1, w3, w2):
    n_slots = xs.shape[0]
    nb = n_slots // MOE_BLOCK
    blk = lambda b, be, nu, nx: (jnp.minimum(b, nu[0] - 1), 0)
    hbm = pl.BlockSpec(memory_space=pl.ANY)
    grid_spec = pltpu.PrefetchScalarGridSpec(
        num_scalar_prefetch=3,
        grid=(nb,),
        in_specs=[pl.BlockSpec((MOE_BLOCK, D_MODEL // 2), blk), hbm, hbm, hbm],
        out_specs=pl.BlockSpec((MOE_BLOCK, D_MODEL // 2), lambda b, be, nu, nx: (b, 0)),
        scratch_shapes=[pltpu.VMEM((D_MODEL, D_EXPERT), F32), pltpu.VMEM((D_MODEL, D_EXPERT), F32),
                        pltpu.VMEM((D_EXPERT, D_MODEL), F32),
                        pltpu.VMEM((D_MODEL, D_EXPERT), BF16), pltpu.VMEM((D_MODEL, D_EXPERT), BF16),
                        pltpu.VMEM((D_EXPERT, D_MODEL), BF16),
                        pltpu.SemaphoreType.DMA((3,))],
    )
    return pl.pallas_call(
        _expert_kernel,
        grid_spec=grid_spec,
        out_shape=jax.ShapeDtypeStruct((n_slots, D_MODEL // 2), jnp.uint32),
        compiler_params=_params(("arbitrary",)),
        name="experts",
    )(block_e, n_used, next_e, xs, w1, w3, w2)


PLE_SPLIT = 4
GATHER_ISSUE_SLABS = 2


def _combine_ple_kernel(dest_ref, h_ref, info_ref, p_ref, g_ref, *rest, tm, t_off):
    wpg_refs = rest[:PLE_SPLIT]
    wpp_ref, yb_hbm, o_ref, buf_ref, sem = rest[PLE_SPLIT:]
    i = pl.program_id(0)
    last = pl.num_programs(0) - 1
    n_all = dest_ref.shape[0] // 2
    slot = i % 2

    def row_copy(step, into, r, k):
        t = t_off + step * tm + r
        return pltpu.make_async_copy(yb_hbm.at[pl.ds(dest_ref[k * n_all + t], 1)],
                                     buf_ref.at[into, k, pl.ds(r, 1)], sem.at[into])

    def drain(into):
        def body(r, carry):
            for k in range(2):
                pltpu.make_async_copy(yb_hbm.at[pl.ds(0, 1)], buf_ref.at[into, k, pl.ds(r, 1)],
                                      sem.at[into]).wait()
            return carry
        lax.fori_loop(0, tm, body, 0, unroll=8)

    @pl.when(i == 0)
    def _():
        def body(r, carry):
            row_copy(0, 0, r, 0).start()
            row_copy(0, 0, r, 1).start()
            return carry
        lax.fori_loop(0, tm, body, 0, unroll=8)

    drain(slot)
    lo0, hi0 = _unpack_f32_pair(buf_ref[slot, 0])
    lo1, hi1 = _unpack_f32_pair(buf_ref[slot, 1])
    g0, g1 = info_ref[:, 4:5], info_ref[:, 5:6]
    h2 = h_ref[...] + jnp.concatenate([g0 * lo0 + g1 * lo1, g0 * hi0 + g1 * hi1], axis=1)
    hn = (h2 * lax.rsqrt(jnp.mean(h2 * h2, axis=-1, keepdims=True) + EPS) * g_ref[...]).astype(BF16)
    pb = p_ref[...].astype(BF16)

    nxt = jnp.minimum(i + 1, last)
    ch = D_MODEL // PLE_SPLIT
    per = tm // GATHER_ISSUE_SLABS
    for k in range(PLE_SPLIT):
        for r in range(k * per, (k + 1) * per if k < GATHER_ISSUE_SLABS else k * per):
            row_copy(nxt, 1 - slot, r, 0).start(priority=0)
            row_copy(nxt, 1 - slot, r, 1).start(priority=1)
        cs = slice(k * ch, (k + 1) * ch)
        gate = jax.nn.sigmoid(_dot(hn, wpg_refs[k][...]))
        o_ref[:, cs] = h2[:, cs] + gate * _dot(pb, wpp_ref[:, cs])

    @pl.when(i == last)
    def _():
        drain(1 - slot)


def _combine_ple(dest_flat, h, info, yb, p, g_ple, w_pg, w_pp, t_off, tm):
    n = h.shape[0]
    kernel = functools.partial(_combine_ple_kernel, tm=tm, t_off=t_off)
    ob = t_off // tm
    row = lambda w: pl.BlockSpec((tm, w), lambda i, d: (i, 0))
    grid_spec = pltpu.PrefetchScalarGridSpec(
        num_scalar_prefetch=1,
        grid=(n // tm,),
        in_specs=[row(D_MODEL), pl.BlockSpec((tm, LANES), lambda i, d: (i + ob, 0)), row(PLE_DIM),
                  _resident((1, D_MODEL))]
        + [_resident((D_MODEL, D_MODEL // PLE_SPLIT), (0, k)) for k in range(PLE_SPLIT)]
        + [_resident((PLE_DIM, D_MODEL)), pl.BlockSpec(memory_space=pl.ANY)],
        out_specs=row(D_MODEL),
        scratch_shapes=[pltpu.VMEM((2, 2, tm, D_MODEL // 2), jnp.uint32), pltpu.SemaphoreType.DMA((2,))],
    )
    return pl.pallas_call(
        kernel,
        grid_spec=grid_spec,
        out_shape=jax.ShapeDtypeStruct((n, D_MODEL), F32),
        compiler_params=_params(("arbitrary",)),
        name="combine_ple",
    )(dest_flat, h, info, p, g_ple, *([w_pg] * PLE_SPLIT), w_pp, yb)


def _mixer_tokens(x2d, wts, tm):
    return _in_proj(x2d, wts["g_mix"], wts["w_parts"], wts["w_f"], wts["b_f"],
                    wts["q_gain"], wts["k_gain"], tm)


def kernel(x_prompt, x_sample, cache_k, cache_v, cache_logf, cache_conv, p_prompt, p_sample,
           g_mix, w_in, b_f, q_gain, k_gain, w_dw, b_dw, ln_g, ln_b, gc, ga, w_out,
           g_ffn, w_router_g, b_router_g, w_router_e, b_router_e, w1, w3, w2,
           g_ple, w_pg, w_pp):
    batch, seq, _ = x_prompt.shape
    dec_batch, t_new, _ = x_sample.shape
    past = cache_k.shape[2]
    n_p = batch * seq
    n_s = dec_batch * t_new
    n_all = n_p + n_s
    tm = 256
    li = 0

    w_in_l = w_in[li]
    pad_lanes = lambda a: jnp.pad(a, ((0, 0), (0, LANES - a.shape[1])))
    row2d = lambda a: a.reshape(1, -1)
    wts = {
        "g_mix": row2d(g_mix[li]),
        "w_parts": [w_in_l[:, c:c + CONV_WIDTH].astype(BF16) for c in range(0, MAIN_COLS, CONV_WIDTH)],
        "w_f": pad_lanes(w_in_l[:, MAIN_COLS:]).astype(BF16),
        "b_f": pad_lanes(row2d(b_f[li])),
        "q_gain": row2d(q_gain[li]),
        "k_gain": row2d(k_gain[li]),
    }
    w_dw_p = jnp.pad(w_dw[li], ((0, CONV_HALO - CONV_KERNEL), (0, 0)))
    w_out_b = w_out[li].astype(BF16)
    w_r = pad_lanes(jnp.concatenate([w_router_e[li], w_router_g[li]], axis=1))
    wr_hi = w_r.astype(BF16)
    wr_lo = (w_r - wr_hi.astype(F32)).astype(BF16)
    b_r = pad_lanes(row2d(jnp.concatenate([b_router_e[li], b_router_g[li]])))
    w_pg_b, w_pp_b = w_pg[li].astype(BF16), w_pp[li].astype(BF16)

    xp = x_prompt.reshape(n_p, D_MODEL)
    u_p, q_p, k_p, kb_p, v_p, _, vbt_p, lf_p = _mixer_tokens(xp, wts, 2 * tm)
    lf_p_row = lf_p.reshape(batch, seq, N_HEADS).transpose(0, 2, 1)
    f_p_row = _cumsum_lanes(lf_p_row.reshape(batch * N_HEADS, seq))
    ctx_p = jnp.zeros((batch, CONV_HALO, CONV_WIDTH), F32)
    conv_args = (w_dw_p, row2d(b_dw[li]), row2d(ln_g[li]), row2d(ln_b[li]), row2d(gc[li]))
    yc_p = _conv_module(u_p, ctx_p, *conv_args, batch, seq, 2 * tm)
    qa_p, ka_p = _attn_aug(f_p_row, batch, seq, 2 * tm)
    ya_p = _attn_prompt(q_p, qa_p, kb_p, ka_p, vbt_p, batch, seq, 256, 256)
    merge_args = (row2d(ga[li]), w_out_b, row2d(g_ffn[li]), jnp.concatenate([wr_hi, wr_lo], axis=1), b_r)
    h_p, xpk_p, lg_p = _merge(yc_p, ya_p, xp, *merge_args, 2 * tm)

    xs_ = x_sample.reshape(n_s, D_MODEL)
    u_s, q_s, k_s, kb_s, v_s, vb_s, _, lf_s = _mixer_tokens(xs_, wts, tm)
    clf_row = cache_logf[li].transpose(0, 2, 1).reshape(dec_batch * N_HEADS, past)
    fc_row = _cumsum_lanes(clf_row).reshape(dec_batch, N_HEADS, past)
    lf_s_row = lf_s.reshape(dec_batch, t_new, N_HEADS).transpose(0, 2, 1).reshape(dec_batch * N_HEADS, t_new)
    fn_row = _cumsum_lanes(jnp.pad(lf_s_row, ((0, 0), (0, LANES - t_new)))).reshape(dec_batch, N_HEADS, LANES)
    fn_col = fn_row[:, :, :t_new].transpose(0, 2, 1).reshape(n_s, N_HEADS)
    ctx_s = jnp.pad(cache_conv[li], ((0, 0), (CONV_HALO - CONV_STATE, 0), (0, 0)))
    yc_s = _conv_module(u_s, ctx_s, *conv_args, dec_batch, t_new, t_new)
    ya_s = _attn_sample(q_s, kb_s, vb_s,
                        cache_k[li], cache_v[li],
                        fc_row, fn_row, fn_col, dec_batch, t_new, past)
    h_s, xpk_s, lg_s = _merge(yc_s, ya_s, xs_, *merge_args, tm)

    info, info_t, counts = _route(jnp.concatenate([lg_p, lg_s], axis=0), tm)
    counts = counts[0, :N_EXPERTS].astype(jnp.int32)
    bcounts = (counts + MOE_BLOCK - 1) // MOE_BLOCK
    bends = jnp.cumsum(bcounts)
    pstarts = (bends - bcounts) * MOE_BLOCK
    n_rows = n_all * 2
    nb = -(-(n_rows + N_EXPERTS * (MOE_BLOCK - 1)) // MOE_BLOCK)
    e_idx = info_t[0:2].astype(jnp.int32)
    expert_ids = jnp.arange(N_EXPERTS, dtype=jnp.int32)[:, None, None]
    seg_start = jnp.sum(jnp.where(e_idx[None] == expert_ids, pstarts[:, None, None], 0), axis=0)
    dest = (seg_start + info_t[2:4].astype(jnp.int32)).reshape(n_rows)
    block_e = jnp.minimum(jnp.sum(bends[None, :] <= jnp.arange(nb, dtype=jnp.int32)[:, None], axis=1),
                          N_EXPERTS - 1).astype(jnp.int32)
    n_used = bends[N_EXPERTS - 1:].astype(jnp.int32)
    seg_end = jnp.sum(jnp.where(block_e[None, :] == expert_ids[:, :, 0], bends[:, None], 0), axis=0)
    block_ids = jnp.arange(nb, dtype=jnp.int32)
    zero_flag = ((block_ids + 1 == seg_end) | (block_ids >= n_used[0])).astype(jnp.int32)
    xs_sorted = _dispatch(dest, zero_flag, xpk_p, xpk_s)
    next_e = jnp.where(seg_end < n_used[0], block_e[jnp.minimum(seg_end, nb - 1)], -1).astype(jnp.int32)
    yb = _experts(block_e, n_used, next_e, xs_sorted, w1[li], w3[li], w2[li])

    ple_args = (row2d(g_ple[li]), w_pg_b, w_pp_b)
    y_p = _combine_ple(dest, h_p, info, yb, p_prompt[li].reshape(n_p, PLE_DIM), *ple_args, 0, tm)
    y_s = _combine_ple(dest, h_s, info, yb, p_sample[li].reshape(n_s, PLE_DIM), *ple_args, n_p, tm)

    heads = lambda a, b, t: a.reshape(1, b, t, N_HEADS, HEAD_DIM)
    return (
        y_p.reshape(batch, seq, D_MODEL),
        y_s.reshape(dec_batch, t_new, D_MODEL),
        heads(k_p, batch, seq), heads(v_p, batch, seq),
        lf_p.reshape(1, batch, seq, N_HEADS),
        u_p.reshape(batch, seq, CONV_WIDTH)[None, :, seq - CONV_STATE:, :],
        heads(k_s, dec_batch, t_new), heads(v_s, dec_batch, t_new),
        lf_s.reshape(1, dec_batch, t_new, N_HEADS),
        u_s.reshape(dec_batch, t_new, CONV_WIDTH)[None, :, t_new - CONV_STATE:, :],
    )
```

```python
import functools

import jax
import jax.numpy as jnp
from jax import lax
from jax.experimental import pallas as pl
from jax.experimental.pallas import tpu as pltpu

D_MODEL = 2048
CONV_WIDTH = 1024
ATTN_WIDTH = 1024
HEAD_DIM = 128
N_HEADS = 8
CONV_KERNEL = 31
CONV_STATE = CONV_KERNEL - 1
N_GROUPS = 4
EXPERTS_PER_GROUP = 8
N_EXPERTS = 32
D_EXPERT = 512
PLE_DIM = 256
MOE_BLOCK = 256
EPS = 1e-6
MAIN_COLS = 2 * CONV_WIDTH + 3 * ATTN_WIDTH

LANES = 128
CONV_HALO = 32
VMEM_LIMIT = 56 * 1024 * 1024

F32 = jnp.float32
BF16 = jnp.bfloat16
NEG_BIG = -1e30
LOG2E = 1.4426950408889634


def _dot(a, b):
    return jnp.dot(a, b, preferred_element_type=F32)


def _params(sem):
    return pltpu.CompilerParams(dimension_semantics=sem, vmem_limit_bytes=VMEM_LIMIT)


def _resident(shape, index=None):
    index = (0,) * len(shape) if index is None else index
    return pl.BlockSpec(shape, lambda *_: index, pipeline_mode=pl.Buffered(1))


def _inproj_kernel(x_ref, g_ref, wval_ref, wgate_ref, wq_ref, wk_ref, wv_ref, wf_ref, bf_ref, qg_ref, kg_ref,
                   u_ref, q_ref, k_ref, kb_ref, v_ref, vb_ref, vbt_ref, lf_ref):
    x = x_ref[...]
    ms = jnp.mean(x * x, axis=-1, keepdims=True)
    a = (x * lax.rsqrt(ms + EPS) * g_ref[...]).astype(BF16)

    ch = 256
    for c in range(0, CONV_WIDTH, ch):
        val = _dot(a, wval_ref[:, c:c + ch])
        gate = _dot(a, wgate_ref[:, c:c + ch])
        u_ref[:, c:c + ch] = val * jax.nn.sigmoid(gate)

    def head_norm(z, gain):
        return z * lax.rsqrt(jnp.mean(z * z, axis=-1, keepdims=True) + EPS) * gain

    scale = LOG2E * HEAD_DIM ** -0.5
    for c in range(0, ATTN_WIDTH, ch):
        zq = _dot(a, wq_ref[:, c:c + ch])
        zk = _dot(a, wk_ref[:, c:c + ch])
        zv = _dot(a, wv_ref[:, c:c + ch])
        for s in range(0, ch, HEAD_DIM):
            qn = head_norm(zq[:, s:s + HEAD_DIM], qg_ref[...])
            kn = head_norm(zk[:, s:s + HEAD_DIM], kg_ref[...])
            q_ref[:, c + s:c + s + HEAD_DIM] = (qn * scale).astype(BF16)
            k_ref[:, c + s:c + s + HEAD_DIM] = kn
            kb_ref[:, c + s:c + s + HEAD_DIM] = kn.astype(BF16)
        v_ref[:, c:c + ch] = zv
        vb_ref[:, c:c + ch] = zv.astype(BF16)
        vbt_ref[c:c + ch, :] = zv.T.astype(BF16)

    f = _dot(a, wf_ref[...]) + bf_ref[...]
    lf = jnp.minimum(f, 0.0) - jnp.log1p(jnp.exp(-jnp.abs(f)))
    lf_ref[...] = lf[:, :N_HEADS]


def _in_proj(x, g_mix, w_parts, w_f, b_f, q_gain, k_gain, tm):
    n = x.shape[0]
    row = lambda w: pl.BlockSpec((tm, w), lambda i: (i, 0))
    out_shape = (
        jax.ShapeDtypeStruct((n, CONV_WIDTH), F32),
        jax.ShapeDtypeStruct((n, ATTN_WIDTH), BF16),
        jax.ShapeDtypeStruct((n, ATTN_WIDTH), F32),
        jax.ShapeDtypeStruct((n, ATTN_WIDTH), BF16),
        jax.ShapeDtypeStruct((n, ATTN_WIDTH), F32),
        jax.ShapeDtypeStruct((n, ATTN_WIDTH), BF16),
        jax.ShapeDtypeStruct((ATTN_WIDTH, n), BF16),
        jax.ShapeDtypeStruct((n, N_HEADS), F32),
    )
    return pl.pallas_call(
        _inproj_kernel,
        grid=(n // tm,),
        in_specs=[row(D_MODEL), _resident((1, D_MODEL))] + [_resident((D_MODEL, CONV_WIDTH))] * 5 + [
                  _resident((D_MODEL, LANES)), _resident((1, LANES)),
                  _resident((1, HEAD_DIM)), _resident((1, HEAD_DIM))],
        out_specs=(row(CONV_WIDTH), row(ATTN_WIDTH), row(ATTN_WIDTH), row(ATTN_WIDTH),
                   row(ATTN_WIDTH), row(ATTN_WIDTH),
                   pl.BlockSpec((ATTN_WIDTH, tm), lambda i: (0, i)), row(N_HEADS)),
        out_shape=out_shape,
        compiler_params=_params(("parallel",)),
        name="in_proj",
    )(x, g_mix, *w_parts, w_f, b_f, q_gain, k_gain)


def _cumsum_kernel(x_ref, o_ref):
    x = x_ref[...]
    width = x.shape[1]
    lane = lax.broadcasted_iota(jnp.int32, x.shape, 1)
    s = 1
    while s < width:
        x = x + jnp.where(lane >= s, pltpu.roll(x, s, axis=1), 0.0)
        s *= 2
    o_ref[...] = x


def _cumsum_lanes(x):
    return pl.pallas_call(
        _cumsum_kernel,
        out_shape=jax.ShapeDtypeStruct(x.shape, F32),
        name="cumsum",
    )(x)


def _conv_kernel(u_ref, halo_ref, ctx_ref, w_ref, bdw_ref, lng_ref, lnb_ref, gc_ref,
                 o_ref, ext_ref, y_ref, *, tm):
    i = pl.program_id(1)

    @pl.when(i == 0)
    def _():
        ext_ref[0:CONV_HALO, :] = ctx_ref[0]

    @pl.when(i > 0)
    def _():
        ext_ref[0:CONV_HALO, :] = halo_ref[...]

    ext_ref[CONV_HALO:CONV_HALO + tm, :] = u_ref[...]

    rows = min(64, tm)
    ch = LANES
    sub = 8
    wlen = rows + CONV_HALO
    first = CONV_HALO - CONV_STATE

    def conv_rows(r, carry):
        r0 = pl.multiple_of(r * rows, rows)
        for c in range(0, CONV_WIDTH, ch):
            acc = jnp.zeros((rows, ch), F32)
            win = ext_ref[pl.ds(r0, wlen), c:c + ch]
            for rho in range(sub):
                sh = win if rho == 0 else pltpu.roll(win, wlen - rho, axis=0)
                for a in range(wlen // sub):
                    j = sub * a + rho - first
                    if 0 <= j < CONV_KERNEL:
                        acc = acc + sh[sub * a:sub * a + rows] * w_ref[j:j + 1, c:c + ch]
            y_ref[pl.ds(r0, rows), c:c + ch] = acc
        return carry

    lax.fori_loop(0, tm // rows, conv_rows, 0)

    nrows = min(128, tm)

    def norm_rows(r, carry):
        r0 = pl.multiple_of(r * nrows, nrows)
        y = y_ref[pl.ds(r0, nrows), :] + bdw_ref[...]
        mu = jnp.mean(y, axis=-1, keepdims=True)
        yc = y - mu
        var = jnp.mean(yc * yc, axis=-1, keepdims=True)
        z = yc * lax.rsqrt(var + EPS) * lng_ref[...] + lnb_ref[...]
        s = z * jax.nn.sigmoid(z)
        ms = jnp.mean(s * s, axis=-1, keepdims=True)
        o_ref[pl.ds(r0, nrows), :] = (s * lax.rsqrt(ms + EPS) * gc_ref[...]).astype(BF16)
        return carry

    lax.fori_loop(0, tm // nrows, norm_rows, 0)


def _conv_module(u, ctx, w_dw, b_dw, ln_g, ln_b, gc, batch, seq, tm):
    nt = seq // tm
    hb = tm // CONV_HALO
    kernel = functools.partial(_conv_kernel, tm=tm)
    return pl.pallas_call(
        kernel,
        grid=(batch, nt),
        in_specs=[
            pl.BlockSpec((tm, CONV_WIDTH), lambda b, i: (b * nt + i, 0)),
            pl.BlockSpec((CONV_HALO, CONV_WIDTH),
                         lambda b, i: (jnp.maximum((b * nt + i) * hb - 1, 0), 0)),
            pl.BlockSpec((1, CONV_HALO, CONV_WIDTH), lambda b, i: (b, 0, 0)),
            _resident((CONV_HALO, CONV_WIDTH)),
            _resident((1, CONV_WIDTH)), _resident((1, CONV_WIDTH)),
            _resident((1, CONV_WIDTH)), _resident((1, CONV_WIDTH)),
        ],
        out_specs=pl.BlockSpec((tm, CONV_WIDTH), lambda b, i: (b * nt + i, 0)),
        out_shape=jax.ShapeDtypeStruct((batch * seq, CONV_WIDTH), BF16),
        scratch_shapes=[pltpu.VMEM((CONV_HALO + tm, CONV_WIDTH), F32),
                        pltpu.VMEM((tm, CONV_WIDTH), F32)],
        compiler_params=_params(("parallel", "arbitrary")),
        name="conv_module",
    )(u, u, ctx, w_dw, b_dw, ln_g, ln_b, gc)


def _qk(q, k):
    return lax.dot_general(q, k, (((1,), (1,)), ((), ())), preferred_element_type=F32)


AUG_TERMS = 3
AUG_STRIDE = LANES // N_HEADS
QK_AHEAD = 4


def _aug_kernel(f_ref, qa_ref, ka_ref):
    f = f_ref[...] * LOG2E
    tm = f.shape[1]
    row = lax.broadcasted_iota(jnp.int32, (LANES, tm), 0)
    sub = row % AUG_STRIDE
    qa = jnp.where((sub >= AUG_TERMS) & (sub < 2 * AUG_TERMS), 1.0, 0.0)
    ka = jnp.where(sub < AUG_TERMS, 1.0, 0.0)
    for h in range(N_HEADS):
        rest = f[h:h + 1, :]
        for t in range(AUG_TERMS):
            piece = rest.astype(BF16).astype(F32)
            rest = rest - piece
            qa = jnp.where(row == h * AUG_STRIDE + t, piece, qa)
            ka = jnp.where(row == h * AUG_STRIDE + AUG_TERMS + t, -piece, ka)
    qa_ref[...] = qa.T.astype(BF16)
    ka_ref[...] = ka.T.astype(BF16)


def _attn_aug(f_row, batch, seq, tm):
    nt = seq // tm
    out = pl.BlockSpec((tm, LANES), lambda b, i: (b * nt + i, 0))
    return pl.pallas_call(
        _aug_kernel,
        grid=(batch, nt),
        in_specs=[pl.BlockSpec((N_HEADS, tm), lambda b, i: (b, i))],
        out_specs=(out, out),
        out_shape=(jax.ShapeDtypeStruct((batch * seq, LANES), BF16),
                   jax.ShapeDtypeStruct((batch * seq, LANES), BF16)),
        compiler_params=_params(("parallel", "parallel")),
        name="attn_aug",
    )(f_row)


def _attn_prompt_kernel(q_ref, qa_ref, k_ref, ka_ref, vt_ref, o_ref, m_ref, l_ref, acc_ref, qt_ref,
                        sp_ref, *, tq, tk):
    i = pl.program_id(1)
    key = lax.broadcasted_iota(jnp.int32, (tk, tq), 0)
    qry = lax.broadcasted_iota(jnp.int32, (tk, tq), 1)

    m_ref[...] = jnp.full(m_ref.shape, NEG_BIG, F32)
    l_ref[...] = jnp.zeros(l_ref.shape, F32)
    acc_ref[...] = jnp.zeros(acc_ref.shape, F32)
    aug_lane = lax.broadcasted_iota(jnp.int32, (tq, LANES), 1)
    for h in range(N_HEADS):
        hs = slice(h * HEAD_DIM, (h + 1) * HEAD_DIM)
        own = (aug_lane >= h * AUG_STRIDE) & (aug_lane < (h + 1) * AUG_STRIDE)
        qa = jnp.where(own, qa_ref[...], jnp.zeros_like(qa_ref))
        qt_ref[h] = jnp.concatenate([q_ref[:, hs], qa], axis=1).T

    def scores(ks, h):
        hs = slice(h * HEAD_DIM, (h + 1) * HEAD_DIM)
        kf = jnp.concatenate([k_ref[pl.ds(ks, tk), hs], ka_ref[pl.ds(ks, tk), :]], axis=1)
        return _dot(kf, qt_ref[h])

    def tile_step(ks, ks_next, masked):
        pending = [sp_ref[a] for a in range(QK_AHEAD)]
        for h in range(N_HEADS):
            hs = slice(h * HEAD_DIM, (h + 1) * HEAD_DIM)
            s = pending.pop(0)
            if h + QK_AHEAD < N_HEADS:
                pending.append(scores(ks, h + QK_AHEAD))
            elif ks_next is not None:
                pending.append(scores(ks_next, h + QK_AHEAD - N_HEADS))
            if masked:
                s = jnp.where(key <= qry, s, -jnp.inf)
            m = m_ref[h]
            m_new = jnp.maximum(m, jnp.max(s, axis=0, keepdims=True))
            alpha = jnp.exp2(m - m_new)
            p = jnp.exp2(s - m_new)
            m_ref[h] = m_new
            l_ref[h] = alpha * l_ref[h] + jnp.sum(p, axis=0, keepdims=True)
            pv = _dot(vt_ref[hs, pl.ds(ks, tk)], p.astype(BF16))
            acc_ref[h] = alpha * acc_ref[h] + pv
        for a, s in enumerate(pending):
            sp_ref[a] = s

    def body(j, carry):
        tile_step(pl.multiple_of(j * tk, tk), pl.multiple_of((j + 1) * tk, tk), False)
        return carry

    for a in range(QK_AHEAD):
        sp_ref[a] = scores(0, a)
    lax.fori_loop(0, i, body, 0)
    tile_step(pl.multiple_of(i * tk, tk), None, True)
    for h in range(N_HEADS):
        o_ref[:, h * HEAD_DIM:(h + 1) * HEAD_DIM] = (acc_ref[h] / l_ref[h]).T


def _attn_prompt(q, qa, kb, ka, vbt, batch, seq, tq, tk):
    nq = seq // tq
    kernel = functools.partial(_attn_prompt_kernel, tq=tq, tk=tk)
    qblk = pl.BlockSpec((tq, ATTN_WIDTH), lambda b, i: (b * nq + i, 0))
    kblk = pl.BlockSpec((seq, ATTN_WIDTH), lambda b, i: (b, 0))
    vblk = pl.BlockSpec((ATTN_WIDTH, seq), lambda b, i: (0, b))
    return pl.pallas_call(
        kernel,
        grid=(batch, nq),
        in_specs=[qblk, pl.BlockSpec((tq, LANES), lambda b, i: (b * nq + i, 0)), kblk,
                  pl.BlockSpec((seq, LANES), lambda b, i: (b, 0)), vblk],
        out_specs=pl.BlockSpec((tq, ATTN_WIDTH), lambda b, i: (b * nq + i, 0)),
        out_shape=jax.ShapeDtypeStruct((batch * seq, ATTN_WIDTH), F32),
        scratch_shapes=[pltpu.VMEM((N_HEADS, 1, tq), F32), pltpu.VMEM((N_HEADS, 1, tq), F32),
                        pltpu.VMEM((N_HEADS, HEAD_DIM, tq), F32),
                        pltpu.VMEM((N_HEADS, 2 * HEAD_DIM, tq), BF16),
                        pltpu.VMEM((QK_AHEAD, tk, tq), F32)],
        compiler_params=_params(("parallel", "arbitrary")),
        name="attn_prompt",
    )(q, qa, kb, ka, vbt)


def _attn_sample_kernel(q_ref, kn_ref, vn_ref, ck_hbm, cv_hbm, fc_ref, fnrow_ref, fncol_ref,
                        o_ref, kbuf_ref, vbuf_ref, sem, *, t_new, past):
    b = pl.program_id(0)
    slot = b % 2

    def fetch(batch_idx, into):
        copies = []
        for h in range(N_HEADS):
            copies.append(pltpu.make_async_copy(ck_hbm.at[batch_idx, :, h, :], kbuf_ref.at[into, h],
                                                sem.at[into, 0]))
            copies.append(pltpu.make_async_copy(cv_hbm.at[batch_idx, :, h, :], vbuf_ref.at[into, h],
                                                sem.at[into, 1]))
        return copies

    @pl.when(b == 0)
    def _():
        for copy in fetch(0, 0):
            copy.start()

    @pl.when(b + 1 < pl.num_programs(0))
    def _():
        for copy in fetch(b + 1, 1 - slot):
            copy.start()

    for copy in fetch(b, slot):
        copy.wait()

    row = lax.broadcasted_iota(jnp.int32, (t_new, t_new), 0)
    col = lax.broadcasted_iota(jnp.int32, (t_new, t_new), 1)
    causal = col <= row
    for h in range(N_HEADS):
        hs = slice(h * HEAD_DIM, (h + 1) * HEAD_DIM)
        q = q_ref[:, hs]
        fn_q = fncol_ref[:, h:h + 1]
        fc = fc_ref[0, h:h + 1, :]
        fc_last = fc[:, past - 1:past]
        s_c = _qk(q, kbuf_ref[slot, h].astype(BF16)) + ((fc_last + fn_q) - fc) * LOG2E
        fn_k = fnrow_ref[0, h:h + 1, 0:t_new]
        s_n = _qk(q, kn_ref[:, hs]) + (fn_q - fn_k) * LOG2E
        s_n = jnp.where(causal, s_n, -jnp.inf)
        m = jnp.maximum(jnp.max(s_c, axis=-1, keepdims=True), jnp.max(s_n, axis=-1, keepdims=True))
        p_c = jnp.exp2(s_c - m)
        p_n = jnp.exp2(s_n - m)
        l = jnp.sum(p_c, axis=-1, keepdims=True) + jnp.sum(p_n, axis=-1, keepdims=True)
        acc = _dot(p_c.astype(BF16), vbuf_ref[slot, h].astype(BF16)) + _dot(p_n.astype(BF16), vn_ref[:, hs])
        o_ref[:, hs] = acc / l


def _attn_sample(q, kb, vb, cache_k, cache_v, fc_row, fn_row, fn_col, batch, t_new, past):
    kernel = functools.partial(_attn_sample_kernel, t_new=t_new, past=past)
    tok = lambda w: pl.BlockSpec((t_new, w), lambda b: (b, 0))
    return pl.pallas_call(
        kernel,
        grid=(batch,),
        in_specs=[
            tok(ATTN_WIDTH), tok(ATTN_WIDTH), tok(ATTN_WIDTH),
            pl.BlockSpec(memory_space=pl.ANY),
            pl.BlockSpec(memory_space=pl.ANY),
            pl.BlockSpec((1, N_HEADS, past), lambda b: (b, 0, 0)),
            pl.BlockSpec((1, N_HEADS, LANES), lambda b: (b, 0, 0)),
            tok(N_HEADS),
        ],
        out_specs=tok(ATTN_WIDTH),
        out_shape=jax.ShapeDtypeStruct((batch * t_new, ATTN_WIDTH), F32),
        scratch_shapes=[pltpu.VMEM((2, N_HEADS, past, HEAD_DIM), F32),
                        pltpu.VMEM((2, N_HEADS, past, HEAD_DIM), F32),
                        pltpu.SemaphoreType.DMA((2, 2))],
        compiler_params=_params(("arbitrary",)),
        name="attn_sample",
    )(q, kb, vb, cache_k, cache_v, fc_row, fn_row, fn_col)


def _pack_bf16_pair(lo, hi):
    lo_bits = lax.bitcast_convert_type(lo, jnp.uint32) >> 16
    hi_bits = lax.bitcast_convert_type(hi, jnp.uint32) & jnp.uint32(0xFFFF0000)
    return lo_bits | hi_bits


def _unpack_f32_pair(w):
    lo = lax.bitcast_convert_type(w << 16, F32)
    hi = lax.bitcast_convert_type(w & jnp.uint32(0xFFFF0000), F32)
    return lo, hi


def _unpack_bf16_pair(w):
    lo, hi = _unpack_f32_pair(w)
    return lo.astype(BF16), hi.astype(BF16)


def _merge_kernel(yc_ref, ya_ref, x_ref, ga_ref, woc_ref, woa_ref, gf_ref, wr_ref, br_ref,
                  h_ref, xp_ref, lg_ref):
    ya = ya_ref[...]
    ya_n = (ya * lax.rsqrt(jnp.mean(ya * ya, axis=-1, keepdims=True) + EPS) * ga_ref[...]).astype(BF16)
    y = _dot(yc_ref[...], woc_ref[...]) + _dot(ya_n, woa_ref[...])
    h = x_ref[...] + y
    h_ref[...] = h
    xn = h * lax.rsqrt(jnp.mean(h * h, axis=-1, keepdims=True) + EPS) * gf_ref[...]
    xn_hi = xn.astype(BF16)
    xn_hi32 = xn_hi.astype(F32)
    xn_lo = (xn - xn_hi32).astype(BF16)
    hh_hl = _dot(xn_hi, wr_ref[...])
    lg_ref[...] = hh_hl[:, :LANES] + hh_hl[:, LANES:] + _dot(xn_lo, wr_ref[:, :LANES]) + br_ref[...]
    half = D_MODEL // 2
    xp_ref[...] = _pack_bf16_pair(xn_hi32[:, :half], xn_hi32[:, half:])


def _merge(yc_n, ya, x, ga, w_out, g_ffn, wr_cat, b_r, tm):
    n = x.shape[0]
    row = lambda w: pl.BlockSpec((tm, w), lambda i: (i, 0))
    return pl.pallas_call(
        _merge_kernel,
        grid=(n // tm,),
        in_specs=[row(CONV_WIDTH), row(ATTN_WIDTH), row(D_MODEL), _resident((1, ATTN_WIDTH)),
                  _resident((CONV_WIDTH, D_MODEL), (0, 0)), _resident((ATTN_WIDTH, D_MODEL), (1, 0)),
                  _resident((1, D_MODEL)),
                  _resident((D_MODEL, 2 * LANES)), _resident((1, LANES))],
        out_specs=(row(D_MODEL), row(D_MODEL // 2), row(LANES)),
        out_shape=(jax.ShapeDtypeStruct((n, D_MODEL), F32),
                   jax.ShapeDtypeStruct((n, D_MODEL // 2), jnp.uint32),
                   jax.ShapeDtypeStruct((n, LANES), F32)),
        compiler_params=_params(("parallel",)),
        name="merge_out",
    )(yc_n, ya, x, ga, w_out, w_out, g_ffn, wr_cat, b_r)


def _route_kernel(lg_ref, info_ref, infot_ref, cnt_ref, carry_ref, *, tm):
    step = pl.program_id(0)

    @pl.when(step == 0)
    def _():
        carry_ref[...] = jnp.zeros_like(carry_ref)

    lg = lg_ref[...]
    lane = lax.broadcasted_iota(jnp.int32, lg.shape, 1)
    lanef = lane.astype(F32)
    big = jnp.float32(1e9)
    rmax = lambda v: jnp.max(v, axis=-1, keepdims=True)
    rmin = lambda v: jnp.min(v, axis=-1, keepdims=True)
    rsum = lambda v: jnp.sum(v, axis=-1, keepdims=True)

    is_g = (lane >= N_EXPERTS) & (lane < N_EXPERTS + N_GROUPS)
    gl = jnp.where(is_g, lg, NEG_BIG)
    gmax = rmax(gl)
    gsum = rsum(jnp.where(is_g, jnp.exp(gl - gmax), 0.0))
    pg_star = 1.0 / gsum
    g_idx = rmin(jnp.where(is_g & (gl == gmax), lanef - N_EXPERTS, big))

    e_lo = g_idx * EXPERTS_PER_GROUP
    is_e = (lanef >= e_lo) & (lanef < e_lo + EXPERTS_PER_GROUP)
    el = jnp.where(is_e, lg, NEG_BIG)
    m1 = rmax(el)
    i1 = rmin(jnp.where(is_e & (el == m1), lanef, big))
    sel1 = lanef == i1
    el2 = jnp.where(sel1, NEG_BIG, el)
    m2 = rmax(el2)
    i2 = rmin(jnp.where(is_e & (el2 == m2) & jnp.logical_not(sel1), lanef, big))
    sel2 = lanef == i2
    z = rsum(jnp.where(is_e, jnp.exp(el - m1), 0.0))
    p1 = 1.0 / z
    p2 = jnp.exp(m2 - m1) / z
    gate1 = pg_star * p1 / (p1 + p2)
    gate2 = pg_star * p2 / (p1 + p2)

    onehot = jnp.where(sel1 | sel2, 1.0, 0.0)
    r = lax.broadcasted_iota(jnp.int32, (tm, tm), 0)
    c = lax.broadcasted_iota(jnp.int32, (tm, tm), 1)
    tri = jnp.where(c < r, 1.0, 0.0).astype(BF16)
    before = _dot(tri, onehot.astype(BF16)) + carry_ref[...]
    rank1 = rsum(jnp.where(sel1, before, 0.0))
    rank2 = rsum(jnp.where(sel2, before, 0.0))
    carry_ref[...] = carry_ref[...] + jnp.sum(onehot, axis=0, keepdims=True)
    cnt_ref[...] = carry_ref[...]

    info = jnp.zeros_like(lg)
    for k, val in enumerate((i1, i2, rank1, rank2, gate1, gate2)):
        info = jnp.where(lane == k, val, info)
    info_ref[...] = info
    infot_ref[...] = info.T[0:8, :]


def _route(logits, tm):
    n = logits.shape[0]
    kernel = functools.partial(_route_kernel, tm=tm)
    return pl.pallas_call(
        kernel,
        grid=(n // tm,),
        in_specs=[pl.BlockSpec((tm, LANES), lambda i: (i, 0))],
        out_specs=(pl.BlockSpec((tm, LANES), lambda i: (i, 0)),
                   pl.BlockSpec((8, tm), lambda i: (0, i)),
                   pl.BlockSpec((1, LANES), lambda i: (0, 0))),
        out_shape=(jax.ShapeDtypeStruct((n, LANES), F32), jax.ShapeDtypeStruct((8, n), F32),
                   jax.ShapeDtypeStruct((1, LANES), F32)),
        scratch_shapes=[pltpu.VMEM((1, LANES), F32)],
        compiler_params=_params(("arbitrary",)),
        name="route",
    )(logits)


DISPATCH_CHUNK = 128


def _dispatch_kernel(dest_ref, zflag_ref, xa_ref, xb_ref, xs_hbm, zero_ref, sem, zsem):
    n_all = dest_ref.shape[0] // 2

    zero_ref[...] = jnp.zeros_like(zero_ref)

    def zero_copy(blk):
        return pltpu.make_async_copy(zero_ref, xs_hbm.at[pl.ds(blk * MOE_BLOCK, MOE_BLOCK)], zsem)

    def start_zero(blk, carry):
        @pl.when(zflag_ref[blk] != 0)
        def _():
            zero_copy(blk).start()
        return carry

    def wait_zero(blk, carry):
        @pl.when(zflag_ref[blk] != 0)
        def _():
            zero_copy(blk).wait()
        return carry

    lax.fori_loop(0, zflag_ref.shape[0], start_zero, 0)
    lax.fori_loop(0, zflag_ref.shape[0], wait_zero, 0)

    def scatter(src_ref, t_off):
        def row_copy(t, k):
            return pltpu.make_async_copy(src_ref.at[pl.ds(t, 1)],
                                         xs_hbm.at[pl.ds(dest_ref[k * n_all + t_off + t], 1)], sem)

        def issue(c):
            def body(r, carry):
                row_copy(c * DISPATCH_CHUNK + r, 0).start(priority=0)
                row_copy(c * DISPATCH_CHUNK + r, 1).start(priority=1)
                return carry
            lax.fori_loop(0, DISPATCH_CHUNK, body, 0, unroll=8)

        def drain(c):
            def body(r, carry):
                row_copy(c * DISPATCH_CHUNK + r, 0).wait()
                row_copy(c * DISPATCH_CHUNK + r, 1).wait()
                return carry
            lax.fori_loop(0, DISPATCH_CHUNK, body, 0, unroll=8)

        n_chunks = src_ref.shape[0] // DISPATCH_CHUNK

        def chunk(c, carry):
            issue(c)

            @pl.when(c > 0)
            def _():
                drain(c - 1)
            return carry

        lax.fori_loop(0, n_chunks, chunk, 0)
        drain(n_chunks - 1)

    scatter(xa_ref, 0)
    scatter(xb_ref, xa_ref.shape[0])


def _dispatch(dest_flat, zero_flag, xa, xb):
    w = xa.shape[1]
    n_slots = zero_flag.shape[0] * MOE_BLOCK
    vmem = pl.BlockSpec(memory_space=pltpu.VMEM)
    grid_spec = pltpu.PrefetchScalarGridSpec(
        num_scalar_prefetch=2,
        grid=(1,),
        in_specs=[vmem, vmem],
        out_specs=pl.BlockSpec(memory_space=pl.ANY),
        scratch_shapes=[pltpu.VMEM((MOE_BLOCK, w), xa.dtype), pltpu.SemaphoreType.DMA(()),
                        pltpu.SemaphoreType.DMA(())],
    )
    return pl.pallas_call(
        _dispatch_kernel,
        grid_spec=grid_spec,
        out_shape=jax.ShapeDtypeStruct((n_slots, w), xa.dtype),
        compiler_params=pltpu.CompilerParams(dimension_semantics=("arbitrary",),
                                             vmem_limit_bytes=VMEM_LIMIT, has_side_effects=True),
        name="dispatch",
    )(dest_flat, zero_flag, xa, xb)


CAST_CHUNK_ELEMS = 64 * 1024
WEIGHT_DMA_SPLIT = 4
WEIGHT_DMA_PRIORITY = 1


def _expert_kernel(be_ref, nu_ref, nx_ref, xs_ref, w1_hbm, w3_hbm, w2_hbm, y_ref,
                   w1f_ref, w3f_ref, w2f_ref, w1b_ref, w3b_ref, w2b_ref, sem):
    b = pl.program_id(0)
    active = b < nu_ref[0]
    new_expert = jnp.logical_or(b == 0, be_ref[b] != be_ref[jnp.maximum(b - 1, 0)])

    def fetch(e):
        copies = []
        for k, (src, dst) in enumerate(((w1_hbm, w1f_ref), (w3_hbm, w3f_ref), (w2_hbm, w2f_ref))):
            slab = dst.shape[0] // WEIGHT_DMA_SPLIT
            for c in range(WEIGHT_DMA_SPLIT):
                rows = pl.ds(c * slab, slab)
                copies.append(pltpu.make_async_copy(src.at[e, rows], dst.at[rows], sem.at[k]))
        return copies

    @pl.when(b == 0)
    def _():
        for copy in fetch(be_ref[0]):
            copy.start(priority=WEIGHT_DMA_PRIORITY)

    @pl.when(jnp.logical_and(active, new_expert))
    def _():
        for copy in fetch(be_ref[b]):
            copy.wait()
        for src, dst in ((w1f_ref, w1b_ref), (w3f_ref, w3b_ref), (w2f_ref, w2b_ref)):
            rows, cols = src.shape
            chunk = CAST_CHUNK_ELEMS // cols

            def cast_rows(c, carry, src=src, dst=dst, chunk=chunk):
                r0 = pl.multiple_of(c * chunk, chunk)
                dst[pl.ds(r0, chunk), :] = src[pl.ds(r0, chunk), :].astype(BF16)
                return carry

            lax.fori_loop(0, rows // chunk, cast_rows, 0)

        @pl.when(nx_ref[b] >= 0)
        def _():
            for copy in fetch(nx_ref[b]):
                copy.start(priority=WEIGHT_DMA_PRIORITY)

    @pl.when(active)
    def _():
        lo, hi = _unpack_bf16_pair(xs_ref[...])
        half = D_MODEL // 2
        h1 = _dot(lo, w1b_ref[:half, :]) + _dot(hi, w1b_ref[half:, :])
        h3 = _dot(lo, w3b_ref[:half, :]) + _dot(hi, w3b_ref[half:, :])
        h = (h1 * jax.nn.sigmoid(h1) * h3).astype(BF16)
        y = _dot(h, w2b_ref[...]).astype(BF16).astype(F32)
        y_ref[...] = _pack_bf16_pair(y[:, :half], y[:, half:])

    @pl.when(jnp.logical_not(active))
    def _():
        y_ref[...] = jnp.zeros_like(y_ref)


def _experts(block_e, n_used, next_e, xs, w1, w3, w2):
    n_slots = xs.shape[0]
    nb = n_slots // MOE_BLOCK
    blk = lambda b, be, nu, nx: (jnp.minimum(b, nu[0] - 1), 0)
    hbm = pl.BlockSpec(memory_space=pl.ANY)
    grid_spec = pltpu.PrefetchScalarGridSpec(
        num_scalar_prefetch=3,
        grid=(nb,),
        in_specs=[pl.BlockSpec((MOE_BLOCK, D_MODEL // 2), blk), hbm, hbm, hbm],
        out_specs=pl.BlockSpec((MOE_BLOCK, D_MODEL // 2), lambda b, be, nu, nx: (b, 0)),
        scratch_shapes=[pltpu.VMEM((D_MODEL, D_EXPERT), F32), pltpu.VMEM((D_MODEL, D_EXPERT), F32),
                        pltpu.VMEM((D_EXPERT, D_MODEL), F32),
                        pltpu.VMEM((D_MODEL, D_EXPERT), BF16), pltpu.VMEM((D_MODEL, D_EXPERT), BF16),
                        pltpu.VMEM((D_EXPERT, D_MODEL), BF16),
                        pltpu.SemaphoreType.DMA((3,))],
    )
    return pl.pallas_call(
        _expert_kernel,
        grid_spec=grid_spec,
        out_shape=jax.ShapeDtypeStruct((n_slots, D_MODEL // 2), jnp.uint32),
        compiler_params=_params(("arbitrary",)),
        name="experts",
    )(block_e, n_used, next_e, xs, w1, w3, w2)


PLE_SPLIT = 4
GATHER_ISSUE_SLABS = 2


def _combine_ple_kernel(dest_ref, h_ref, info_ref, p_ref, g_ref, *rest, tm, t_off):
    wpg_refs = rest[:PLE_SPLIT]
    wpp_ref, yb_hbm, o_ref, buf_ref, sem = rest[PLE_SPLIT:]
    i = pl.program_id(0)
    last = pl.num_programs(0) - 1
    n_all = dest_ref.shape[0] // 2
    slot = i % 2

    def row_copy(step, into, r, k):
        t = t_off + step * tm + r
        return pltpu.make_async_copy(yb_hbm.at[pl.ds(dest_ref[k * n_all + t], 1)],
                                     buf_ref.at[into, k, pl.ds(r, 1)], sem.at[into])

    def drain(into):
        def body(r, carry):
            for k in range(2):
                pltpu.make_async_copy(yb_hbm.at[pl.ds(0, 1)], buf_ref.at[into, k, pl.ds(r, 1)],
                                      sem.at[into]).wait()
            return carry
        lax.fori_loop(0, tm, body, 0, unroll=8)

    @pl.when(i == 0)
    def _():
        def body(r, carry):
            row_copy(0, 0, r, 0).start()
            row_copy(0, 0, r, 1).start()
            return carry
        lax.fori_loop(0, tm, body, 0, unroll=8)

    drain(slot)
    lo0, hi0 = _unpack_f32_pair(buf_ref[slot, 0])
    lo1, hi1 = _unpack_f32_pair(buf_ref[slot, 1])
    g0, g1 = info_ref[:, 4:5], info_ref[:, 5:6]
    h2 = h_ref[...] + jnp.concatenate([g0 * lo0 + g1 * lo1, g0 * hi0 + g1 * hi1], axis=1)
    hn = (h2 * lax.rsqrt(jnp.mean(h2 * h2, axis=-1, keepdims=True) + EPS) * g_ref[...]).astype(BF16)
    pb = p_ref[...].astype(BF16)

    nxt = jnp.minimum(i + 1, last)
    ch = D_MODEL // PLE_SPLIT
    per = tm // GATHER_ISSUE_SLABS
    for k in range(PLE_SPLIT):
        for r in range(k * per, (k + 1) * per if k < GATHER_ISSUE_SLABS else k * per):
            row_copy(nxt, 1 - slot, r, 0).start(priority=0)
            row_copy(nxt, 1 - slot, r, 1).start(priority=1)
        cs = slice(k * ch, (k + 1) * ch)
        gate = jax.nn.sigmoid(_dot(hn, wpg_refs[k][...]))
        o_ref[:, cs] = h2[:, cs] + gate * _dot(pb, wpp_ref[:, cs])

    @pl.when(i == last)
    def _():
        drain(1 - slot)


def _combine_ple(dest_flat, h, info, yb, p, g_ple, w_pg, w_pp, t_off, tm):
    n = h.shape[0]
    kernel = functools.partial(_combine_ple_kernel, tm=tm, t_off=t_off)
    ob = t_off // tm
    row = lambda w: pl.BlockSpec((tm, w), lambda i, d: (i, 0))
    grid_spec = pltpu.PrefetchScalarGridSpec(
        num_scalar_prefetch=1,
        grid=(n // tm,),
        in_specs=[row(D_MODEL), pl.BlockSpec((tm, LANES), lambda i, d: (i + ob, 0)), row(PLE_DIM),
                  _resident((1, D_MODEL))]
        + [_resident((D_MODEL, D_MODEL // PLE_SPLIT), (0, k)) for k in range(PLE_SPLIT)]
        + [_resident((PLE_DIM, D_MODEL)), pl.BlockSpec(memory_space=pl.ANY)],
        out_specs=row(D_MODEL),
        scratch_shapes=[pltpu.VMEM((2, 2, tm, D_MODEL // 2), jnp.uint32), pltpu.SemaphoreType.DMA((2,))],
    )
    return pl.pallas_call(
        kernel,
        grid_spec=grid_spec,
        out_shape=jax.ShapeDtypeStruct((n, D_MODEL), F32),
        compiler_params=_params(("arbitrary",)),
        name="combine_ple",
    )(dest_flat, h, info, p, g_ple, *([w_pg] * PLE_SPLIT), w_pp, yb)


def _mixer_tokens(x2d, wts, tm):
    return _in_proj(x2d, wts["g_mix"], wts["w_parts"], wts["w_f"], wts["b_f"],
                    wts["q_gain"], wts["k_gain"], tm)


def kernel(x_prompt, x_sample, cache_k, cache_v, cache_logf, cache_conv, p_prompt, p_sample,
           g_mix, w_in, b_f, q_gain, k_gain, w_dw, b_dw, ln_g, ln_b, gc, ga, w_out,
           g_ffn, w_router_g, b_router_g, w_router_e, b_router_e, w1, w3, w2,
           g_ple, w_pg, w_pp):
    batch, seq, _ = x_prompt.shape
    dec_batch, t_new, _ = x_sample.shape
    past = cache_k.shape[2]
    n_p = batch * seq
    n_s = dec_batch * t_new
    n_all = n_p + n_s
    tm = 256
    li = 0

    w_in_l = w_in[li]
    pad_lanes = lambda a: jnp.pad(a, ((0, 0), (0, LANES - a.shape[1])))
    row2d = lambda a: a.reshape(1, -1)
    wts = {
        "g_mix": row2d(g_mix[li]),
        "w_parts": [w_in_l[:, c:c + CONV_WIDTH].astype(BF16) for c in range(0, MAIN_COLS, CONV_WIDTH)],
        "w_f": pad_lanes(w_in_l[:, MAIN_COLS:]).astype(BF16),
        "b_f": pad_lanes(row2d(b_f[li])),
        "q_gain": row2d(q_gain[li]),
        "k_gain": row2d(k_gain[li]),
    }
    w_dw_p = jnp.pad(w_dw[li], ((0, CONV_HALO - CONV_KERNEL), (0, 0)))
    w_out_b = w_out[li].astype(BF16)
    w_r = pad_lanes(jnp.concatenate([w_router_e[li], w_router_g[li]], axis=1))
    wr_hi = w_r.astype(BF16)
    wr_lo = (w_r - wr_hi.astype(F32)).astype(BF16)
    b_r = pad_lanes(row2d(jnp.concatenate([b_router_e[li], b_router_g[li]])))
    w_pg_b, w_pp_b = w_pg[li].astype(BF16), w_pp[li].astype(BF16)

    xp = x_prompt.reshape(n_p, D_MODEL)
    u_p, q_p, k_p, kb_p, v_p, _, vbt_p, lf_p = _mixer_tokens(xp, wts, 2 * tm)
    lf_p_row = lf_p.reshape(batch, seq, N_HEADS).transpose(0, 2, 1)
    f_p_row = _cumsum_lanes(lf_p_row.reshape(batch * N_HEADS, seq))
    ctx_p = jnp.zeros((batch, CONV_HALO, CONV_WIDTH), F32)
    conv_args = (w_dw_p, row2d(b_dw[li]), row2d(ln_g[li]), row2d(ln_b[li]), row2d(gc[li]))
    yc_p = _conv_module(u_p, ctx_p, *conv_args, batch, seq, 2 * tm)
    qa_p, ka_p = _attn_aug(f_p_row, batch, seq, 2 * tm)
    ya_p = _attn_prompt(q_p, qa_p, kb_p, ka_p, vbt_p, batch, seq, 256, 256)
    merge_args = (row2d(ga[li]), w_out_b, row2d(g_ffn[li]), jnp.concatenate([wr_hi, wr_lo], axis=1), b_r)
    h_p, xpk_p, lg_p = _merge(yc_p, ya_p, xp, *merge_args, 2 * tm)

    xs_ = x_sample.reshape(n_s, D_MODEL)
    u_s, q_s, k_s, kb_s, v_s, vb_s, _, lf_s = _mixer_tokens(xs_, wts, tm)
    clf_row = cache_logf[li].transpose(0, 2, 1).reshape(dec_batch * N_HEADS, past)
    fc_row = _cumsum_lanes(clf_row).reshape(dec_batch, N_HEADS, past)
    lf_s_row = lf_s.reshape(dec_batch, t_new, N_HEADS).transpose(0, 2, 1).reshape(dec_batch * N_HEADS, t_new)
    fn_row = _cumsum_lanes(jnp.pad(lf_s_row, ((0, 0), (0, LANES - t_new)))).reshape(dec_batch, N_HEADS, LANES)
    fn_col = fn_row[:, :, :t_new].transpose(0, 2, 1).reshape(n_s, N_HEADS)
    ctx_s = jnp.pad(cache_conv[li], ((0, 0), (CONV_HALO - CONV_STATE, 0), (0, 0)))
    yc_s = _conv_module(u_s, ctx_s, *conv_args, dec_batch, t_new, t_new)
    ya_s = _attn_sample(q_s, kb_s, vb_s,
                        cache_k[li], cache_v[li],
                        fc_row, fn_row, fn_col, dec_batch, t_new, past)
    h_s, xpk_s, lg_s = _merge(yc_s, ya_s, xs_, *merge_args, tm)

    info, info_t, counts = _route(jnp.concatenate([lg_p, lg_s], axis=0), tm)
    counts = counts[0, :N_EXPERTS].astype(jnp.int32)
    bcounts = (counts + MOE_BLOCK - 1) // MOE_BLOCK
    bends = jnp.cumsum(bcounts)
    pstarts = (bends - bcounts) * MOE_BLOCK
    n_rows = n_all * 2
    nb = -(-(n_rows + N_EXPERTS * (MOE_BLOCK - 1)) // MOE_BLOCK)
    e_idx = info_t[0:2].astype(jnp.int32)
    expert_ids = jnp.arange(N_EXPERTS, dtype=jnp.int32)[:, None, None]
    seg_start = jnp.sum(jnp.where(e_idx[None] == expert_ids, pstarts[:, None, None], 0), axis=0)
    dest = (seg_start + info_t[2:4].astype(jnp.int32)).reshape(n_rows)
    block_e = jnp.minimum(jnp.sum(bends[None, :] <= jnp.arange(nb, dtype=jnp.int32)[:, None], axis=1),
                          N_EXPERTS - 1).astype(jnp.int32)
    n_used = bends[N_EXPERTS - 1:].astype(jnp.int32)
    seg_end = jnp.sum(jnp.where(block_e[None, :] == expert_ids[:, :, 0], bends[:, None], 0), axis=0)
    block_ids = jnp.arange(nb, dtype=jnp.int32)
    zero_flag = ((block_ids + 1 == seg_end) | (block_ids >= n_used[0])).astype(jnp.int32)
    xs_sorted = _dispatch(dest, zero_flag, xpk_p, xpk_s)
    next_e = jnp.where(seg_end < n_used[0], block_e[jnp.minimum(seg_end, nb - 1)], -1).astype(jnp.int32)
    yb = _experts(block_e, n_used, next_e, xs_sorted, w1[li], w3[li], w2[li])

    ple_args = (row2d(g_ple[li]), w_pg_b, w_pp_b)
    y_p = _combine_ple(dest, h_p, info, yb, p_prompt[li].reshape(n_p, PLE_DIM), *ple_args, 0, tm)
    y_s = _combine_ple(dest, h_s, info, yb, p_sample[li].reshape(n_s, PLE_DIM), *ple_args, n_p, tm)

    heads = lambda a, b, t: a.reshape(1, b, t, N_HEADS, HEAD_DIM)
    return (
        y_p.reshape(batch, seq, D_MODEL),
        y_s.reshape(dec_batch, t_new, D_MODEL),
        heads(k_p, batch, seq), heads(v_p, batch, seq),
        lf_p.reshape(1, batch, seq, N_HEADS),
        u_p.reshape(batch, seq, CONV_WIDTH)[None, :, seq - CONV_STATE:, :],
        heads(k_s, dec_batch, t_new), heads(v_s, dec_batch, t_new),
        lf_s.reshape(1, dec_batch, t_new, N_HEADS),
        u_s.reshape(dec_batch, t_new, CONV_WIDTH)[None, :, t_new - CONV_STATE:, :],
    )
```
